```python
import jax, jax.numpy as jnp
from jax import lax
import numpy as np

D_MODEL = 1024
BATCH = 16
SEQ = 4096
DEPTH = 2
DEC_BATCH = 16
DEC_SEQ = 32
PAST_LEN = 2048

CHUNK = 64
QBLK = 128
RBLK = 16
N_AB = (DEPTH + 1) // 2
N_CD = DEPTH // 2

A_HEADS = 4
A_DK = 128
A_DV = 128
B_HEADS = 4
B_DK = 64
B_DV = 128
B_GATE_RANK = 16
B_GATE_NORM = 16.0
C_HEADS = 8
C_HD = 64
D_HEADS = 8
D_KV_HEADS = 2
D_HD = 64
IDX_HEADS = 4
IDX_DIM = 64
IDX_TOPK_MAX = 256
IDX_SCALE = (IDX_DIM ** -0.5) * (IDX_HEADS ** -0.5)
FFN_HIDDEN = ((8 * D_MODEL // 3 + 255) // 256) * 256

AB_SIZES = (A_HEADS * A_DK, A_HEADS * A_DK, A_HEADS * A_DV, A_HEADS * A_DV,
            B_HEADS * B_DK, B_HEADS * B_DK, B_HEADS * B_DV, B_HEADS * B_DV, B_GATE_RANK)
AB_IN = sum(AB_SIZES)
AB_MIX = A_HEADS * A_DV + B_HEADS * B_DV
CD_SIZES = (C_HEADS * C_HD, C_HEADS * C_HD, C_HEADS * C_HD, C_HEADS,
            D_HEADS * D_HD, D_KV_HEADS * D_HD, D_KV_HEADS * D_HD, IDX_HEADS * IDX_DIM, IDX_DIM, IDX_HEADS)
CD_IN = sum(CD_SIZES)
CD_MIX = C_HEADS * C_HD + D_HEADS * D_HD
F32 = jnp.float32

kernel_name = "hybrid_stream_hgrn2_gla_fox_dsa"


def rmsnorm(x, g, eps=1e-6):
    xf = x.astype(F32)
    y = xf * lax.rsqrt(jnp.mean(xf * xf, axis=-1, keepdims=True) + eps) * g.astype(F32)
    return y.astype(x.dtype)


def split_cols(h, sizes):
    return jnp.split(h, [int(o) for o in np.cumsum(sizes)[:-1]], axis=-1)


def heads(z, n):
    return z.reshape(z.shape[:-1] + (n, z.shape[-1] // n))


def to_blocks(z, nb, blk):
    return z.reshape((z.shape[0], nb, blk) + z.shape[2:]).swapaxes(0, 1)


def from_blocks(z):
    return z.swapaxes(0, 1).reshape((z.shape[1], z.shape[0] * z.shape[2]) + z.shape[3:])


def gated_linear_recurrence(q, k, v, log_a, s0):
    t_len = q.shape[1]
    n_blk = -(-t_len // RBLK)
    pad = ((0, 0), (0, n_blk * RBLK - t_len), (0, 0), (0, 0))
    qb, kb, vb, ab = (to_blocks(jnp.pad(z.astype(F32), pad), n_blk, RBLK) for z in (q, k, v, log_a))
    causal = jnp.tril(jnp.ones((RBLK, RBLK), dtype=bool))

    def step(s, blk):
        qc, kc, vc, ac = blk
        b = jnp.cumsum(ac, axis=1)
        b_last = b[:, -1]
        q_dec = qc * jnp.exp(b)
        att = jnp.einsum('blhk,bmhk->bhlm', q_dec, kc * jnp.exp(-b))
        att = jnp.where(causal, att, 0.0)
        o = jnp.einsum('blhk,bhkv->blhv', q_dec, s) + jnp.einsum('bhlm,bmhv->blhv', att, vc)
        s_new = jnp.exp(b_last)[..., None] * s + jnp.einsum(
            'blhk,blhv->bhkv', kc * jnp.exp(b_last[:, None] - b), vc)
        return s_new, o

    s_final, o = lax.scan(step, s0.astype(F32), (qb, kb, vb, ab))
    return from_blocks(o)[:, :t_len], s_final


def ab_mixer(xn, w_in, w_out, lower_bound, a_gnorm, w_gk, b_gk, b_gnorm, s_a, s_b):
    bsz, t_len = xn.shape[:2]
    h = jnp.einsum('btd,de->bte', xn, w_in)
    a_q, a_f, a_i, a_g, b_q, b_k, b_v, b_g, b_lr = split_cols(h, AB_SIZES)
    f = lower_bound + (1.0 - lower_bound) * jax.nn.sigmoid(a_f.astype(F32))
    o_a, s_a_new = gated_linear_recurrence(
        heads(jax.nn.silu(a_q), A_HEADS), heads(1.0 - f, A_HEADS), heads(a_i, A_HEADS),
        heads(jnp.log(f), A_HEADS), s_a)
    o_a = rmsnorm(o_a, a_gnorm) * jax.nn.silu(heads(a_g, A_HEADS).astype(F32))
    gk = jnp.einsum('btr,rk->btk', b_lr, w_gk) + b_gk
    log_alpha = jax.nn.log_sigmoid(gk.astype(F32)) / B_GATE_NORM
    o_b, s_b_new = gated_linear_recurrence(
        heads(b_q * (B_DK ** -0.5), B_HEADS), heads(b_k, B_HEADS), heads(b_v, B_HEADS),
        heads(log_alpha, B_HEADS), s_b)
    o_b = rmsnorm(o_b, b_gnorm) * jax.nn.silu(heads(b_g, B_HEADS).astype(F32))
    o = jnp.concatenate([o_a.reshape(bsz, t_len, -1), o_b.reshape(bsz, t_len, -1)], axis=-1).astype(xn.dtype)
    return jnp.einsum('bte,ed->btd', o, w_out), s_a_new, s_b_new


def fox_attention(q, k_all, v_all, logf_all, q_start):
    bsz, tq, h, hd = q.shape
    s_len = k_all.shape[1]
    qblk = min(QBLK, tq)
    nb = tq // qblk
    cum = jnp.cumsum(logf_all.astype(F32), axis=1)
    cum_q = lax.slice_in_dim(cum, q_start, q_start + tq, axis=1)
    cum_t = cum.transpose(0, 2, 1)
    key_pos = jnp.arange(s_len)
    q_pos = q_start + jnp.arange(tq)

    def block(args):
        qb, cqb, pb = args
        logits = jnp.einsum('bqhd,bshd->bhqs', qb, k_all).astype(F32) * (hd ** -0.5)
        logits = logits + cqb.transpose(0, 2, 1)[..., None] - cum_t[:, :, None, :]
        logits = jnp.where(key_pos[None, None, None, :] <= pb[None, None, :, None], logits, -jnp.inf)
        p = jax.nn.softmax(logits, axis=-1)
        return jnp.einsum('bhqs,bshd->bqhd', p.astype(v_all.dtype), v_all)

    out = lax.map(block, (to_blocks(q, nb, qblk), to_blocks(cum_q, nb, qblk), q_pos.reshape(nb, qblk)))
    return from_blocks(out)


def dsa_attention(q, k_all, v_all, iq, iw, ik_all, q_start):
    bsz, tq, h, hd = q.shape
    s_len, g = k_all.shape[1], k_all.shape[2]
    topk = min(IDX_TOPK_MAX, s_len // 4)
    qblk = min(QBLK, tq)
    nb = tq // qblk
    key_chunk = jnp.arange(s_len) // CHUNK
    q_pos = q_start + jnp.arange(tq)
    gather_rows = jax.vmap(lambda rows, idx: rows[idx])

    def block(args):
        qb, iqb, iwb, pb = args
        q_chunk = pb // CHUNK
        score = jax.nn.relu(jnp.einsum('bqhd,bsd->bqhs', iqb, ik_all).astype(F32))
        score = jnp.einsum('bqhs,bqh->bqs', score, iwb.astype(F32)) * IDX_SCALE
        admissible = key_chunk[None, :] <= q_chunk[:, None]
        score = jnp.where(admissible[None], score, -jnp.inf)
        _, idx = lax.top_k(score, topk)
        valid = key_chunk[idx] <= q_chunk[None, :, None]
        k_sel = gather_rows(k_all, idx)
        v_sel = gather_rows(v_all, idx)
        qg = qb.reshape(bsz, qblk, g, h // g, hd)
        logits = jnp.einsum('bqgnd,bqkgd->bqgnk', qg, k_sel).astype(F32) * (hd ** -0.5)
        logits = jnp.where(valid[:, :, None, None, :], logits, -jnp.inf)
        p = jax.nn.softmax(logits, axis=-1)
        o = jnp.einsum('bqgnk,bqkgd->bqgnd', p.astype(v_sel.dtype), v_sel)
        return o.reshape(bsz, qblk, h, hd)

    out = lax.map(block, (to_blocks(q, nb, qblk), to_blocks(iq, nb, qblk), to_blocks(iw, nb, qblk),
                          q_pos.reshape(nb, qblk)))
    return from_blocks(out)


def cd_mixer(xn, w_in, f_bias, w_out, cache):
    bsz, t_len = xn.shape[:2]
    h = jnp.einsum('btd,de->bte', xn, w_in)
    c_q, c_k, c_v, c_f, d_q, d_k, d_v, d_iq, d_ik, d_iw = split_cols(h, CD_SIZES)
    c_q, c_k, c_v = heads(c_q, C_HEADS), heads(c_k, C_HEADS), heads(c_v, C_HEADS)
    c_logf = jax.nn.log_sigmoid(c_f.astype(F32) + f_bias.astype(F32))
    d_q, d_k, d_v = heads(d_q, D_HEADS), heads(d_k, D_KV_HEADS), heads(d_v, D_KV_HEADS)
    d_iq = heads(d_iq, IDX_HEADS)
    rows = (c_k, c_v, c_logf, d_k, d_v, d_ik)
    if cache is None:
        q_start, keys = 0, rows
    else:
        q_start = cache[0].shape[1]
        keys = [jnp.concatenate([c.astype(r.dtype), r], axis=1) for c, r in zip(cache, rows)]
    fk, fv, flogf, dk, dv, dik = keys
    o_c = fox_attention(c_q, fk, fv, flogf, q_start)
    o_d = dsa_attention(d_q, dk, dv, d_iq, d_iw, dik, q_start)
    o = jnp.concatenate([o_c.reshape(bsz, t_len, -1), o_d.reshape(bsz, t_len, -1)], axis=-1).astype(xn.dtype)
    return jnp.einsum('bte,ed->btd', o, w_out), rows


def swiglu(xn, w_in, w_out):
    gate, up = jnp.split(jnp.einsum('btd,df->btf', xn, w_in), 2, axis=-1)
    return jnp.einsum('btf,fd->btd', jax.nn.silu(gate) * up, w_out)


def run_trunk(x, hgrn_s, gla_s, cd_cache, norm_mix, norm_ffn, norm_final, ab_w_in, ab_w_out,
              hgrn_lb_logits, hgrn_gnorm, gla_w_gk, gla_b_gk, gla_gnorm, cd_w_in, cd_w_out,
              fox_f_bias, ffn_w_in, ffn_w_out):
    lbs = jnp.cumsum(jax.nn.softmax(hgrn_lb_logits.astype(F32), axis=0), axis=0)
    new_hgrn, new_gla, new_cd = [], [], []
    for layer in range(DEPTH):
        li = layer // 2
        xn = rmsnorm(x, norm_mix[layer])
        if layer % 2 == 0:
            y, sa, sb = ab_mixer(xn, ab_w_in[li], ab_w_out[li], lbs[li], hgrn_gnorm[li], gla_w_gk[li],
                                 gla_b_gk[li], gla_gnorm[li], hgrn_s[li], gla_s[li])
            new_hgrn.append(sa)
            new_gla.append(sb)
        else:
            cache = None if cd_cache is None else [c[li] for c in cd_cache]
            y, rows = cd_mixer(xn, cd_w_in[li], fox_f_bias[li], cd_w_out[li], cache)
            new_cd.append(rows)
        x = x + y
        x = x + swiglu(rmsnorm(x, norm_ffn[layer]), ffn_w_in[layer], ffn_w_out[layer])
    cd_rows = [jnp.stack([r[j] for r in new_cd]) for j in range(6)]
    return (rmsnorm(x, norm_final), jnp.stack(new_hgrn), jnp.stack(new_gla), *cd_rows)


def setup_inputs(seed: int = 0) -> dict:
    key = jax.random.key(seed)
    ks = iter(jax.random.split(key, 32))

    def nrm(shape, scale=1.0, offset=0.0):
        return offset + scale * jax.random.normal(next(ks), shape, dtype=F32)

    return {
        'x_prompt': nrm((BATCH, SEQ, D_MODEL)),
        'x_sample': nrm((DEC_BATCH, DEC_SEQ, D_MODEL)),
        'state_hgrn': nrm((N_AB, DEC_BATCH, A_HEADS, A_DK, A_DV), 0.5),
        'state_gla': nrm((N_AB, DEC_BATCH, B_HEADS, B_DK, B_DV), 1.0),
        'cache_fox_k': nrm((N_CD, DEC_BATCH, PAST_LEN, C_HEADS, C_HD)),
        'cache_fox_v': nrm((N_CD, DEC_BATCH, PAST_LEN, C_HEADS, C_HD)),
        'cache_fox_logf': jax.nn.log_sigmoid(nrm((N_CD, DEC_BATCH, PAST_LEN, C_HEADS), 0.5, 2.0)),
        'cache_dsa_k': nrm((N_CD, DEC_BATCH, PAST_LEN, D_KV_HEADS, D_HD)),
        'cache_dsa_v': nrm((N_CD, DEC_BATCH, PAST_LEN, D_KV_HEADS, D_HD)),
        'cache_dsa_ik': nrm((N_CD, DEC_BATCH, PAST_LEN, IDX_DIM)),
        'norm_mix': nrm((DEPTH, D_MODEL), 0.05, 1.0),
        'norm_ffn': nrm((DEPTH, D_MODEL), 0.05, 1.0),
        'norm_final': nrm((D_MODEL,), 0.05, 1.0),
        'ab_w_in': nrm((N_AB, D_MODEL, AB_IN), D_MODEL ** -0.5),
        'ab_w_out': nrm((N_AB, AB_MIX, D_MODEL), AB_MIX ** -0.5),
        'hgrn_lb_logits': nrm((N_AB + 1, A_HEADS * A_DK), 0.5),
        'hgrn_gnorm': nrm((N_AB, A_DV), 0.05, 1.0),
        'gla_w_gk': nrm((N_AB, B_GATE_RANK, B_HEADS * B_DK), B_GATE_RANK ** -0.5),
        'gla_b_gk': nrm((N_AB, B_HEADS * B_DK), 0.1),
        'gla_gnorm': nrm((N_AB, B_DV), 0.05, 1.0),
        'cd_w_in': nrm((N_CD, D_MODEL, CD_IN), D_MODEL ** -0.5),
        'cd_w_out': nrm((N_CD, CD_MIX, D_MODEL), CD_MIX ** -0.5),
        'fox_f_bias': nrm((N_CD, C_HEADS), 0.1, 2.0),
        'ffn_w_in': nrm((DEPTH, D_MODEL, 2 * FFN_HIDDEN), D_MODEL ** -0.5),
        'ffn_w_out': nrm((DEPTH, FFN_HIDDEN, D_MODEL), FFN_HIDDEN ** -0.5),
    }


def reference(x_prompt, x_sample, state_hgrn, state_gla, cache_fox_k, cache_fox_v, cache_fox_logf,
              cache_dsa_k, cache_dsa_v, cache_dsa_ik, norm_mix, norm_ffn, norm_final, ab_w_in, ab_w_out,
              hgrn_lb_logits, hgrn_gnorm, gla_w_gk, gla_b_gk, gla_gnorm, cd_w_in, cd_w_out, fox_f_bias,
              ffn_w_in, ffn_w_out):
    weights = (norm_mix, norm_ffn, norm_final, ab_w_in, ab_w_out, hgrn_lb_logits, hgrn_gnorm, gla_w_gk,
               gla_b_gk, gla_gnorm, cd_w_in, cd_w_out, fox_f_bias, ffn_w_in, ffn_w_out)
    bp = x_prompt.shape[0]
    zero_hgrn = jnp.zeros((N_AB, bp, A_HEADS, A_DK, A_DV), F32)
    zero_gla = jnp.zeros((N_AB, bp, B_HEADS, B_DK, B_DV), F32)
    (y_prompt, p_hgrn, p_gla, p_fox_k, p_fox_v, p_fox_logf, p_dsa_k, p_dsa_v, p_dsa_ik) = run_trunk(
        x_prompt, zero_hgrn, zero_gla, None, *weights)
    cd_cache = (cache_fox_k, cache_fox_v, cache_fox_logf, cache_dsa_k, cache_dsa_v, cache_dsa_ik)
    (y_sample, s_hgrn, s_gla, s_fox_k, s_fox_v, s_fox_logf, s_dsa_k, s_dsa_v, s_dsa_ik) = run_trunk(
        x_sample, state_hgrn, state_gla, cd_cache, *weights)
    return (y_prompt, y_sample, p_hgrn, p_gla, p_fox_k, p_fox_v, p_fox_logf, p_dsa_k, p_dsa_v, p_dsa_ik,
            s_hgrn, s_gla, s_fox_k, s_fox_v, s_fox_logf, s_dsa_k, s_dsa_v, s_dsa_ik)
```

```python
import functools

import numpy as np
import jax
import jax.numpy as jnp
from jax import lax
from jax.experimental import pallas as pl
from jax.experimental.pallas import tpu as pltpu

F32 = jnp.float32
BF16 = jnp.bfloat16
I32 = jnp.int32

D_MODEL = 1024
CHUNK = 64
A_HEADS, A_DK, A_DV = 4, 128, 128
B_HEADS, B_DK, B_DV = 4, 64, 128
B_GATE_RANK = 16
B_GATE_NORM = 16.0
C_HEADS, C_HD = 8, 64
D_HEADS, D_KV_HEADS, D_HD = 8, 2, 64
IDX_HEADS, IDX_DIM = 4, 64
IDX_TOPK_MAX = 256
IDX_SCALE = (IDX_DIM ** -0.5) * (IDX_HEADS ** -0.5)
FFN_HIDDEN = ((8 * D_MODEL // 3 + 255) // 256) * 256

LANES = 128
RSUB = 16
REC_CHUNK = 128
VMEM_LIMIT = 56 * 1024 * 1024
NEG = -1e30
INT_MIN = -2 ** 31

AB_COLS = 4 * 512 + 256 + 256 + 512 + 512 + LANES
CD_COLS = 3 * 512 + 8 * LANES + 2 * LANES + 256 + LANES + LANES


def _dot(a, b):
    return jnp.dot(a, b, preferred_element_type=F32)


def _dot_nt(a, b):
    return lax.dot_general(a, b, (((1,), (1,)), ((), ())), preferred_element_type=F32)


def _rms(x, g, eps=1e-6):
    return x * lax.rsqrt(jnp.mean(x * x, axis=-1, keepdims=True) + eps) * g


def _silu(x):
    return x * jax.nn.sigmoid(x)


def _log_sigmoid(x):
    return jnp.minimum(x, 0.0) - jnp.log1p(jnp.exp(-jnp.abs(x)))


def _split3(x):
    hi = x.astype(BF16)
    r = x - hi.astype(F32)
    mid = r.astype(BF16)
    lo = (r - mid.astype(F32)).astype(BF16)
    return hi, mid, lo


def _tri_dot(tri, x):
    hi, mid, lo = _split3(x)
    return _dot(tri, hi) + _dot(tri, mid) + _dot(tri, lo)


def _iota(shape, dim):
    return lax.broadcasted_iota(I32, shape, dim)


def _const_spec(shape):
    zeros = (0,) * len(shape)
    return pl.BlockSpec(shape, lambda *_: zeros, pipeline_mode=pl.Buffered(1))


def _params(sem):
    return pltpu.CompilerParams(dimension_semantics=sem, vmem_limit_bytes=VMEM_LIMIT)


def _ab_kernel(x_ref, sa_ref, sb_ref, g_ref, win_ref, wgk_ref, bgk_ref, lb_ref, agn_ref, bgn_ref, wout_ref,
               xo_ref, sao_ref, sbo_ref, s_sc, *, t_valid, t_pad):
    C = REC_CHUNK
    t = pl.program_id(1)

    @pl.when(t == 0)
    def _():
        s_sc[0:4] = sa_ref[0]
        s_sc[4:6] = sb_ref[0]

    x = x_ref[0]
    xn = _rms(x, g_ref[...]).astype(BF16)
    h = _dot(xn, win_ref[...])
    a_q, a_f, a_i, a_g = h[:, 0:512], h[:, 512:1024], h[:, 1024:1536], h[:, 1536:2048]
    b_q, b_k, b_v, b_g = h[:, 2048:2304], h[:, 2304:2560], h[:, 2560:3072], h[:, 3072:3584]
    b_lr = h[:, 3584:3712]

    lb = lb_ref[...]
    f = lb + (1.0 - lb) * jax.nn.sigmoid(a_f)
    gk = _dot(b_lr.astype(BF16), wgk_ref[...]) + bgk_ref[...]
    la = jnp.concatenate([jnp.log(f), _log_sigmoid(gk) * (1.0 / B_GATE_NORM)], axis=1)
    q = jnp.concatenate([_silu(a_q), b_q * (B_DK ** -0.5)], axis=1)
    k = jnp.concatenate([1.0 - f, b_k], axis=1)
    v_a, v_b = a_i, b_v
    if t_valid < t_pad:
        ok = (t * C + _iota((C, 1), 0)) < t_valid
        la = jnp.where(ok, la, 0.0)
        k = jnp.where(ok, k, 0.0)
        v_a = jnp.where(ok, v_a, 0.0)
        v_b = jnp.where(ok, v_b, 0.0)

    row = _iota((C, C), 0)
    col = _iota((C, C), 1)
    causal = col <= row
    tri = jnp.where(causal, 1.0, 0.0).astype(BF16)
    tri_in = jnp.where(causal & ((row >> 4) == (col >> 4)), 1.0, 0.0).astype(BF16)
    bc = _tri_dot(tri, la)
    b_in = _tri_dot(tri_in, la)
    lane = _iota((1, LANES), 1)
    lo_half = lane < 64
    srow = _iota((LANES, 1), 0) < 64

    o_heads = [None] * 8
    for u in range(6):
        sl = slice(u * LANES, (u + 1) * LANES)
        qu, ku, bcu = q[:, sl], k[:, sl], bc[:, sl]
        s_old = s_sc[u]
        bend = bcu[C - 1:C, :]
        qt = qu * jnp.exp(b_in[:, sl])
        qdec = qu * jnp.exp(bcu)
        if u < 4:
            heads = [(u, None, v_a[:, sl])]
        else:
            ha = 4 + 2 * (u - 4)
            heads = [(ha, lo_half, v_b[:, (ha - 4) * LANES:(ha - 3) * LANES]),
                     (ha + 1, jnp.logical_not(lo_half), v_b[:, (ha - 3) * LANES:(ha - 2) * LANES])]
        a_rows = [[] for _ in heads]
        for i in range(C // RSUB):
            n = RSUB * (i + 1)
            if i == 0:
                kt = ku[0:n] * jnp.exp(-bcu[0:n])
            else:
                kt = ku[0:n] * jnp.exp(bcu[RSUB * i - 1:RSUB * i, :] - bcu[0:n])
            if n < C:
                kt = jnp.concatenate([kt, jnp.zeros((C - n, LANES), F32)], axis=0)
            ktb = kt.astype(BF16)
            qi = qt[RSUB * i:RSUB * (i + 1)]
            for hi_, (_, msk, _) in enumerate(heads):
                qim = qi if msk is None else jnp.where(msk, qi, 0.0)
                a_rows[hi_].append(_dot_nt(qim.astype(BF16), ktb))
        sb16 = s_old.astype(BF16)
        for hi_, (hd, msk, vh) in enumerate(heads):
            att = jnp.where(causal, jnp.concatenate(a_rows[hi_], axis=0), 0.0)
            qd = qdec if msk is None else jnp.where(msk, qdec, 0.0)
            o_heads[hd] = _dot(qd.astype(BF16), sb16) + _dot(att.astype(BF16), vh.astype(BF16))
        kht = (ku * jnp.exp(bend - bcu)).T
        dcol = jnp.broadcast_to(jnp.exp(bend), (LANES, LANES)).T
        if u < 4:
            upd = _dot(kht.astype(BF16), heads[0][2].astype(BF16))
        else:
            lhs = jnp.concatenate([jnp.where(srow, kht, 0.0), jnp.where(srow, 0.0, kht)], axis=1)
            rhs = jnp.concatenate([heads[0][2], heads[1][2]], axis=0)
            upd = _dot(lhs.astype(BF16), rhs.astype(BF16))
        s_sc[u] = dcol * s_old + upd

    outs = []
    for hd in range(8):
        if hd < 4:
            gn, gate = agn_ref[...], a_g[:, hd * LANES:(hd + 1) * LANES]
        else:
            gn, gate = bgn_ref[...], b_g[:, (hd - 4) * LANES:(hd - 3) * LANES]
        outs.append(_rms(o_heads[hd], gn) * _silu(gate))
    o = jnp.concatenate(outs, axis=1).astype(BF16)
    xo_ref[0] = x + _dot(o, wout_ref[...])

    @pl.when(t == pl.num_programs(1) - 1)
    def _():
        sao_ref[0] = s_sc[0:4]
        sbo_ref[0] = s_sc[4:6]


def _ab_layer(x, s_a, s_b, g, w_in, w_gk, b_gk, lb, a_gn, b_gn, w_out, t_valid):
    bsz, t_pad, _ = x.shape
    C = REC_CHUNK
    kern = functools.partial(_ab_kernel, t_valid=t_valid, t_pad=t_pad)
    return pl.pallas_call(
        kern,
        grid=(bsz, t_pad // C),
        in_specs=[
            pl.BlockSpec((1, C, D_MODEL), lambda b, t: (b, t, 0)),
            pl.BlockSpec((1, 4, 128, 128), lambda b, t: (b, 0, 0, 0)),
            pl.BlockSpec((1, 2, 128, 128), lambda b, t: (b, 0, 0, 0)),
            _const_spec((1, D_MODEL)),
            _const_spec((D_MODEL, AB_COLS)),
            _const_spec((LANES, 256)),
            _const_spec((1, 256)),
            _const_spec((1, 512)),
            _const_spec((1, 128)),
            _const_spec((1, 128)),
            _const_spec((D_MODEL, D_MODEL)),
        ],
        out_specs=[
            pl.BlockSpec((1, C, D_MODEL), lambda b, t: (b, t, 0)),
            pl.BlockSpec((1, 4, 128, 128), lambda b, t: (b, 0, 0, 0)),
            pl.BlockSpec((1, 2, 128, 128), lambda b, t: (b, 0, 0, 0)),
        ],
        out_shape=[
            jax.ShapeDtypeStruct((bsz, t_pad, D_MODEL), F32),
            jax.ShapeDtypeStruct((bsz, 4, 128, 128), F32),
            jax.ShapeDtypeStruct((bsz, 2, 128, 128), F32),
        ],
        scratch_shapes=[pltpu.VMEM((6, 128, 128), F32)],
        compiler_params=_params(("parallel", "arbitrary")),
        name="ab_layer",
    )(x, s_a, s_b, g, w_in, w_gk, b_gk, lb, a_gn, b_gn, w_out)


FFN_TILE = FFN_HIDDEN // 2


def _ffn_kernel(*refs, has_attn, has_final):
    refs = list(refs)
    x_ref = refs.pop(0)
    x = x_ref[...]
    if has_attn:
        oc_ref, od_ref, wo_ref = refs.pop(0), refs.pop(0), refs.pop(0)
        x = x + _dot(jnp.concatenate([oc_ref[...], od_ref[...]], axis=1), wo_ref[...])
    g_ref, win_ref, wout_ref = refs.pop(0), refs.pop(0), refs.pop(0)
    gf_ref = refs.pop(0) if has_final else None
    out_ref = refs.pop(0)
    xn = _rms(x, g_ref[...]).astype(BF16)
    acc = x
    for j in range(FFN_HIDDEN // FFN_TILE):
        gate = _dot(xn, win_ref[:, j * FFN_TILE:(j + 1) * FFN_TILE])
        up = _dot(xn, win_ref[:, FFN_HIDDEN + j * FFN_TILE:FFN_HIDDEN + (j + 1) * FFN_TILE])
        act = (_silu(gate) * up).astype(BF16)
        acc = acc + _dot(act, wout_ref[j * FFN_TILE:(j + 1) * FFN_TILE, :])
    if has_final:
        acc = _rms(acc, gf_ref[...])
    out_ref[...] = acc


def _ffn(x, g, w_in, w_out, attn=None, g_final=None, tm=512):
    n = x.shape[0]
    tm = min(tm, n)
    row = lambda i: (i, 0)
    args = [x]
    specs = [pl.BlockSpec((tm, D_MODEL), row)]
    if attn is not None:
        oc, od, wo = attn
        args += [oc, od, wo]
        specs += [pl.BlockSpec((tm, 512), row), pl.BlockSpec((tm, 512), row), _const_spec((D_MODEL, D_MODEL))]
    args += [g, w_in, w_out]
    specs += [_const_spec((1, D_MODEL)), _const_spec((D_MODEL, 2 * FFN_HIDDEN)), _const_spec((FFN_HIDDEN, D_MODEL))]
    if g_final is not None:
        args.append(g_final)
        specs.append(_const_spec((1, D_MODEL)))
    kern = functools.partial(_ffn_kernel, has_attn=attn is not None, has_final=g_final is not None)
    return pl.pallas_call(
        kern,
        grid=(n // tm,),
        in_specs=specs,
        out_specs=pl.BlockSpec((tm, D_MODEL), row),
        out_shape=jax.ShapeDtypeStruct((n, D_MODEL), F32),
        compiler_params=_params(("parallel",)),
        name="ffn",
    )(*args)


def _cd_proj_kernel(x_ref, g_ref, w_ref, fb_ref, fq_ref, ck_ref, cv_ref, dq_ref, dk_ref, dv_ref, iq_ref,
                    misc_ref, iw_ref):
    xn = _rms(x_ref[...], g_ref[...]).astype(BF16)
    h = _dot(xn, w_ref[...])
    fq_ref[...] = (h[:, 0:512] * (C_HD ** -0.5)).astype(BF16)
    ck_ref[...] = h[:, 512:1024]
    cv_ref[...] = h[:, 1024:1536]
    dq_ref[...] = (h[:, 1536:2560] * (D_HD ** -0.5)).astype(BF16)
    dk_ref[...] = h[:, 2560:2688]
    dv_ref[...] = h[:, 2688:2816]
    iq_ref[...] = h[:, 2816:3072].astype(BF16)
    misc = h[:, 3072:3200]
    lane = _iota((1, LANES), 1)
    is_f = (lane >= IDX_DIM) & (lane < IDX_DIM + C_HEADS)
    misc_ref[...] = jnp.where(is_f, _log_sigmoid(misc + fb_ref[...]), misc)
    iw_ref[...] = h[:, 3200:3328]


def _cd_proj(x, g, w, fb, tm=512):
    n = x.shape[0]
    tm = min(tm, n)
    row = lambda i: (i, 0)
    widths = [(512, BF16), (512, F32), (512, F32), (1024, BF16), (128, F32), (128, F32), (256, BF16),
              (128, F32), (128, F32)]
    return pl.pallas_call(
        _cd_proj_kernel,
        grid=(n // tm,),
        in_specs=[pl.BlockSpec((tm, D_MODEL), row), _const_spec((1, D_MODEL)), _const_spec((D_MODEL, CD_COLS)),
                  _const_spec((1, LANES))],
        out_specs=[pl.BlockSpec((tm, w_), row) for w_, _ in widths],
        out_shape=[jax.ShapeDtypeStruct((n, w_), dt) for w_, dt in widths],
        compiler_params=_params(("parallel",)),
        name="cd_proj",
    )(x, g, w, fb)


def _aug_consts(is_query):
    p = np.zeros((3, LANES, 8 * LANES), np.float32)
    ones = np.zeros((1, 8 * LANES), np.float32)
    for h in range(8):
        off = h * LANES + (64 if h % 2 == 0 else 0)
        for c in range(3):
            if is_query:
                p[c, h, off + c] = 1.0
                ones[0, off + 3 + c] = 1.0
            else:
                p[c, h, off + 3 + c] = -1.0
                ones[0, off + c] = 1.0
    return jnp.asarray(p, BF16), jnp.asarray(ones, F32)


def _aug_lanes(cum, p_ref, ones_ref):
    hi, mid, lo = _split3(cum)
    return _dot(hi, p_ref[0]) + _dot(mid, p_ref[1]) + _dot(lo, p_ref[2]) + ones_ref[...]


def _fox_pack_kernel(k_ref, v_ref, lf_ref, p_ref, ones_ref, ka_ref, vh_ref, cum_ref, carry):
    tm = k_ref.shape[1]

    @pl.when(pl.program_id(1) == 0)
    def _():
        carry[...] = jnp.zeros_like(carry)

    tri = jnp.where(_iota((tm, tm), 1) <= _iota((tm, tm), 0), 1.0, 0.0).astype(BF16)
    cum = _tri_dot(tri, lf_ref[0]) + carry[...]
    carry[...] = cum[tm - 1:tm, :]
    cum_ref[0] = cum
    aug = _aug_lanes(cum, p_ref, ones_ref)
    lane = _iota((1, LANES), 1)
    kk, vv = k_ref[0], v_ref[0]
    for h in range(8):
        pr = slice((h // 2) * LANES, (h // 2 + 1) * LANES)
        own = (lane < 64) if h % 2 == 0 else (lane >= 64)
        ka_ref[0, h] = jnp.where(own, kk[:, pr], aug[:, h * LANES:(h + 1) * LANES]).astype(BF16)
        vh_ref[0, h] = jnp.where(own, vv[:, pr], 0.0).astype(BF16)


def _fox_pack(k, v, lf, tm):
    bsz, s, _ = k.shape
    p, ones = _aug_consts(False)
    return pl.pallas_call(
        _fox_pack_kernel,
        grid=(bsz, s // tm),
        in_specs=[pl.BlockSpec((1, tm, 512), lambda b, t: (b, t, 0)),
                  pl.BlockSpec((1, tm, 512), lambda b, t: (b, t, 0)),
                  pl.BlockSpec((1, tm, LANES), lambda b, t: (b, t, 0)),
                  _const_spec((3, LANES, 8 * LANES)), _const_spec((1, 8 * LANES))],
        out_specs=[pl.BlockSpec((1, 8, tm, LANES), lambda b, t: (b, 0, t, 0)),
                   pl.BlockSpec((1, 8, tm, LANES), lambda b, t: (b, 0, t, 0)),
                   pl.BlockSpec((1, tm, LANES), lambda b, t: (b, t, 0))],
        out_shape=[jax.ShapeDtypeStruct((bsz, 8, s, LANES), BF16),
                   jax.ShapeDtypeStruct((bsz, 8, s, LANES), BF16),
                   jax.ShapeDtypeStruct((bsz, s, LANES), F32)],
        scratch_shapes=[pltpu.VMEM((1, LANES), F32)],
        compiler_params=_params(("parallel", "arbitrary")),
        name="fox_pack",
    )(k, v, lf, p, ones)


def _q_pack_kernel(q_ref, cum_ref, p_ref, ones_ref, qa_ref):
    aug = _aug_lanes(cum_ref[0], p_ref, ones_ref)
    lane = _iota((1, LANES), 1)
    qq = q_ref[0].astype(F32)
    for h in range(8):
        pr = slice((h // 2) * LANES, (h // 2 + 1) * LANES)
        own = (lane < 64) if h % 2 == 0 else (lane >= 64)
        qa_ref[0, h] = jnp.where(own, qq[:, pr], aug[:, h * LANES:(h + 1) * LANES]).astype(BF16)


def _q_pack(q, cum_q, tm):
    bsz, tq, _ = q.shape
    p, ones = _aug_consts(True)
    return pl.pallas_call(
        _q_pack_kernel,
        grid=(bsz, tq // tm),
        in_specs=[pl.BlockSpec((1, tm, 512), lambda b, t: (b, t, 0)),
                  pl.BlockSpec((1, tm, LANES), lambda b, t: (b, t, 0)),
                  _const_spec((3, LANES, 8 * LANES)), _const_spec((1, 8 * LANES))],
        out_specs=pl.BlockSpec((1, 8, tm, LANES), lambda b, t: (b, 0, t, 0)),
        out_shape=jax.ShapeDtypeStruct((bsz, 8, tq, LANES), BF16),
        compiler_params=_params(("parallel", "parallel")),
        name="q_pack",
    )(q, cum_q, p, ones)


def _fox_attn_kernel(q_ref, k_ref, v_ref, o_ref, *, tq, tk, q_start):
    i = pl.program_id(2)
    q_lo = q_start + i * tq
    n_tiles = (q_lo + tq + tk - 1) // tk
    q_pos = q_lo + _iota((tq, 1), 0)
    out = None
    for hh in range(2):
        q = q_ref[0, hh]

        def body(j, carry, q=q, hh=hh):
            m, l, acc = carry
            off = pl.multiple_of(j * tk, tk)
            kt = k_ref[0, hh, pl.ds(off, tk), :]
            vt = v_ref[0, hh, pl.ds(off, tk), :]
            s = _dot_nt(q, kt)
            ok = (off + _iota((1, tk), 1)) <= q_pos
            s = jnp.where(ok, s, NEG)
            m_new = jnp.maximum(m, jnp.max(s, axis=-1, keepdims=True))
            alpha = jnp.exp(m - m_new)
            p = jnp.where(ok, jnp.exp(s - m_new), 0.0)
            l = alpha * l + jnp.sum(p, axis=-1, keepdims=True)
            acc = alpha * acc + _dot(p.astype(BF16), vt)
            return m_new, l, acc

        init = (jnp.full((tq, 1), NEG, F32), jnp.zeros((tq, 1), F32), jnp.zeros((tq, LANES), F32))
        _, l, acc = lax.fori_loop(0, n_tiles, body, init)
        o = acc / l
        out = o if out is None else out + o
    o_ref[0] = out.astype(BF16)


def _fox_attn(q_aug, k_aug, v_hm, tq, tk, q_start):
    bsz, _, t_q, _ = q_aug.shape
    s = k_aug.shape[2]
    kern = functools.partial(_fox_attn_kernel, tq=tq, tk=tk, q_start=q_start)
    return pl.pallas_call(
        kern,
        grid=(bsz, 4, t_q // tq),
        in_specs=[pl.BlockSpec((1, 2, tq, LANES), lambda b, p, i: (b, p, i, 0)),
                  pl.BlockSpec((1, 2, s, LANES), lambda b, p, i: (b, p, 0, 0)),
                  pl.BlockSpec((1, 2, s, LANES), lambda b, p, i: (b, p, 0, 0))],
        out_specs=pl.BlockSpec((1, tq, LANES), lambda b, p, i: (b, i, p)),
        out_shape=jax.ShapeDtypeStruct((bsz, t_q, 512), BF16),
        compiler_params=_params(("parallel", "parallel", "arbitrary")),
        name="fox_attn",
    )(q_aug, k_aug, v_hm)


def _dsa_pack_kernel(dk_ref, dv_ref, ik_ref, k2_ref, v4_ref, ik2_ref):
    lane = _iota((1, LANES), 1)
    lo = lane < 64
    k2_ref[0] = dk_ref[0].astype(BF16)
    v = dv_ref[0]
    vr = pltpu.roll(v, 64, 1)
    v4_ref[0, 0] = jnp.where(lo, v, 0.0).astype(BF16)
    v4_ref[0, 1] = jnp.where(lo, 0.0, vr).astype(BF16)
    v4_ref[0, 2] = jnp.where(lo, vr, 0.0).astype(BF16)
    v4_ref[0, 3] = jnp.where(lo, 0.0, v).astype(BF16)
    ik = jnp.where(lo, ik_ref[0], 0.0)
    ik2_ref[0, 0] = ik.astype(BF16)
    ik2_ref[0, 1] = pltpu.roll(ik, 64, 1).astype(BF16)


def _dsa_pack(dk, dv, ik, tm):
    bsz, s, _ = dk.shape
    tok = pl.BlockSpec((1, tm, LANES), lambda b, t: (b, t, 0))
    return pl.pallas_call(
        _dsa_pack_kernel,
        grid=(bsz, s // tm),
        in_specs=[tok, tok, tok],
        out_specs=[tok,
                   pl.BlockSpec((1, 4, tm, LANES), lambda b, t: (b, 0, t, 0)),
                   pl.BlockSpec((1, 2, tm, LANES), lambda b, t: (b, 0, t, 0))],
        out_shape=[jax.ShapeDtypeStruct((bsz, s, LANES), BF16),
                   jax.ShapeDtypeStruct((bsz, 4, s, LANES), BF16),
                   jax.ShapeDtypeStruct((bsz, 2, s, LANES), BF16)],
        compiler_params=_params(("parallel", "parallel")),
        name="dsa_pack",
    )(dk, dv, ik)


def _dsa_attn_kernel(dq_ref, iq_ref, iw_ref, ik2_ref, k2_ref, v4_ref, o_ref, keys_sc, m_sc, l_sc, acc_sc,
                     *, tq, tk, q_start, s_valid, topk):
    i = pl.program_id(1)
    q_lo = q_start + i * tq
    q_pos = q_lo + _iota((tq, 1), 0)
    q_chunk = q_pos // CHUNK
    adm_end = jnp.minimum(((q_lo + tq - 1) // CHUNK + 1) * CHUNK, s_valid)
    n_tiles = (adm_end + tk - 1) // tk
    iw = iw_ref[0]
    iq = iq_ref[0]

    def score_body(j, c):
        off = pl.multiple_of(j * tk, tk)
        sc = jnp.zeros((tq, tk), F32)
        for hd in range(IDX_HEADS):
            qp = iq[:, (hd // 2) * LANES:(hd // 2 + 1) * LANES]
            r = jnp.maximum(_dot_nt(qp, ik2_ref[0, hd % 2, pl.ds(off, tk), :]), 0.0)
            sc = sc + r * iw[:, hd:hd + 1]
        sc = sc * IDX_SCALE + 0.0
        kpos = off + _iota((1, tk), 1)
        adm = ((kpos // CHUNK) <= q_chunk) & (kpos < s_valid)
        bits = lax.bitcast_convert_type(sc, I32)
        key = bits ^ ((bits >> 31) & 0x7FFFFFFF)
        keys_sc[j] = jnp.where(adm, key, INT_MIN)
        return c

    lax.fori_loop(0, n_tiles, score_body, 0)

    def count(pred):
        def body(j, c):
            return c + jnp.sum(jnp.where(pred(j, keys_sc[j]), 1.0, 0.0), axis=-1, keepdims=True)
        return lax.fori_loop(0, n_tiles, body, jnp.zeros((tq, 1), F32))

    def tau_body(b, tau):
        cand = tau + lax.shift_left(jnp.int32(1), 31 - b)
        cnt = count(lambda j, kk: kk >= cand)
        return jnp.where(cnt >= topk, cand, tau)

    tau = lax.fori_loop(0, 32, tau_body, jnp.full((tq, 1), INT_MIN, I32))
    need = topk - count(lambda j, kk: kk > tau)

    def pos_body(b, p):
        cand = p + lax.shift_left(jnp.int32(1), 12 - b)

        def pred(j, kk):
            kpos = j * tk + _iota((1, tk), 1)
            return (kk == tau) & (kpos < cand)
        cnt = count(pred)
        return jnp.where(cnt < need, cand, p)

    p_cut = lax.fori_loop(0, 13, pos_body, jnp.zeros((tq, 1), I32))

    m_sc[...] = jnp.full(m_sc.shape, NEG, F32)
    l_sc[...] = jnp.zeros(l_sc.shape, F32)
    acc_sc[...] = jnp.zeros(acc_sc.shape, F32)

    def attn_body(j, c):
        off = pl.multiple_of(j * tk, tk)
        kk = keys_sc[j]
        kpos = off + _iota((1, tk), 1)
        sel = ((kk > tau) | ((kk == tau) & (kpos <= p_cut))) & (kk != INT_MIN)
        kt = k2_ref[0, pl.ds(off, tk), :]
        for hd in range(D_HEADS):
            g = hd // (D_HEADS // D_KV_HEADS)
            q = dq_ref[0, :, hd * LANES:(hd + 1) * LANES]
            vt = v4_ref[0, 2 * g + hd % 2, pl.ds(off, tk), :]
            s = jnp.where(sel, _dot_nt(q, kt), NEG)
            m_old = m_sc[hd]
            m_new = jnp.maximum(m_old, jnp.max(s, axis=-1, keepdims=True))
            alpha = jnp.exp(m_old - m_new)
            p = jnp.where(sel, jnp.exp(s - m_new), 0.0)
            l_sc[hd] = alpha * l_sc[hd] + jnp.sum(p, axis=-1, keepdims=True)
            acc_sc[hd] = alpha * acc_sc[hd] + _dot(p.astype(BF16), vt)
            m_sc[hd] = m_new
        return c

    lax.fori_loop(0, n_tiles, attn_body, 0)
    for pr in range(D_HEADS // 2):
        o = acc_sc[2 * pr] / l_sc[2 * pr] + acc_sc[2 * pr + 1] / l_sc[2 * pr + 1]
        o_ref[0, :, pr * LANES:(pr + 1) * LANES] = o.astype(BF16)


def _dsa_attn(dq, iq, iw, ik2, k2, v4, tq, tk, q_start, s_valid):
    bsz, t_q, _ = dq.shape
    s = k2.shape[1]
    topk = min(IDX_TOPK_MAX, s_valid // 4)
    kern = functools.partial(_dsa_attn_kernel, tq=tq, tk=tk, q_start=q_start, s_valid=s_valid, topk=float(topk))
    return pl.pallas_call(
        kern,
        grid=(bsz, t_q // tq),
        in_specs=[pl.BlockSpec((1, tq, 8 * LANES), lambda b, i: (b, i, 0)),
                  pl.BlockSpec((1, tq, 256), lambda b, i: (b, i, 0)),
                  pl.BlockSpec((1, tq, LANES), lambda b, i: (b, i, 0)),
                  pl.BlockSpec((1, 2, s, LANES), lambda b, i: (b, 0, 0, 0)),
                  pl.BlockSpec((1, s, LANES), lambda b, i: (b, 0, 0)),
                  pl.BlockSpec((1, 4, s, LANES), lambda b, i: (b, 0, 0, 0))],
        out_specs=pl.BlockSpec((1, tq, 512), lambda b, i: (b, i, 0)),
        out_shape=jax.ShapeDtypeStruct((bsz, t_q, 512), BF16),
        scratch_shapes=[pltpu.VMEM((s // tk, tq, tk), I32),
                        pltpu.VMEM((D_HEADS, tq, 1), F32),
                        pltpu.VMEM((D_HEADS, tq, 1), F32),
                        pltpu.VMEM((D_HEADS, tq, LANES), F32)],
        compiler_params=_params(("parallel", "arbitrary")),
        name="dsa_attn",
    )(dq, iq, iw, ik2, k2, v4)


def _prep_ab(w_in, w_gk):
    w = jnp.pad(w_in, ((0, 0), (0, AB_COLS - w_in.shape[1]))).astype(BF16)
    wgk = jnp.pad(w_gk, ((0, LANES - B_GATE_RANK), (0, 0))).astype(BF16)
    return w, wgk


def _prep_cd(w_in, f_bias):
    o = np.cumsum([0, 512, 512, 512, C_HEADS, 512, 128, 128, 256, IDX_DIM, IDX_HEADS])
    c_q, c_k, c_v, c_f, d_q, d_k, d_v, d_iq, d_ik, d_iw = (w_in[:, o[i]:o[i + 1]] for i in range(10))
    zeros = lambda n: jnp.zeros((D_MODEL, n), w_in.dtype)
    dq_cols = []
    for h in range(D_HEADS):
        g = h // (D_HEADS // D_KV_HEADS)
        wh = d_q[:, h * 64:(h + 1) * 64]
        dq_cols += [wh, zeros(64)] if g == 0 else [zeros(64), wh]
    w = jnp.concatenate([c_q, c_k, c_v] + dq_cols + [d_k, d_v, d_iq, d_ik, c_f, zeros(LANES - IDX_DIM - C_HEADS),
                                                     d_iw, zeros(LANES - IDX_HEADS)], axis=1).astype(BF16)
    fb = jnp.pad(f_bias.astype(F32), (IDX_DIM, LANES - IDX_DIM - C_HEADS)).reshape(1, LANES)
    return w, fb


def _pad_rows(z, s):
    return jnp.pad(z, ((0, 0), (0, s - z.shape[1]), (0, 0)))


def _pad_lanes(z, n=LANES):
    return jnp.pad(z, ((0, 0), (0, 0), (0, n - z.shape[2])))


def _trunk(x, s_a, s_b, cache, wts):
    bsz, t_len, _ = x.shape
    row = lambda z: z.reshape(1, -1).astype(F32)

    t_pad = -(-t_len // REC_CHUNK) * REC_CHUNK
    x1, sa_new, sb_new = _ab_layer(
        _pad_rows(x, t_pad), s_a, s_b.reshape(bsz, 2, 128, 128), row(wts['norm_mix'][0]), wts['ab_w_in'],
        wts['gla_w_gk'], row(wts['gla_b_gk']), row(wts['lb']), row(wts['hgrn_gnorm']), row(wts['gla_gnorm']),
        wts['ab_w_out'], t_len)
    n = bsz * t_len
    x1 = x1[:, :t_len].reshape(n, D_MODEL)
    x2 = _ffn(x1, row(wts['norm_ffn'][0]), wts['ffn_w_in'][0], wts['ffn_w_out'][0])

    fq, ck, cv, dq, dk, dv, iq, misc, iw = _cd_proj(x2, row(wts['norm_mix'][1]), wts['cd_w_in'], wts['fox_f_bias'])
    per_b = lambda z: z.reshape(bsz, t_len, z.shape[-1])
    fq, ck, cv, dq, dk, dv, iq, misc, iw = map(per_b, (fq, ck, cv, dq, dk, dv, iq, misc, iw))
    logf = misc[:, :, IDX_DIM:IDX_DIM + C_HEADS]
    if cache is None:
        q_start = 0
        k_all, v_all, lf_all, dk_all, dv_all, ik_all = ck, cv, _pad_lanes(logf), dk, dv, misc
    else:
        c_k, c_v, c_lf, c_dk, c_dv, c_ik = cache
        q_start = c_k.shape[1]
        cat = lambda c, r: jnp.concatenate([c.reshape(bsz, q_start, -1).astype(F32), r], axis=1)
        k_all, v_all, dk_all, dv_all = cat(c_k, ck), cat(c_v, cv), cat(c_dk, dk), cat(c_dv, dv)
        lf_all = _pad_lanes(cat(c_lf, logf))
        ik_all = cat(_pad_lanes(c_ik), misc)
    s_valid = q_start + t_len
    tk = 512
    s_pad = -(-s_valid // tk) * tk
    k_all, v_all, lf_all, dk_all, dv_all, ik_all = (_pad_rows(z, s_pad) for z in
                                                    (k_all, v_all, lf_all, dk_all, dv_all, ik_all))
    tq = min(256, t_len)
    k_aug, v_hm, cum = _fox_pack(k_all, v_all, lf_all, tm=256)
    q_aug = _q_pack(fq, cum[:, q_start:q_start + t_len], tm=tq)
    o_c = _fox_attn(q_aug, k_aug, v_hm, tq=tq, tk=tk, q_start=q_start)
    k2, v4, ik2 = _dsa_pack(dk_all, dv_all, ik_all, tm=512)
    o_d = _dsa_attn(dq, iq, iw, ik2, k2, v4, tq=min(128, t_len), tk=tk, q_start=q_start, s_valid=s_valid)

    y = _ffn(x2, row(wts['norm_ffn'][1]), wts['ffn_w_in'][1], wts['ffn_w_out'][1],
             attn=(o_c.reshape(n, 512), o_d.reshape(n, 512), wts['cd_w_out']), g_final=row(wts['norm_final']))
    rows = (ck.reshape(1, bsz, t_len, C_HEADS, C_HD), cv.reshape(1, bsz, t_len, C_HEADS, C_HD),
            logf[None], dk.reshape(1, bsz, t_len, D_KV_HEADS, D_HD), dv.reshape(1, bsz, t_len, D_KV_HEADS, D_HD),
            misc[None, :, :, :IDX_DIM])
    return (y.reshape(bsz, t_len, D_MODEL), sa_new[None], sb_new.reshape(1, bsz, B_HEADS, B_DK, B_DV)) + rows


def kernel(x_prompt, x_sample, state_hgrn, state_gla, cache_fox_k, cache_fox_v, cache_fox_logf, cache_dsa_k, cache_dsa_v, cache_dsa_ik, norm_mix, norm_ffn, norm_final, ab_w_in, ab_w_out, hgrn_lb_logits, hgrn_gnorm, gla_w_gk, gla_b_gk, gla_gnorm, cd_w_in, cd_w_out, fox_f_bias, ffn_w_in, ffn_w_out):
    lbs = jnp.cumsum(jax.nn.softmax(hgrn_lb_logits.astype(F32), axis=0), axis=0)
    w_ab, w_gk = _prep_ab(ab_w_in[0], gla_w_gk[0])
    w_cd, fb = _prep_cd(cd_w_in[0], fox_f_bias[0])
    wts = dict(norm_mix=norm_mix, norm_ffn=norm_ffn, norm_final=norm_final, ab_w_in=w_ab,
               ab_w_out=ab_w_out[0].astype(BF16), lb=lbs[0], hgrn_gnorm=hgrn_gnorm[0], gla_w_gk=w_gk,
               gla_b_gk=gla_b_gk[0], gla_gnorm=gla_gnorm[0], cd_w_in=w_cd, cd_w_out=cd_w_out[0].astype(BF16),
               fox_f_bias=fb, ffn_w_in=ffn_w_in.astype(BF16), ffn_w_out=ffn_w_out.astype(BF16))
    bp = x_prompt.shape[0]
    p_out = _trunk(x_prompt, jnp.zeros((bp, A_HEADS, A_DK, A_DV), F32), jnp.zeros((bp, B_HEADS, B_DK, B_DV), F32),
                   None, wts)
    cache = (cache_fox_k[0], cache_fox_v[0], cache_fox_logf[0], cache_dsa_k[0], cache_dsa_v[0], cache_dsa_ik[0])
    s_out = _trunk(x_sample, state_hgrn[0], state_gla[0], cache, wts)
    return (p_out[0], s_out[0]) + tuple(p_out[1:]) + tuple(s_out[1:])
```

```python
import functools

import numpy as np
import jax
import jax.numpy as jnp
from jax import lax
from jax.experimental import pallas as pl
from jax.experimental.pallas import tpu as pltpu

F32 = jnp.float32
BF16 = jnp.bfloat16
I32 = jnp.int32

D_MODEL = 1024
CHUNK = 64
A_HEADS, A_DK, A_DV = 4, 128, 128
B_HEADS, B_DK, B_DV = 4, 64, 128
B_GATE_RANK = 16
B_GATE_NORM = 16.0
C_HEADS, C_HD = 8, 64
D_HEADS, D_KV_HEADS, D_HD = 8, 2, 64
IDX_HEADS, IDX_DIM = 4, 64
IDX_TOPK_MAX = 256
IDX_SCALE = (IDX_DIM ** -0.5) * (IDX_HEADS ** -0.5)
FFN_HIDDEN = ((8 * D_MODEL // 3 + 255) // 256) * 256

LANES = 128
RSUB = 16
REC_CHUNK = 128
VMEM_LIMIT = 56 * 1024 * 1024
NEG = -1e30
MASKED = -2e30
INT_MIN = -2 ** 31

AB_COLS = 4 * 512 + 256 + 256 + 512 + 512 + LANES
CD_COLS = 3 * 512 + 8 * LANES + 2 * LANES + 256 + LANES + LANES


def _dot(a, b):
    return jnp.dot(a, b, preferred_element_type=F32)


def _dot_nt(a, b):
    return lax.dot_general(a, b, (((1,), (1,)), ((), ())), preferred_element_type=F32)


def _rms(x, g, eps=1e-6):
    return x * lax.rsqrt(jnp.mean(x * x, axis=-1, keepdims=True) + eps) * g


def _silu(x):
    return x * jax.nn.sigmoid(x)


def _log_sigmoid(x):
    return jnp.minimum(x, 0.0) - jnp.log1p(jnp.exp(-jnp.abs(x)))


def _split3(x):
    hi = x.astype(BF16)
    r = x - hi.astype(F32)
    mid = r.astype(BF16)
    lo = (r - mid.astype(F32)).astype(BF16)
    return hi, mid, lo


def _tri_dot(tri, x):
    hi, mid, lo = _split3(x)
    return _dot(tri, hi) + _dot(tri, mid) + _dot(tri, lo)


def _iota(shape, dim):
    return lax.broadcasted_iota(I32, shape, dim)


def _const_spec(shape):
    zeros = (0,) * len(shape)
    return pl.BlockSpec(shape, lambda *_: zeros, pipeline_mode=pl.Buffered(1))


def _params(sem):
    return pltpu.CompilerParams(dimension_semantics=sem, vmem_limit_bytes=VMEM_LIMIT)


def _ab_kernel(x_ref, sa_ref, sb_ref, g_ref, win_ref, wgk_ref, bgk_ref, lb_ref, agn_ref, bgn_ref, wout_ref,
               xo_ref, sao_ref, sbo_ref, s_sc, *, t_valid, t_pad):
    C = REC_CHUNK
    t = pl.program_id(1)

    @pl.when(t == 0)
    def _():
        s_sc[0:4] = sa_ref[0]
        s_sc[4:6] = sb_ref[0]

    x = x_ref[0]
    xn = _rms(x, g_ref[...]).astype(BF16)
    h = _dot(xn, win_ref[...])
    a_q, a_f, a_i, a_g = h[:, 0:512], h[:, 512:1024], h[:, 1024:1536], h[:, 1536:2048]
    b_q, b_k, b_v, b_g = h[:, 2048:2304], h[:, 2304:2560], h[:, 2560:3072], h[:, 3072:3584]
    b_lr = h[:, 3584:3712]

    lb = lb_ref[...]
    f = lb + (1.0 - lb) * jax.nn.sigmoid(a_f)
    gk = _dot(b_lr.astype(BF16), wgk_ref[...]) + bgk_ref[...]
    la = jnp.concatenate([jnp.log(f), _log_sigmoid(gk) * (1.0 / B_GATE_NORM)], axis=1)
    q = jnp.concatenate([_silu(a_q), b_q * (B_DK ** -0.5)], axis=1)
    k = jnp.concatenate([1.0 - f, b_k], axis=1)
    v_a, v_b = a_i, b_v
    if t_valid < t_pad:
        ok = (t * C + _iota((C, 1), 0)) < t_valid
        la = jnp.where(ok, la, 0.0)
        k = jnp.where(ok, k, 0.0)
        v_a = jnp.where(ok, v_a, 0.0)
        v_b = jnp.where(ok, v_b, 0.0)

    row = _iota((C, C), 0)
    col = _iota((C, C), 1)
    causal = col <= row
    tri = jnp.where(causal, 1.0, 0.0).astype(BF16)
    tri_in = jnp.where(causal & ((row >> 4) == (col >> 4)), 1.0, 0.0).astype(BF16)
    bc = _tri_dot(tri, la)
    b_in = _tri_dot(tri_in, la)
    lane = _iota((1, LANES), 1)
    lo_half = lane < 64
    srow = _iota((LANES, 1), 0) < 64

    o_heads = [None] * 8
    for u in range(6):
        sl = slice(u * LANES, (u + 1) * LANES)
        qu, ku, bcu = q[:, sl], k[:, sl], bc[:, sl]
        s_old = s_sc[u]
        bend = bcu[C - 1:C, :]
        qt = qu * jnp.exp(b_in[:, sl])
        qdec = qu * jnp.exp(bcu)
        if u < 4:
            heads = [(u, None, v_a[:, sl])]
        else:
            ha = 4 + 2 * (u - 4)
            heads = [(ha, lo_half, v_b[:, (ha - 4) * LANES:(ha - 3) * LANES]),
                     (ha + 1, jnp.logical_not(lo_half), v_b[:, (ha - 3) * LANES:(ha - 2) * LANES])]
        a_rows = [[] for _ in heads]
        for i in range(C // RSUB):
            n = RSUB * (i + 1)
            if i == 0:
                kt = ku[0:n] * jnp.exp(-bcu[0:n])
            else:
                kt = ku[0:n] * jnp.exp(bcu[RSUB * i - 1:RSUB * i, :] - bcu[0:n])
            if n < C:
                kt = jnp.concatenate([kt, jnp.zeros((C - n, LANES), F32)], axis=0)
            ktb = kt.astype(BF16)
            qi = qt[RSUB * i:RSUB * (i + 1)]
            for hi_, (_, msk, _) in enumerate(heads):
                qim = qi if msk is None else jnp.where(msk, qi, 0.0)
                a_rows[hi_].append(_dot_nt(qim.astype(BF16), ktb))
        sb16 = s_old.astype(BF16)
        for hi_, (hd, msk, vh) in enumerate(heads):
            att = jnp.where(causal, jnp.concatenate(a_rows[hi_], axis=0), 0.0)
            qd = qdec if msk is None else jnp.where(msk, qdec, 0.0)
            o_heads[hd] = _dot(qd.astype(BF16), sb16) + _dot(att.astype(BF16), vh.astype(BF16))
        kht = (ku * jnp.exp(bend - bcu)).T
        dcol = jnp.broadcast_to(jnp.exp(bend), (LANES, LANES)).T
        if u < 4:
            upd = _dot(kht.astype(BF16), heads[0][2].astype(BF16))
        else:
            lhs = jnp.concatenate([jnp.where(srow, kht, 0.0), jnp.where(srow, 0.0, kht)], axis=1)
            rhs = jnp.concatenate([heads[0][2], heads[1][2]], axis=0)
            upd = _dot(lhs.astype(BF16), rhs.astype(BF16))
        s_sc[u] = dcol * s_old + upd

    outs = []
    for hd in range(8):
        if hd < 4:
            gn, gate = agn_ref[...], a_g[:, hd * LANES:(hd + 1) * LANES]
        else:
            gn, gate = bgn_ref[...], b_g[:, (hd - 4) * LANES:(hd - 3) * LANES]
        outs.append(_rms(o_heads[hd], gn) * _silu(gate))
    o = jnp.concatenate(outs, axis=1).astype(BF16)
    xo_ref[0] = x + _dot(o, wout_ref[...])

    @pl.when(t == pl.num_programs(1) - 1)
    def _():
        sao_ref[0] = s_sc[0:4]
        sbo_ref[0] = s_sc[4:6]


def _ab_layer(x, s_a, s_b, g, w_in, w_gk, b_gk, lb, a_gn, b_gn, w_out, t_valid):
    bsz, t_pad, _ = x.shape
    C = REC_CHUNK
    kern = functools.partial(_ab_kernel, t_valid=t_valid, t_pad=t_pad)
    return pl.pallas_call(
        kern,
        grid=(bsz, t_pad // C),
        in_specs=[
            pl.BlockSpec((1, C, D_MODEL), lambda b, t: (b, t, 0)),
            pl.BlockSpec((1, 4, 128, 128), lambda b, t: (b, 0, 0, 0)),
            pl.BlockSpec((1, 2, 128, 128), lambda b, t: (b, 0, 0, 0)),
            _const_spec((1, D_MODEL)),
            _const_spec((D_MODEL, AB_COLS)),
            _const_spec((LANES, 256)),
            _const_spec((1, 256)),
            _const_spec((1, 512)),
            _const_spec((1, 128)),
            _const_spec((1, 128)),
            _const_spec((D_MODEL, D_MODEL)),
        ],
        out_specs=[
            pl.BlockSpec((1, C, D_MODEL), lambda b, t: (b, t, 0)),
            pl.BlockSpec((1, 4, 128, 128), lambda b, t: (b, 0, 0, 0)),
            pl.BlockSpec((1, 2, 128, 128), lambda b, t: (b, 0, 0, 0)),
        ],
        out_shape=[
            jax.ShapeDtypeStruct((bsz, t_pad, D_MODEL), F32),
            jax.ShapeDtypeStruct((bsz, 4, 128, 128), F32),
            jax.ShapeDtypeStruct((bsz, 2, 128, 128), F32),
        ],
        scratch_shapes=[pltpu.VMEM((6, 128, 128), F32)],
        compiler_params=_params(("parallel", "arbitrary")),
        name="ab_layer",
    )(x, s_a, s_b, g, w_in, w_gk, b_gk, lb, a_gn, b_gn, w_out)


FFN_TILE = FFN_HIDDEN // 2


def _ffn_kernel(*refs, has_attn, has_final):
    refs = list(refs)
    x_ref = refs.pop(0)
    x = x_ref[...]
    if has_attn:
        oc_ref, od_ref, wo_ref = refs.pop(0), refs.pop(0), refs.pop(0)
        x = x + _dot(jnp.concatenate([oc_ref[...], od_ref[...]], axis=1), wo_ref[...])
    g_ref, win_ref, wout_ref = refs.pop(0), refs.pop(0), refs.pop(0)
    gf_ref = refs.pop(0) if has_final else None
    out_ref = refs.pop(0)
    xn = _rms(x, g_ref[...]).astype(BF16)
    acc = x
    for j in range(FFN_HIDDEN // FFN_TILE):
        gate = _dot(xn, win_ref[:, j * FFN_TILE:(j + 1) * FFN_TILE])
        up = _dot(xn, win_ref[:, FFN_HIDDEN + j * FFN_TILE:FFN_HIDDEN + (j + 1) * FFN_TILE])
        act = (_silu(gate) * up).astype(BF16)
        acc = acc + _dot(act, wout_ref[j * FFN_TILE:(j + 1) * FFN_TILE, :])
    if has_final:
        acc = _rms(acc, gf_ref[...])
    out_ref[...] = acc


def _ffn(x, g, w_in, w_out, attn=None, g_final=None, tm=512):
    n = x.shape[0]
    tm = min(tm, n)
    row = lambda i: (i, 0)
    args = [x]
    specs = [pl.BlockSpec((tm, D_MODEL), row)]
    if attn is not None:
        oc, od, wo = attn
        args += [oc, od, wo]
        specs += [pl.BlockSpec((tm, 512), row), pl.BlockSpec((tm, 512), row), _const_spec((D_MODEL, D_MODEL))]
    args += [g, w_in, w_out]
    specs += [_const_spec((1, D_MODEL)), _const_spec((D_MODEL, 2 * FFN_HIDDEN)), _const_spec((FFN_HIDDEN, D_MODEL))]
    if g_final is not None:
        args.append(g_final)
        specs.append(_const_spec((1, D_MODEL)))
    kern = functools.partial(_ffn_kernel, has_attn=attn is not None, has_final=g_final is not None)
    return pl.pallas_call(
        kern,
        grid=(n // tm,),
        in_specs=specs,
        out_specs=pl.BlockSpec((tm, D_MODEL), row),
        out_shape=jax.ShapeDtypeStruct((n, D_MODEL), F32),
        compiler_params=_params(("parallel",)),
        name="ffn",
    )(*args)


def _cd_proj_kernel(x_ref, g_ref, w_ref, fb_ref, fq_ref, ck_ref, cv_ref, dq_ref, dk_ref, dv_ref, iq_ref,
                    misc_ref, iw_ref):
    xn = _rms(x_ref[...], g_ref[...]).astype(BF16)
    h = _dot(xn, w_ref[...])
    fq_ref[...] = (h[:, 0:512] * (C_HD ** -0.5)).astype(BF16)
    ck_ref[...] = h[:, 512:1024]
    cv_ref[...] = h[:, 1024:1536]
    dq_ref[...] = (h[:, 1536:2560] * (D_HD ** -0.5)).astype(BF16)
    dk_ref[...] = h[:, 2560:2688]
    dv_ref[...] = h[:, 2688:2816]
    iq_ref[...] = h[:, 2816:3072].astype(BF16)
    misc = h[:, 3072:3200]
    lane = _iota((1, LANES), 1)
    is_f = (lane >= IDX_DIM) & (lane < IDX_DIM + C_HEADS)
    misc_ref[...] = jnp.where(is_f, _log_sigmoid(misc + fb_ref[...]), misc)
    iw_ref[...] = h[:, 3200:3328]


def _cd_proj(x, g, w, fb, tm=512):
    n = x.shape[0]
    tm = min(tm, n)
    row = lambda i: (i, 0)
    widths = [(512, BF16), (512, F32), (512, F32), (1024, BF16), (128, F32), (128, F32), (256, BF16),
              (128, F32), (128, F32)]
    return pl.pallas_call(
        _cd_proj_kernel,
        grid=(n // tm,),
        in_specs=[pl.BlockSpec((tm, D_MODEL), row), _const_spec((1, D_MODEL)), _const_spec((D_MODEL, CD_COLS)),
                  _const_spec((1, LANES))],
        out_specs=[pl.BlockSpec((tm, w_), row) for w_, _ in widths],
        out_shape=[jax.ShapeDtypeStruct((n, w_), dt) for w_, dt in widths],
        compiler_params=_params(("parallel",)),
        name="cd_proj",
    )(x, g, w, fb)


def _aug_consts(is_query):
    p = np.zeros((3, LANES, 8 * LANES), np.float32)
    ones = np.zeros((1, 8 * LANES), np.float32)
    for h in range(8):
        off = h * LANES + (64 if h % 2 == 0 else 0)
        for c in range(3):
            if is_query:
                p[c, h, off + c] = 1.0
                ones[0, off + 3 + c] = 1.0
            else:
                p[c, h, off + 3 + c] = -1.0
                ones[0, off + c] = 1.0
    return jnp.asarray(p, BF16), jnp.asarray(ones, F32)


def _aug_lanes(cum, p_ref, ones_ref):
    hi, mid, lo = _split3(cum)
    return _dot(hi, p_ref[0]) + _dot(mid, p_ref[1]) + _dot(lo, p_ref[2]) + ones_ref[...]


def _fox_pack_kernel(k_ref, v_ref, lf_ref, p_ref, ones_ref, ka_ref, vt_ref, cum_ref, carry):
    tm = k_ref.shape[1]

    @pl.when(pl.program_id(1) == 0)
    def _():
        carry[...] = jnp.zeros_like(carry)

    tri = jnp.where(_iota((tm, tm), 1) <= _iota((tm, tm), 0), 1.0, 0.0).astype(BF16)
    cum = _tri_dot(tri, lf_ref[0]) + carry[...]
    carry[...] = cum[tm - 1:tm, :]
    cum_ref[0] = cum
    aug = _aug_lanes(cum, p_ref, ones_ref)
    lane = _iota((1, LANES), 1)
    kk, vv = k_ref[0], v_ref[0]
    for h in range(8):
        pr = slice((h // 2) * LANES, (h // 2 + 1) * LANES)
        own = (lane < 64) if h % 2 == 0 else (lane >= 64)
        ka_ref[0, h] = jnp.where(own, kk[:, pr], aug[:, h * LANES:(h + 1) * LANES]).astype(BF16)
    for pr in range(4):
        vt_ref[0, pr, 0] = vv[:, pr * LANES:(pr + 1) * LANES].T.astype(BF16)


def _fox_pack(k, v, lf, tm):
    bsz, s, _ = k.shape
    p, ones = _aug_consts(False)
    return pl.pallas_call(
        _fox_pack_kernel,
        grid=(bsz, s // tm),
        in_specs=[pl.BlockSpec((1, tm, 512), lambda b, t: (b, t, 0)),
                  pl.BlockSpec((1, tm, 512), lambda b, t: (b, t, 0)),
                  pl.BlockSpec((1, tm, LANES), lambda b, t: (b, t, 0)),
                  _const_spec((3, LANES, 8 * LANES)), _const_spec((1, 8 * LANES))],
        out_specs=[pl.BlockSpec((1, 8, tm, LANES), lambda b, t: (b, 0, t, 0)),
                   pl.BlockSpec((1, 4, 1, LANES, tm), lambda b, t: (b, 0, t, 0, 0)),
                   pl.BlockSpec((1, tm, LANES), lambda b, t: (b, t, 0))],
        out_shape=[jax.ShapeDtypeStruct((bsz, 8, s, LANES), BF16),
                   jax.ShapeDtypeStruct((bsz, 4, s // tm, LANES, tm), BF16),
                   jax.ShapeDtypeStruct((bsz, s, LANES), F32)],
        scratch_shapes=[pltpu.VMEM((1, LANES), F32)],
        compiler_params=_params(("parallel", "arbitrary")),
        name="fox_pack",
    )(k, v, lf, p, ones)


def _q_pack_kernel(q_ref, cum_ref, p_ref, ones_ref, qa_ref):
    aug = _aug_lanes(cum_ref[0], p_ref, ones_ref)
    lane = _iota((1, LANES), 1)
    qq = q_ref[0].astype(F32)
    for h in range(8):
        pr = slice((h // 2) * LANES, (h // 2 + 1) * LANES)
        own = (lane < 64) if h % 2 == 0 else (lane >= 64)
        qa_ref[0, h] = jnp.where(own, qq[:, pr], aug[:, h * LANES:(h + 1) * LANES]).astype(BF16)


def _q_pack(q, cum_q, tm):
    bsz, tq, _ = q.shape
    p, ones = _aug_consts(True)
    return pl.pallas_call(
        _q_pack_kernel,
        grid=(bsz, tq // tm),
        in_specs=[pl.BlockSpec((1, tm, 512), lambda b, t: (b, t, 0)),
                  pl.BlockSpec((1, tm, LANES), lambda b, t: (b, t, 0)),
                  _const_spec((3, LANES, 8 * LANES)), _const_spec((1, 8 * LANES))],
        out_specs=pl.BlockSpec((1, 8, tm, LANES), lambda b, t: (b, 0, t, 0)),
        out_shape=jax.ShapeDtypeStruct((bsz, 8, tq, LANES), BF16),
        compiler_params=_params(("parallel", "parallel")),
        name="q_pack",
    )(q, cum_q, p, ones)


def _fox_attn_kernel(q_ref, k_ref, vt_ref, o_ref, *, tq, tk, q_start):
    i = pl.program_id(2)
    q_lo = q_start + i * tq
    n_full = (q_lo + 1) // tk
    n_tiles = (q_lo + tq + tk - 1) // tk
    q_pos = q_lo + _iota((1, tq), 1)
    qs = (q_ref[0, 0], q_ref[0, 1])

    def body(j, carry, masked):
        off = pl.multiple_of(j * tk, tk)
        ss = [_dot_nt(k_ref[0, hh, pl.ds(off, tk), :], qs[hh]) for hh in range(2)]
        if masked:
            ok = (off + _iota((tk, 1), 0)) <= q_pos
            ss = [jnp.where(ok, s, MASKED) for s in ss]
        out = []
        for hh in range(2):
            m, l, acc = carry[hh]
            m_new = jnp.maximum(m, jnp.max(ss[hh], axis=0, keepdims=True))
            alpha = jnp.exp(m - m_new)
            p = jnp.exp(ss[hh] - m_new)
            l = alpha * l + jnp.sum(p, axis=0, keepdims=True)
            acc = alpha * acc + _dot(vt_ref[0, 0, j, hh * 64:(hh + 1) * 64, :], p.astype(BF16))
            out.append((m_new, l, acc))
        return tuple(out)

    init = (jnp.full((1, tq), NEG, F32), jnp.zeros((1, tq), F32), jnp.zeros((64, tq), F32))
    carry = lax.fori_loop(0, n_full, functools.partial(body, masked=False), (init, init))
    carry = lax.fori_loop(n_full, n_tiles, functools.partial(body, masked=True), carry)
    o_t = jnp.concatenate([acc / l for _, l, acc in carry], axis=0)
    o_ref[0] = o_t.T.astype(BF16)


def _fox_attn(q_aug, k_aug, vt, tq, tk, q_start):
    bsz, _, t_q, _ = q_aug.shape
    s = k_aug.shape[2]
    kern = functools.partial(_fox_attn_kernel, tq=tq, tk=tk, q_start=q_start)
    return pl.pallas_call(
        kern,
        grid=(bsz, 4, t_q // tq),
        in_specs=[pl.BlockSpec((1, 2, tq, LANES), lambda b, p, i: (b, p, i, 0)),
                  pl.BlockSpec((1, 2, s, LANES), lambda b, p, i: (b, p, 0, 0)),
                  pl.BlockSpec((1, 1, s // tk, LANES, tk), lambda b, p, i: (b, p, 0, 0, 0))],
        out_specs=pl.BlockSpec((1, tq, LANES), lambda b, p, i: (b, i, p)),
        out_shape=jax.ShapeDtypeStruct((bsz, t_q, 512), BF16),
        compiler_params=_params(("parallel", "parallel", "arbitrary")),
        name="fox_attn",
    )(q_aug, k_aug, vt)


def _dsa_pack_kernel(dk_ref, dv_ref, ik_ref, k2_ref, vt_ref, ik2_ref):
    lo = _iota((1, LANES), 1) < 64
    k2_ref[0] = dk_ref[0].astype(BF16)
    vt_ref[0, 0] = dv_ref[0].T.astype(BF16)
    ik = jnp.where(lo, ik_ref[0], 0.0)
    ik2_ref[0, 0] = ik.astype(BF16)
    ik2_ref[0, 1] = pltpu.roll(ik, 64, 1).astype(BF16)


def _dsa_pack(dk, dv, ik, tm):
    bsz, s, _ = dk.shape
    tok = pl.BlockSpec((1, tm, LANES), lambda b, t: (b, t, 0))
    return pl.pallas_call(
        _dsa_pack_kernel,
        grid=(bsz, s // tm),
        in_specs=[tok, tok, tok],
        out_specs=[tok,
                   pl.BlockSpec((1, 1, LANES, tm), lambda b, t: (b, t, 0, 0)),
                   pl.BlockSpec((1, 2, tm, LANES), lambda b, t: (b, 0, t, 0))],
        out_shape=[jax.ShapeDtypeStruct((bsz, s, LANES), BF16),
                   jax.ShapeDtypeStruct((bsz, s // tm, LANES, tm), BF16),
                   jax.ShapeDtypeStruct((bsz, 2, s, LANES), BF16)],
        compiler_params=_params(("parallel", "parallel")),
        name="dsa_pack",
    )(dk, dv, ik)


def _dsa_attn_kernel(dq_ref, iq_ref, iw_ref, ik2_ref, k2_ref, vt_ref, o_ref, keys_sc, m_sc, l_sc, acc_sc,
                     *, tq, tk, q_start, s_valid, topk):
    i = pl.program_id(1)
    q_lo = q_start + i * tq
    q_chunk = (q_lo + _iota((1, tq), 1)) // CHUNK
    adm_end = jnp.minimum(((q_lo + tq - 1) // CHUNK + 1) * CHUNK, s_valid)
    n_tiles = (adm_end + tk - 1) // tk
    iw_t = iw_ref[0].T
    iq = iq_ref[0]

    def score_body(j, c):
        off = pl.multiple_of(j * tk, tk)
        sc = jnp.zeros((tk, tq), F32)
        for hd in range(IDX_HEADS):
            qp = iq[:, (hd // 2) * LANES:(hd // 2 + 1) * LANES]
            r = jnp.maximum(_dot_nt(ik2_ref[0, hd % 2, pl.ds(off, tk), :], qp), 0.0)
            sc = sc + r * iw_t[hd:hd + 1, :]
        sc = sc * IDX_SCALE + 0.0
        kpos = off + _iota((tk, 1), 0)
        adm = ((kpos // CHUNK) <= q_chunk) & (kpos < s_valid)
        bits = lax.bitcast_convert_type(sc, I32)
        key = bits ^ ((bits >> 31) & 0x7FFFFFFF)
        keys_sc[j] = jnp.where(adm, key, INT_MIN)
        return c

    lax.fori_loop(0, n_tiles, score_body, 0)

    def count(pred):
        def body(j, c):
            hit = jnp.where(pred(j, keys_sc[j]), 1.0, 0.0)
            return c + jnp.sum(hit.reshape(tk // 32, 4, 8, tq), axis=0)
        part = lax.fori_loop(0, n_tiles, body, jnp.zeros((4, 8, tq), F32))
        return jnp.sum(jnp.sum(part, axis=0), axis=0, keepdims=True)

    def tau_body(b, tau):
        cand = tau + lax.shift_left(jnp.int32(1), 31 - b)
        cnt = count(lambda j, kk: kk >= cand)
        return jnp.where(cnt >= topk, cand, tau)

    tau = lax.fori_loop(0, 32, tau_body, jnp.full((1, tq), INT_MIN, I32))
    need = jnp.where(tau == INT_MIN, float(2 ** 20), topk - count(lambda j, kk: kk > tau))
    tied = jnp.max(count(lambda j, kk: kk == tau) - need) > 0.0

    def pos_body(b, p):
        cand = p + lax.shift_left(jnp.int32(1), 12 - b)

        def pred(j, kk):
            kpos = j * tk + _iota((tk, tq), 0)
            return (kk == tau) & (kpos < cand)
        cnt = count(pred)
        return jnp.where(cnt < need, cand, p)

    p_cut = lax.fori_loop(0, jnp.where(tied, 13, 0), pos_body,
                          jnp.full((1, tq), jnp.where(tied, 0, 2 ** 20), I32))

    m_sc[...] = jnp.full(m_sc.shape, NEG, F32)
    l_sc[...] = jnp.zeros(l_sc.shape, F32)
    acc_sc[...] = jnp.zeros(acc_sc.shape, F32)
    n_rep = D_HEADS // D_KV_HEADS
    q_stack = [jnp.concatenate([dq_ref[0, :, hd * LANES:(hd + 1) * LANES]
                                for hd in range(g * n_rep, (g + 1) * n_rep)], axis=0) for g in range(D_KV_HEADS)]

    def attn_body(j, c):
        off = pl.multiple_of(j * tk, tk)
        kk = keys_sc[j]
        kpos = off + _iota((tk, tq), 0)
        sel = ((kk > tau) | ((kk == tau) & (kpos <= p_cut))) & (kk != INT_MIN)
        bias = jnp.where(sel, 0.0, MASKED)
        bias = jnp.concatenate([bias] * n_rep, axis=1)
        kt = k2_ref[0, pl.ds(off, tk), :]
        for g in range(D_KV_HEADS):
            s = _dot_nt(kt, q_stack[g]) + bias
            m_old = m_sc[g]
            m_new = jnp.maximum(m_old, jnp.max(s, axis=0, keepdims=True))
            alpha = jnp.exp(m_old - m_new)
            p = jnp.exp(s - m_new)
            l_sc[g] = alpha * l_sc[g] + jnp.sum(p, axis=0, keepdims=True)
            acc_sc[g] = alpha * acc_sc[g] + _dot(vt_ref[0, j, g * 64:(g + 1) * 64, :], p.astype(BF16))
            m_sc[g] = m_new
        return c

    lax.fori_loop(0, n_tiles, attn_body, 0)
    for g in range(D_KV_HEADS):
        o_g = acc_sc[g] / l_sc[g]
        for pr in range(n_rep // 2):
            o_t = jnp.concatenate([o_g[:, (2 * pr) * tq:(2 * pr + 1) * tq],
                                   o_g[:, (2 * pr + 1) * tq:(2 * pr + 2) * tq]], axis=0)
            col = (g * n_rep // 2 + pr) * LANES
            o_ref[0, :, col:col + LANES] = o_t.T.astype(BF16)


def _dsa_attn(dq, iq, iw, ik2, k2, vt, tq, tk, q_start, s_valid):
    bsz, t_q, _ = dq.shape
    s = k2.shape[1]
    topk = min(IDX_TOPK_MAX, s_valid // 4)
    kern = functools.partial(_dsa_attn_kernel, tq=tq, tk=tk, q_start=q_start, s_valid=s_valid, topk=float(topk))
    return pl.pallas_call(
        kern,
        grid=(bsz, t_q // tq),
        in_specs=[pl.BlockSpec((1, tq, 8 * LANES), lambda b, i: (b, i, 0)),
                  pl.BlockSpec((1, tq, 256), lambda b, i: (b, i, 0)),
                  pl.BlockSpec((1, tq, LANES), lambda b, i: (b, i, 0)),
                  pl.BlockSpec((1, 2, s, LANES), lambda b, i: (b, 0, 0, 0)),
                  pl.BlockSpec((1, s, LANES), lambda b, i: (b, 0, 0)),
                  pl.BlockSpec((1, s // tk, LANES, tk), lambda b, i: (b, 0, 0, 0))],
        out_specs=pl.BlockSpec((1, tq, 512), lambda b, i: (b, i, 0)),
        out_shape=jax.ShapeDtypeStruct((bsz, t_q, 512), BF16),
        scratch_shapes=[pltpu.VMEM((s // tk, tk, tq), I32),
                        pltpu.VMEM((D_KV_HEADS, 1, D_HEADS // D_KV_HEADS * tq), F32),
                        pltpu.VMEM((D_KV_HEADS, 1, D_HEADS // D_KV_HEADS * tq), F32),
                        pltpu.VMEM((D_KV_HEADS, 64, D_HEADS // D_KV_HEADS * tq), F32)],
        compiler_params=_params(("parallel", "arbitrary")),
        name="dsa_attn",
    )(dq, iq, iw, ik2, k2, vt)


def _prep_ab(w_in, w_gk):
    w = jnp.pad(w_in, ((0, 0), (0, AB_COLS - w_in.shape[1]))).astype(BF16)
    wgk = jnp.pad(w_gk, ((0, LANES - B_GATE_RANK), (0, 0))).astype(BF16)
    return w, wgk


def _prep_cd(w_in, f_bias):
    o = np.cumsum([0, 512, 512, 512, C_HEADS, 512, 128, 128, 256, IDX_DIM, IDX_HEADS])
    c_q, c_k, c_v, c_f, d_q, d_k, d_v, d_iq, d_ik, d_iw = (w_in[:, o[i]:o[i + 1]] for i in range(10))
    zeros = lambda n: jnp.zeros((D_MODEL, n), w_in.dtype)
    dq_cols = []
    for h in range(D_HEADS):
        g = h // (D_HEADS // D_KV_HEADS)
        wh = d_q[:, h * 64:(h + 1) * 64]
        dq_cols += [wh, zeros(64)] if g == 0 else [zeros(64), wh]
    w = jnp.concatenate([c_q, c_k, c_v] + dq_cols + [d_k, d_v, d_iq, d_ik, c_f, zeros(LANES - IDX_DIM - C_HEADS),
                                                     d_iw, zeros(LANES - IDX_HEADS)], axis=1).astype(BF16)
    fb = jnp.pad(f_bias.astype(F32), (IDX_DIM, LANES - IDX_DIM - C_HEADS)).reshape(1, LANES)
    return w, fb


def _pad_rows(z, s):
    return jnp.pad(z, ((0, 0), (0, s - z.shape[1]), (0, 0)))


def _pad_lanes(z, n=LANES):
    return jnp.pad(z, ((0, 0), (0, 0), (0, n - z.shape[2])))


def _trunk(x, s_a, s_b, cache, wts):
    bsz, t_len, _ = x.shape
    row = lambda z: z.reshape(1, -1).astype(F32)

    t_pad = -(-t_len // REC_CHUNK) * REC_CHUNK
    x1, sa_new, sb_new = _ab_layer(
        _pad_rows(x, t_pad), s_a, s_b.reshape(bsz, 2, 128, 128), row(wts['norm_mix'][0]), wts['ab_w_in'],
        wts['gla_w_gk'], row(wts['gla_b_gk']), row(wts['lb']), row(wts['hgrn_gnorm']), row(wts['gla_gnorm']),
        wts['ab_w_out'], t_len)
    n = bsz * t_len
    x1 = x1[:, :t_len].reshape(n, D_MODEL)
    x2 = _ffn(x1, row(wts['norm_ffn'][0]), wts['ffn_w_in'][0], wts['ffn_w_out'][0])

    fq, ck, cv, dq, dk, dv, iq, misc, iw = _cd_proj(x2, row(wts['norm_mix'][1]), wts['cd_w_in'], wts['fox_f_bias'])
    per_b = lambda z: z.reshape(bsz, t_len, z.shape[-1])
    fq, ck, cv, dq, dk, dv, iq, misc, iw = map(per_b, (fq, ck, cv, dq, dk, dv, iq, misc, iw))
    logf = misc[:, :, IDX_DIM:IDX_DIM + C_HEADS]
    if cache is None:
        q_start = 0
        k_all, v_all, lf_all, dk_all, dv_all, ik_all = ck, cv, _pad_lanes(logf), dk, dv, misc
    else:
        c_k, c_v, c_lf, c_dk, c_dv, c_ik = cache
        q_start = c_k.shape[1]
        cat = lambda c, r: jnp.concatenate([c.reshape(bsz, q_start, -1).astype(F32), r], axis=1)
        k_all, v_all, dk_all, dv_all = cat(c_k, ck), cat(c_v, cv), cat(c_dk, dk), cat(c_dv, dv)
        lf_all = _pad_lanes(cat(c_lf, logf))
        ik_all = cat(_pad_lanes(c_ik), misc)
    s_valid = q_start + t_len
    tk = 512
    s_pad = -(-s_valid // tk) * tk
    k_all, v_all, lf_all, dk_all, dv_all, ik_all = (_pad_rows(z, s_pad) for z in
                                                    (k_all, v_all, lf_all, dk_all, dv_all, ik_all))
    tq_pad = -(-t_len // LANES) * LANES
    fox_tq = 256 if tq_pad % 256 == 0 else LANES
    k_aug, v_t, cum = _fox_pack(k_all, v_all, lf_all, tm=tk)
    q_aug = _q_pack(_pad_rows(fq, tq_pad), _pad_rows(cum[:, q_start:q_start + t_len], tq_pad), tm=LANES)
    o_c = _fox_attn(q_aug, k_aug, v_t, tq=fox_tq, tk=tk, q_start=q_start)[:, :t_len]
    k2, dv_t, ik2 = _dsa_pack(dk_all, dv_all, ik_all, tm=tk)
    o_d = _dsa_attn(_pad_rows(dq, tq_pad), _pad_rows(iq, tq_pad), _pad_rows(iw, tq_pad), ik2, k2, dv_t,
                    tq=LANES, tk=tk, q_start=q_start, s_valid=s_valid)[:, :t_len]

    y = _ffn(x2, row(wts['norm_ffn'][1]), wts['ffn_w_in'][1], wts['ffn_w_out'][1],
             attn=(o_c.reshape(n, 512), o_d.reshape(n, 512), wts['cd_w_out']), g_final=row(wts['norm_final']))
    rows = (ck.reshape(1, bsz, t_len, C_HEADS, C_HD), cv.reshape(1, bsz, t_len, C_HEADS, C_HD),
            logf[None], dk.reshape(1, bsz, t_len, D_KV_HEADS, D_HD), dv.reshape(1, bsz, t_len, D_KV_HEADS, D_HD),
            misc[None, :, :, :IDX_DIM])
    return (y.reshape(bsz, t_len, D_MODEL), sa_new[None], sb_new.reshape(1, bsz, B_HEADS, B_DK, B_DV)) + rows


def kernel(x_prompt, x_sample, state_hgrn, state_gla, cache_fox_k, cache_fox_v, cache_fox_logf, cache_dsa_k, cache_dsa_v, cache_dsa_ik, norm_mix, norm_ffn, norm_final, ab_w_in, ab_w_out, hgrn_lb_logits, hgrn_gnorm, gla_w_gk, gla_b_gk, gla_gnorm, cd_w_in, cd_w_out, fox_f_bias, ffn_w_in, ffn_w_out):
    lbs = jnp.cumsum(jax.nn.softmax(hgrn_lb_logits.astype(F32), axis=0), axis=0)
    w_ab, w_gk = _prep_ab(ab_w_in[0], gla_w_gk[0])
    w_cd, fb = _prep_cd(cd_w_in[0], fox_f_bias[0])
    wts = dict(norm_mix=norm_mix, norm_ffn=norm_ffn, norm_final=norm_final, ab_w_in=w_ab,
               ab_w_out=ab_w_out[0].astype(BF16), lb=lbs[0], hgrn_gnorm=hgrn_gnorm[0], gla_w_gk=w_gk,
               gla_b_gk=gla_b_gk[0], gla_gnorm=gla_gnorm[0], cd_w_in=w_cd, cd_w_out=cd_w_out[0].astype(BF16),
               fox_f_bias=fb, ffn_w_in=ffn_w_in.astype(BF16), ffn_w_out=ffn_w_out.astype(BF16))
    bp = x_prompt.shape[0]
    p_out = _trunk(x_prompt, jnp.zeros((bp, A_HEADS, A_DK, A_DV), F32), jnp.zeros((bp, B_HEADS, B_DK, B_DV), F32),
                   None, wts)
    cache = (cache_fox_k[0], cache_fox_v[0], cache_fox_logf[0], cache_dsa_k[0], cache_dsa_v[0], cache_dsa_ik[0])
    s_out = _trunk(x_sample, state_hgrn[0], state_gla[0], cache, wts)
    return (p_out[0], s_out[0]) + tuple(p_out[1:]) + tuple(s_out[1:])
```

```python
import functools

import numpy as np
import jax
import jax.numpy as jnp
from jax import lax
from jax.experimental import pallas as pl
from jax.experimental.pallas import tpu as pltpu

F32 = jnp.float32
BF16 = jnp.bfloat16
I32 = jnp.int32

D_MODEL = 1024
CHUNK = 64
A_HEADS, A_DK, A_DV = 4, 128, 128
B_HEADS, B_DK, B_DV = 4, 64, 128
B_GATE_RANK = 16
B_GATE_NORM = 16.0
C_HEADS, C_HD = 8, 64
D_HEADS, D_KV_HEADS, D_HD = 8, 2, 64
IDX_HEADS, IDX_DIM = 4, 64
IDX_TOPK_MAX = 256
IDX_SCALE = (IDX_DIM ** -0.5) * (IDX_HEADS ** -0.5)
FFN_HIDDEN = ((8 * D_MODEL // 3 + 255) // 256) * 256

LANES = 128
RSUB = 16
REC_CHUNK = 128
VMEM_LIMIT = 56 * 1024 * 1024
NEG = -1e30
MASKED = -2e30
LOG2E = 1.4426950408889634
ONES_ROWS = 16
INT_MIN = -2 ** 31

AB_COLS = 4 * 512 + 256 + 256 + 512 + 512 + LANES
CD_COLS = 3 * 512 + 8 * LANES + 2 * LANES + 256 + LANES + LANES


def _dot(a, b):
    return jnp.dot(a, b, preferred_element_type=F32)


def _dot_nt(a, b):
    return lax.dot_general(a, b, (((1,), (1,)), ((), ())), preferred_element_type=F32)


def _rms(x, g, eps=1e-6):
    return x * lax.rsqrt(jnp.mean(x * x, axis=-1, keepdims=True) + eps) * g


def _silu(x):
    return x * jax.nn.sigmoid(x)


def _log_sigmoid(x):
    return jnp.minimum(x, 0.0) - jnp.log1p(jnp.exp(-jnp.abs(x)))


def _split3(x):
    hi = x.astype(BF16)
    r = x - hi.astype(F32)
    mid = r.astype(BF16)
    lo = (r - mid.astype(F32)).astype(BF16)
    return hi, mid, lo


def _tri_dot(tri, x):
    hi, mid, lo = _split3(x)
    return _dot(tri, hi) + _dot(tri, mid) + _dot(tri, lo)


def _iota(shape, dim):
    return lax.broadcasted_iota(I32, shape, dim)


def _const_spec(shape):
    zeros = (0,) * len(shape)
    return pl.BlockSpec(shape, lambda *_: zeros, pipeline_mode=pl.Buffered(1))


def _params(sem):
    return pltpu.CompilerParams(dimension_semantics=sem, vmem_limit_bytes=VMEM_LIMIT)


def _ab_kernel(x_ref, sa_ref, sb_ref, g_ref, win_ref, wgk_ref, bgk_ref, lb_ref, agn_ref, bgn_ref, wout_ref,
               xo_ref, sao_ref, sbo_ref, s_sc, h_sc, o_sc, *, t_valid, t_pad):
    C = REC_CHUNK
    tm = x_ref.shape[1]
    t = pl.program_id(1)

    @pl.when(t == 0)
    def _():
        s_sc[0:4] = sa_ref[0]
        s_sc[4:6] = sb_ref[0]

    h_sc[...] = _dot(_rms(x_ref[0], g_ref[...]).astype(BF16), win_ref[...])

    def chunk_body(c, carry):
        r0 = pl.multiple_of(c * C, C)
        _ab_chunk(r0, t * tm + r0, h_sc, o_sc, s_sc, wgk_ref, bgk_ref, lb_ref, agn_ref, bgn_ref,
                  t_valid=t_valid, t_pad=t_pad)
        return carry

    lax.fori_loop(0, tm // C, chunk_body, 0)
    xo_ref[0] = x_ref[0] + _dot(o_sc[...], wout_ref[...])

    @pl.when(t == pl.num_programs(1) - 1)
    def _():
        sao_ref[0] = s_sc[0:4]
        sbo_ref[0] = s_sc[4:6]


def _ab_chunk(r0, row0, h_sc, o_sc, s_sc, wgk_ref, bgk_ref, lb_ref, agn_ref, bgn_ref, *, t_valid, t_pad):
    C = REC_CHUNK
    cols = lambda a, b: h_sc[pl.ds(r0, C), a:b]
    a_q, a_f, a_i, a_g = cols(0, 512), cols(512, 1024), cols(1024, 1536), cols(1536, 2048)
    b_q, b_k, b_v, b_g = cols(2048, 2304), cols(2304, 2560), cols(2560, 3072), cols(3072, 3584)
    b_lr = cols(3584, 3712)

    lb = lb_ref[...]
    f = lb + (1.0 - lb) * jax.nn.sigmoid(a_f)
    gk = _dot(b_lr.astype(BF16), wgk_ref[...]) + bgk_ref[...]
    la = jnp.concatenate([jnp.log(f), _log_sigmoid(gk) * (1.0 / B_GATE_NORM)], axis=1)
    q = jnp.concatenate([_silu(a_q), b_q * (B_DK ** -0.5)], axis=1)
    k = jnp.concatenate([1.0 - f, b_k], axis=1)
    v_a, v_b = a_i, b_v
    if t_valid < t_pad:
        ok = (row0 + _iota((C, 1), 0)) < t_valid
        la = jnp.where(ok, la, 0.0)
        k = jnp.where(ok, k, 0.0)
        v_a = jnp.where(ok, v_a, 0.0)
        v_b = jnp.where(ok, v_b, 0.0)

    row = _iota((C, C), 0)
    col = _iota((C, C), 1)
    causal = col <= row
    tri = jnp.where(causal, 1.0, 0.0).astype(BF16)
    tri_in = jnp.where(causal & ((row >> 4) == (col >> 4)), 1.0, 0.0).astype(BF16)
    bc = _tri_dot(tri, la)
    b_in = _tri_dot(tri_in, la)
    lane = _iota((1, LANES), 1)
    lo_half = lane < 64
    srow = _iota((LANES, 1), 0) < 64

    o_heads = [None] * 8
    for u in range(6):
        sl = slice(u * LANES, (u + 1) * LANES)
        qu, ku, bcu = q[:, sl], k[:, sl], bc[:, sl]
        s_old = s_sc[u]
        bend = bcu[C - 1:C, :]
        qt = qu * jnp.exp(b_in[:, sl])
        qdec = qu * jnp.exp(bcu)
        if u < 4:
            heads = [(u, None, v_a[:, sl])]
        else:
            ha = 4 + 2 * (u - 4)
            heads = [(ha, lo_half, v_b[:, (ha - 4) * LANES:(ha - 3) * LANES]),
                     (ha + 1, jnp.logical_not(lo_half), v_b[:, (ha - 3) * LANES:(ha - 2) * LANES])]
        a_rows = [[] for _ in heads]
        for i in range(C // RSUB):
            n = RSUB * (i + 1)
            if i == 0:
                kt = ku[0:n] * jnp.exp(-bcu[0:n])
            else:
                kt = ku[0:n] * jnp.exp(bcu[RSUB * i - 1:RSUB * i, :] - bcu[0:n])
            if n < C:
                kt = jnp.concatenate([kt, jnp.zeros((C - n, LANES), F32)], axis=0)
            ktb = kt.astype(BF16)
            qi = qt[RSUB * i:RSUB * (i + 1)]
            for hi_, (_, msk, _) in enumerate(heads):
                qim = qi if msk is None else jnp.where(msk, qi, 0.0)
                a_rows[hi_].append(_dot_nt(qim.astype(BF16), ktb))
        sb16 = s_old.astype(BF16)
        for hi_, (hd, msk, vh) in enumerate(heads):
            att = jnp.where(causal, jnp.concatenate(a_rows[hi_], axis=0), 0.0)
            qd = qdec if msk is None else jnp.where(msk, qdec, 0.0)
            o_heads[hd] = _dot(qd.astype(BF16), sb16) + _dot(att.astype(BF16), vh.astype(BF16))
        kht = (ku * jnp.exp(bend - bcu)).T
        dcol = jnp.broadcast_to(jnp.exp(bend), (LANES, LANES)).T
        if u < 4:
            upd = _dot(kht.astype(BF16), heads[0][2].astype(BF16))
        else:
            lhs = jnp.concatenate([jnp.where(srow, kht, 0.0), jnp.where(srow, 0.0, kht)], axis=1)
            rhs = jnp.concatenate([heads[0][2], heads[1][2]], axis=0)
            upd = _dot(lhs.astype(BF16), rhs.astype(BF16))
        s_sc[u] = dcol * s_old + upd

    outs = []
    for hd in range(8):
        if hd < 4:
            gn, gate = agn_ref[...], a_g[:, hd * LANES:(hd + 1) * LANES]
        else:
            gn, gate = bgn_ref[...], b_g[:, (hd - 4) * LANES:(hd - 3) * LANES]
        outs.append(_rms(o_heads[hd], gn) * _silu(gate))
    o_sc[pl.ds(r0, C), :] = jnp.concatenate(outs, axis=1).astype(BF16)


def _ab_layer(x, s_a, s_b, g, w_in, w_gk, b_gk, lb, a_gn, b_gn, w_out, t_valid):
    bsz, t_pad, _ = x.shape
    tm = 512 if t_pad % 512 == 0 else REC_CHUNK
    kern = functools.partial(_ab_kernel, t_valid=t_valid, t_pad=t_pad)
    return pl.pallas_call(
        kern,
        grid=(bsz, t_pad // tm),
        in_specs=[
            pl.BlockSpec((1, tm, D_MODEL), lambda b, t: (b, t, 0)),
            pl.BlockSpec((1, 4, 128, 128), lambda b, t: (b, 0, 0, 0)),
            pl.BlockSpec((1, 2, 128, 128), lambda b, t: (b, 0, 0, 0)),
            _const_spec((1, D_MODEL)),
            _const_spec((D_MODEL, AB_COLS)),
            _const_spec((LANES, 256)),
            _const_spec((1, 256)),
            _const_spec((1, 512)),
            _const_spec((1, 128)),
            _const_spec((1, 128)),
            _const_spec((D_MODEL, D_MODEL)),
        ],
        out_specs=[
            pl.BlockSpec((1, tm, D_MODEL), lambda b, t: (b, t, 0)),
            pl.BlockSpec((1, 4, 128, 128), lambda b, t: (b, 0, 0, 0)),
            pl.BlockSpec((1, 2, 128, 128), lambda b, t: (b, 0, 0, 0)),
        ],
        out_shape=[
            jax.ShapeDtypeStruct((bsz, t_pad, D_MODEL), F32),
            jax.ShapeDtypeStruct((bsz, 4, 128, 128), F32),
            jax.ShapeDtypeStruct((bsz, 2, 128, 128), F32),
        ],
        scratch_shapes=[pltpu.VMEM((6, 128, 128), F32), pltpu.VMEM((tm, AB_COLS), F32),
                        pltpu.VMEM((tm, D_MODEL), BF16)],
        compiler_params=_params(("parallel", "arbitrary")),
        name="ab_layer",
    )(x, s_a, s_b, g, w_in, w_gk, b_gk, lb, a_gn, b_gn, w_out)


FFN_TILE = FFN_HIDDEN // 2


def _ffn_kernel(*refs, has_attn, has_final):
    refs = list(refs)
    x_ref = refs.pop(0)
    x = x_ref[...]
    if has_attn:
        oc_ref, od_ref, wo_ref = refs.pop(0), refs.pop(0), refs.pop(0)
        x = x + _dot(jnp.concatenate([oc_ref[...], od_ref[...]], axis=1), wo_ref[...])
    g_ref, win_ref, wout_ref = refs.pop(0), refs.pop(0), refs.pop(0)
    gf_ref = refs.pop(0) if has_final else None
    out_ref = refs.pop(0)
    xn = _rms(x, g_ref[...]).astype(BF16)
    acc = x
    for j in range(FFN_HIDDEN // FFN_TILE):
        gate = _dot(xn, win_ref[:, j * FFN_TILE:(j + 1) * FFN_TILE])
        up = _dot(xn, win_ref[:, FFN_HIDDEN + j * FFN_TILE:FFN_HIDDEN + (j + 1) * FFN_TILE])
        act = (_silu(gate) * up).astype(BF16)
        acc = acc + _dot(act, wout_ref[j * FFN_TILE:(j + 1) * FFN_TILE, :])
    if has_final:
        acc = _rms(acc, gf_ref[...])
    out_ref[...] = acc


def _ffn(x, g, w_in, w_out, attn=None, g_final=None, tm=512):
    n = x.shape[0]
    tm = min(tm, n)
    row = lambda i: (i, 0)
    args = [x]
    specs = [pl.BlockSpec((tm, D_MODEL), row)]
    if attn is not None:
        oc, od, wo = attn
        args += [oc, od, wo]
        specs += [pl.BlockSpec((tm, 512), row), pl.BlockSpec((tm, 512), row), _const_spec((D_MODEL, D_MODEL))]
    args += [g, w_in, w_out]
    specs += [_const_spec((1, D_MODEL)), _const_spec((D_MODEL, 2 * FFN_HIDDEN)), _const_spec((FFN_HIDDEN, D_MODEL))]
    if g_final is not None:
        args.append(g_final)
        specs.append(_const_spec((1, D_MODEL)))
    kern = functools.partial(_ffn_kernel, has_attn=attn is not None, has_final=g_final is not None)
    return pl.pallas_call(
        kern,
        grid=(n // tm,),
        in_specs=specs,
        out_specs=pl.BlockSpec((tm, D_MODEL), row),
        out_shape=jax.ShapeDtypeStruct((n, D_MODEL), F32),
        compiler_params=_params(("parallel",)),
        name="ffn",
    )(*args)


def _cd_proj_kernel(x_ref, g_ref, w_ref, fb_ref, fq_ref, ck_ref, cv_ref, dq_ref, dk_ref, dv_ref, iq_ref,
                    misc_ref, iw_ref):
    xn = _rms(x_ref[...], g_ref[...]).astype(BF16)
    h = _dot(xn, w_ref[...])
    fq_ref[...] = (h[:, 0:512] * (C_HD ** -0.5 * LOG2E)).astype(BF16)
    ck_ref[...] = h[:, 512:1024]
    cv_ref[...] = h[:, 1024:1536]
    dq_ref[...] = (h[:, 1536:2560] * (D_HD ** -0.5 * LOG2E)).astype(BF16)
    dk_ref[...] = h[:, 2560:2688]
    dv_ref[...] = h[:, 2688:2816]
    iq_ref[...] = h[:, 2816:3072].astype(BF16)
    misc = h[:, 3072:3200]
    lane = _iota((1, LANES), 1)
    is_f = (lane >= IDX_DIM) & (lane < IDX_DIM + C_HEADS)
    misc_ref[...] = jnp.where(is_f, _log_sigmoid(misc + fb_ref[...]), misc)
    iw_ref[...] = h[:, 3200:3328]


def _cd_proj(x, g, w, fb, tm=512):
    n = x.shape[0]
    tm = min(tm, n)
    row = lambda i: (i, 0)
    widths = [(512, BF16), (512, F32), (512, F32), (1024, BF16), (128, F32), (128, F32), (256, BF16),
              (128, F32), (128, F32)]
    return pl.pallas_call(
        _cd_proj_kernel,
        grid=(n // tm,),
        in_specs=[pl.BlockSpec((tm, D_MODEL), row), _const_spec((1, D_MODEL)), _const_spec((D_MODEL, CD_COLS)),
                  _const_spec((1, LANES))],
        out_specs=[pl.BlockSpec((tm, w_), row) for w_, _ in widths],
        out_shape=[jax.ShapeDtypeStruct((n, w_), dt) for w_, dt in widths],
        compiler_params=_params(("parallel",)),
        name="cd_proj",
    )(x, g, w, fb)


def _aug_consts(is_query):
    p = np.zeros((3, LANES, 8 * LANES), np.float32)
    ones = np.zeros((1, 8 * LANES), np.float32)
    for h in range(8):
        off = h * LANES + (64 if h % 2 == 0 else 0)
        for c in range(3):
            if is_query:
                p[c, h, off + c] = 1.0
                ones[0, off + 3 + c] = 1.0
            else:
                p[c, h, off + 3 + c] = -1.0
                ones[0, off + c] = 1.0
    return jnp.asarray(p, BF16), jnp.asarray(ones, F32)


def _aug_lanes(cum, p_ref, ones_ref):
    hi, mid, lo = _split3(cum * LOG2E)
    return _dot(hi, p_ref[0]) + _dot(mid, p_ref[1]) + _dot(lo, p_ref[2]) + ones_ref[...]


def _fox_pack_kernel(k_ref, v_ref, lf_ref, p_ref, ones_ref, ka_ref, vt_ref, cum_ref, carry):
    tm = k_ref.shape[1]

    @pl.when(pl.program_id(1) == 0)
    def _():
        carry[...] = jnp.zeros_like(carry)

    tri = jnp.where(_iota((tm, tm), 1) <= _iota((tm, tm), 0), 1.0, 0.0).astype(BF16)
    cum = _tri_dot(tri, lf_ref[0]) + carry[...]
    carry[...] = cum[tm - 1:tm, :]
    cum_ref[0] = cum
    aug = _aug_lanes(cum, p_ref, ones_ref)
    lane = _iota((1, LANES), 1)
    kk, vv = k_ref[0], v_ref[0]
    for h in range(8):
        pr = slice((h // 2) * LANES, (h // 2 + 1) * LANES)
        own = (lane < 64) if h % 2 == 0 else (lane >= 64)
        ka_ref[0, h] = jnp.where(own, kk[:, pr], aug[:, h * LANES:(h + 1) * LANES]).astype(BF16)
    for pr in range(4):
        vt_ref[0, pr, 0] = vv[:, pr * LANES:(pr + 1) * LANES].T.astype(BF16)


def _fox_pack(k, v, lf, tm):
    bsz, s, _ = k.shape
    p, ones = _aug_consts(False)
    return pl.pallas_call(
        _fox_pack_kernel,
        grid=(bsz, s // tm),
        in_specs=[pl.BlockSpec((1, tm, 512), lambda b, t: (b, t, 0)),
                  pl.BlockSpec((1, tm, 512), lambda b, t: (b, t, 0)),
                  pl.BlockSpec((1, tm, LANES), lambda b, t: (b, t, 0)),
                  _const_spec((3, LANES, 8 * LANES)), _const_spec((1, 8 * LANES))],
        out_specs=[pl.BlockSpec((1, 8, tm, LANES), lambda b, t: (b, 0, t, 0)),
                   pl.BlockSpec((1, 4, 1, LANES, tm), lambda b, t: (b, 0, t, 0, 0)),
                   pl.BlockSpec((1, tm, LANES), lambda b, t: (b, t, 0))],
        out_shape=[jax.ShapeDtypeStruct((bsz, 8, s, LANES), BF16),
                   jax.ShapeDtypeStruct((bsz, 4, s // tm, LANES, tm), BF16),
                   jax.ShapeDtypeStruct((bsz, s, LANES), F32)],
        scratch_shapes=[pltpu.VMEM((1, LANES), F32)],
        compiler_params=_params(("parallel", "arbitrary")),
        name="fox_pack",
    )(k, v, lf, p, ones)


def _q_pack_kernel(q_ref, cum_ref, p_ref, ones_ref, qa_ref):
    aug = _aug_lanes(cum_ref[0], p_ref, ones_ref)
    lane = _iota((1, LANES), 1)
    qq = q_ref[0].astype(F32)
    for h in range(8):
        pr = slice((h // 2) * LANES, (h // 2 + 1) * LANES)
        own = (lane < 64) if h % 2 == 0 else (lane >= 64)
        qa_ref[0, h] = jnp.where(own, qq[:, pr], aug[:, h * LANES:(h + 1) * LANES]).astype(BF16)


def _q_pack(q, cum_q, tm):
    bsz, tq, _ = q.shape
    p, ones = _aug_consts(True)
    return pl.pallas_call(
        _q_pack_kernel,
        grid=(bsz, tq // tm),
        in_specs=[pl.BlockSpec((1, tm, 512), lambda b, t: (b, t, 0)),
                  pl.BlockSpec((1, tm, LANES), lambda b, t: (b, t, 0)),
                  _const_spec((3, LANES, 8 * LANES)), _const_spec((1, 8 * LANES))],
        out_specs=pl.BlockSpec((1, 8, tm, LANES), lambda b, t: (b, 0, t, 0)),
        out_shape=jax.ShapeDtypeStruct((bsz, 8, tq, LANES), BF16),
        compiler_params=_params(("parallel", "parallel")),
        name="q_pack",
    )(q, cum_q, p, ones)


def _fox_attn_kernel(q_ref, k_ref, vt_ref, o_ref, s_sc, *, tq, tk, q_start):
    i = pl.program_id(2)
    q_lo = q_start + i * tq
    n_full = (q_lo + 1) // tk
    n_tiles = (q_lo + tq + tk - 1) // tk
    q_pos = q_lo + _iota((1, tq), 1)
    qs = (q_ref[0, 0], q_ref[0, 1])

    def logits(j):
        off = pl.multiple_of(j * tk, tk)
        return [_dot_nt(k_ref[0, hh, pl.ds(off, tk), :], qs[hh]) for hh in range(2)]

    s_sc[0], s_sc[1] = logits(0)

    def body(j, carry, masked):
        off = pl.multiple_of(j * tk, tk)
        s_next = logits(jnp.minimum(j + 1, n_tiles - 1))
        ss = [s_sc[0], s_sc[1]]
        if masked:
            ok = (off + _iota((tk, 1), 0)) <= q_pos
            ss = [jnp.where(ok, s, MASKED) for s in ss]
        out = []
        for hh in range(2):
            m, acc = carry[hh]
            m_new = jnp.maximum(m, jnp.max(ss[hh], axis=0, keepdims=True))
            p = jnp.exp2(ss[hh] - m_new).astype(BF16)
            vt = jnp.concatenate([vt_ref[0, 0, j, hh * 64:(hh + 1) * 64, :], ones], axis=0)
            acc = jnp.exp2(m - m_new) * acc + _dot(vt, p)
            out.append((m_new, acc))
        s_sc[0], s_sc[1] = s_next
        return tuple(out)

    ones = jnp.ones((ONES_ROWS, tk), BF16)
    init = (jnp.full((1, tq), NEG, F32), jnp.zeros((64 + ONES_ROWS, tq), F32))
    carry = lax.fori_loop(0, n_full, functools.partial(body, masked=False), (init, init))
    carry = lax.fori_loop(n_full, n_tiles, functools.partial(body, masked=True), carry)
    o_t = jnp.concatenate([acc[0:64] / acc[64:65] for _, acc in carry], axis=0)
    o_ref[0] = o_t.T.astype(BF16)


def _fox_attn(q_aug, k_aug, vt, tq, tk, q_start):
    bsz, _, t_q, _ = q_aug.shape
    s = k_aug.shape[2]
    kern = functools.partial(_fox_attn_kernel, tq=tq, tk=tk, q_start=q_start)
    return pl.pallas_call(
        kern,
        grid=(bsz, 4, t_q // tq),
        in_specs=[pl.BlockSpec((1, 2, tq, LANES), lambda b, p, i: (b, p, i, 0)),
                  pl.BlockSpec((1, 2, s, LANES), lambda b, p, i: (b, p, 0, 0)),
                  pl.BlockSpec((1, 1, s // tk, LANES, tk), lambda b, p, i: (b, p, 0, 0, 0))],
        out_specs=pl.BlockSpec((1, tq, LANES), lambda b, p, i: (b, i, p)),
        out_shape=jax.ShapeDtypeStruct((bsz, t_q, 512), BF16),
        scratch_shapes=[pltpu.VMEM((2, tk, tq), F32)],
        compiler_params=_params(("parallel", "parallel", "arbitrary")),
        name="fox_attn",
    )(q_aug, k_aug, vt)


def _dsa_pack_kernel(dk_ref, dv_ref, ik_ref, k2_ref, vt_ref, ik2_ref):
    lo = _iota((1, LANES), 1) < 64
    k2_ref[0] = dk_ref[0].astype(BF16)
    vt_ref[0, 0] = dv_ref[0].T.astype(BF16)
    ik = jnp.where(lo, ik_ref[0], 0.0)
    ik2_ref[0, 0] = ik.astype(BF16)
    ik2_ref[0, 1] = pltpu.roll(ik, 64, 1).astype(BF16)


def _dsa_pack(dk, dv, ik, tm):
    bsz, s, _ = dk.shape
    tok = pl.BlockSpec((1, tm, LANES), lambda b, t: (b, t, 0))
    return pl.pallas_call(
        _dsa_pack_kernel,
        grid=(bsz, s // tm),
        in_specs=[tok, tok, tok],
        out_specs=[tok,
                   pl.BlockSpec((1, 1, LANES, tm), lambda b, t: (b, t, 0, 0)),
                   pl.BlockSpec((1, 2, tm, LANES), lambda b, t: (b, 0, t, 0))],
        out_shape=[jax.ShapeDtypeStruct((bsz, s, LANES), BF16),
                   jax.ShapeDtypeStruct((bsz, s // tm, LANES, tm), BF16),
                   jax.ShapeDtypeStruct((bsz, 2, s, LANES), BF16)],
        compiler_params=_params(("parallel", "parallel")),
        name="dsa_pack",
    )(dk, dv, ik)


def _dsa_attn_kernel(dq_ref, iq_ref, iw_ref, ik2_ref, k2_ref, vt_ref, o_ref, keys_sc, m_sc, acc_sc, s_sc,
                     *, tq, tk, q_start, s_valid, topk):
    i = pl.program_id(1)
    q_lo = q_start + i * tq
    adm_row = jnp.minimum(((q_lo + _iota((1, tq), 1)) // CHUNK + 1) * CHUNK, s_valid)
    adm_end = jnp.minimum(((q_lo + tq - 1) // CHUNK + 1) * CHUNK, s_valid)
    n_tiles = (adm_end + tk - 1) // tk
    iw_t = iw_ref[0].T
    iq = iq_ref[0]

    def score_body(j, c):
        off = pl.multiple_of(j * tk, tk)
        sc = jnp.zeros((tk, tq), F32)
        for hd in range(IDX_HEADS):
            qp = iq[:, (hd // 2) * LANES:(hd // 2 + 1) * LANES]
            r = jnp.maximum(_dot_nt(ik2_ref[0, hd % 2, pl.ds(off, tk), :], qp), 0.0)
            sc = sc + r * iw_t[hd:hd + 1, :]
        sc = sc * IDX_SCALE + 0.0
        adm = (off + _iota((tk, tq), 0)) < adm_row
        bits = lax.bitcast_convert_type(sc, I32)
        key = bits ^ ((bits >> 31) & 0x7FFFFFFF)
        keys_sc[j] = jnp.where(adm, key, INT_MIN)
        return c

    lax.fori_loop(0, n_tiles, score_body, 0)

    def count(pred):
        def body(j, c):
            hit = jnp.where(pred(j, keys_sc[j]), 1.0, 0.0)
            return c + jnp.sum(hit.reshape(tk // 32, 4, 8, tq), axis=0)
        part = lax.fori_loop(0, n_tiles, body, jnp.zeros((4, 8, tq), F32))
        return jnp.sum(jnp.sum(part, axis=0), axis=0, keepdims=True)

    def tau_body(b, tau):
        cand = tau + lax.shift_left(jnp.int32(1), 31 - b)
        cnt = count(lambda j, kk: kk >= cand)
        return jnp.where(cnt >= topk, cand, tau)

    tau = lax.fori_loop(0, 32, tau_body, jnp.full((1, tq), INT_MIN, I32))
    need = jnp.where(tau == INT_MIN, float(2 ** 20), topk - count(lambda j, kk: kk > tau))
    tied = jnp.max(count(lambda j, kk: kk == tau) - need) > 0.0

    def pos_body(b, p):
        cand = p + lax.shift_left(jnp.int32(1), 12 - b)

        def pred(j, kk):
            kpos = j * tk + _iota((tk, tq), 0)
            return (kk == tau) & (kpos < cand)
        cnt = count(pred)
        return jnp.where(cnt < need, cand, p)

    p_cut = lax.fori_loop(0, jnp.where(tied, 13, 0), pos_body,
                          jnp.full((1, tq), jnp.where(tied, 0, 2 ** 20), I32))

    m_sc[...] = jnp.full(m_sc.shape, NEG, F32)
    acc_sc[...] = jnp.zeros(acc_sc.shape, F32)
    ones = jnp.ones((ONES_ROWS, tk), BF16)
    n_rep = D_HEADS // D_KV_HEADS
    q_stack = [jnp.concatenate([dq_ref[0, :, hd * LANES:(hd + 1) * LANES]
                                for hd in range(g * n_rep, (g + 1) * n_rep)], axis=0) for g in range(D_KV_HEADS)]

    def logits(j):
        kt = k2_ref[0, pl.ds(pl.multiple_of(j * tk, tk), tk), :]
        return [_dot_nt(kt, q_stack[g]) for g in range(D_KV_HEADS)]

    s_sc[0], s_sc[1] = logits(0)

    def attn_body(j, c):
        s_next = logits(jnp.minimum(j + 1, n_tiles - 1))
        kk = keys_sc[j]
        kpos = j * tk + _iota((tk, tq), 0)
        sel = ((kk > tau) | ((kk == tau) & (kpos <= p_cut))) & (kk != INT_MIN)
        bias = jnp.where(sel, 0.0, MASKED)
        bias = jnp.concatenate([bias] * n_rep, axis=1)
        for g in range(D_KV_HEADS):
            s = s_sc[g] + bias
            m_old = m_sc[g]
            m_new = jnp.maximum(m_old, jnp.max(s, axis=0, keepdims=True))
            p = jnp.exp2(s - m_new).astype(BF16)
            vt = jnp.concatenate([vt_ref[0, j, g * 64:(g + 1) * 64, :], ones], axis=0)
            acc_sc[g] = jnp.exp2(m_old - m_new) * acc_sc[g] + _dot(vt, p)
            m_sc[g] = m_new
        s_sc[0], s_sc[1] = s_next
        return c

    lax.fori_loop(0, n_tiles, attn_body, 0)
    for g in range(D_KV_HEADS):
        o_g = acc_sc[g, 0:64] / acc_sc[g, 64:65]
        for pr in range(n_rep // 2):
            o_t = jnp.concatenate([o_g[:, (2 * pr) * tq:(2 * pr + 1) * tq],
                                   o_g[:, (2 * pr + 1) * tq:(2 * pr + 2) * tq]], axis=0)
            col = (g * n_rep // 2 + pr) * LANES
            o_ref[0, :, col:col + LANES] = o_t.T.astype(BF16)


def _dsa_attn(dq, iq, iw, ik2, k2, vt, tq, tk, q_start, s_valid):
    bsz, t_q, _ = dq.shape
    s = k2.shape[1]
    topk = min(IDX_TOPK_MAX, s_valid // 4)
    kern = functools.partial(_dsa_attn_kernel, tq=tq, tk=tk, q_start=q_start, s_valid=s_valid, topk=float(topk))
    return pl.pallas_call(
        kern,
        grid=(bsz, t_q // tq),
        in_specs=[pl.BlockSpec((1, tq, 8 * LANES), lambda b, i: (b, i, 0)),
                  pl.BlockSpec((1, tq, 256), lambda b, i: (b, i, 0)),
                  pl.BlockSpec((1, tq, LANES), lambda b, i: (b, i, 0)),
                  pl.BlockSpec((1, 2, s, LANES), lambda b, i: (b, 0, 0, 0)),
                  pl.BlockSpec((1, s, LANES), lambda b, i: (b, 0, 0)),
                  pl.BlockSpec((1, s // tk, LANES, tk), lambda b, i: (b, 0, 0, 0))],
        out_specs=pl.BlockSpec((1, tq, 512), lambda b, i: (b, i, 0)),
        out_shape=jax.ShapeDtypeStruct((bsz, t_q, 512), BF16),
        scratch_shapes=[pltpu.VMEM((s // tk, tk, tq), I32),
                        pltpu.VMEM((D_KV_HEADS, 1, D_HEADS // D_KV_HEADS * tq), F32),
                        pltpu.VMEM((D_KV_HEADS, 64 + ONES_ROWS, D_HEADS // D_KV_HEADS * tq), F32),
                        pltpu.VMEM((D_KV_HEADS, tk, D_HEADS // D_KV_HEADS * tq), F32)],
        compiler_params=_params(("parallel", "arbitrary")),
        name="dsa_attn",
    )(dq, iq, iw, ik2, k2, vt)


def _prep_ab(w_in, w_gk):
    w = jnp.pad(w_in, ((0, 0), (0, AB_COLS - w_in.shape[1]))).astype(BF16)
    wgk = jnp.pad(w_gk, ((0, LANES - B_GATE_RANK), (0, 0))).astype(BF16)
    return w, wgk


def _prep_cd(w_in, f_bias):
    o = np.cumsum([0, 512, 512, 512, C_HEADS, 512, 128, 128, 256, IDX_DIM, IDX_HEADS])
    c_q, c_k, c_v, c_f, d_q, d_k, d_v, d_iq, d_ik, d_iw = (w_in[:, o[i]:o[i + 1]] for i in range(10))
    zeros = lambda n: jnp.zeros((D_MODEL, n), w_in.dtype)
    dq_cols = []
    for h in range(D_HEADS):
        g = h // (D_HEADS // D_KV_HEADS)
        wh = d_q[:, h * 64:(h + 1) * 64]
        dq_cols += [wh, zeros(64)] if g == 0 else [zeros(64), wh]
    w = jnp.concatenate([c_q, c_k, c_v] + dq_cols + [d_k, d_v, d_iq, d_ik, c_f, zeros(LANES - IDX_DIM - C_HEADS),
                                                     d_iw, zeros(LANES - IDX_HEADS)], axis=1).astype(BF16)
    fb = jnp.pad(f_bias.astype(F32), (IDX_DIM, LANES - IDX_DIM - C_HEADS)).reshape(1, LANES)
    return w, fb


def _pad_rows(z, s):
    return jnp.pad(z, ((0, 0), (0, s - z.shape[1]), (0, 0)))


def _pad_lanes(z, n=LANES):
    return jnp.pad(z, ((0, 0), (0, 0), (0, n - z.shape[2])))


def _trunk(x, s_a, s_b, cache, wts):
    bsz, t_len, _ = x.shape
    row = lambda z: z.reshape(1, -1).astype(F32)

    t_pad = -(-t_len // REC_CHUNK) * REC_CHUNK
    x1, sa_new, sb_new = _ab_layer(
        _pad_rows(x, t_pad), s_a, s_b.reshape(bsz, 2, 128, 128), row(wts['norm_mix'][0]), wts['ab_w_in'],
        wts['gla_w_gk'], row(wts['gla_b_gk']), row(wts['lb']), row(wts['hgrn_gnorm']), row(wts['gla_gnorm']),
        wts['ab_w_out'], t_len)
    n = bsz * t_len
    x1 = x1[:, :t_len].reshape(n, D_MODEL)
    x2 = _ffn(x1, row(wts['norm_ffn'][0]), wts['ffn_w_in'][0], wts['ffn_w_out'][0])

    fq, ck, cv, dq, dk, dv, iq, misc, iw = _cd_proj(x2, row(wts['norm_mix'][1]), wts['cd_w_in'], wts['fox_f_bias'])
    per_b = lambda z: z.reshape(bsz, t_len, z.shape[-1])
    fq, ck, cv, dq, dk, dv, iq, misc, iw = map(per_b, (fq, ck, cv, dq, dk, dv, iq, misc, iw))
    logf = misc[:, :, IDX_DIM:IDX_DIM + C_HEADS]
    if cache is None:
        q_start = 0
        k_all, v_all, lf_all, dk_all, dv_all, ik_all = ck, cv, _pad_lanes(logf), dk, dv, misc
    else:
        c_k, c_v, c_lf, c_dk, c_dv, c_ik = cache
        q_start = c_k.shape[1]
        cat = lambda c, r: jnp.concatenate([c.reshape(bsz, q_start, -1).astype(F32), r], axis=1)
        k_all, v_all, dk_all, dv_all = cat(c_k, ck), cat(c_v, cv), cat(c_dk, dk), cat(c_dv, dv)
        lf_all = _pad_lanes(cat(c_lf, logf))
        ik_all = cat(_pad_lanes(c_ik), misc)
    s_valid = q_start + t_len
    tk = 512
    s_pad = -(-s_valid // tk) * tk
    k_all, v_all, lf_all, dk_all, dv_all, ik_all = (_pad_rows(z, s_pad) for z in
                                                    (k_all, v_all, lf_all, dk_all, dv_all, ik_all))
    tq_pad = -(-t_len // LANES) * LANES
    fox_tq = 512 if tq_pad % 512 == 0 else LANES
    k_aug, v_t, cum = _fox_pack(k_all, v_all, lf_all, tm=tk)
    q_aug = _q_pack(_pad_rows(fq, tq_pad), _pad_rows(cum[:, q_start:q_start + t_len], tq_pad),
                    tm=512 if tq_pad % 512 == 0 else LANES)
    o_c = _fox_attn(q_aug, k_aug, v_t, tq=fox_tq, tk=tk, q_start=q_start)[:, :t_len]
    k2, dv_t, ik2 = _dsa_pack(dk_all, dv_all, ik_all, tm=tk)
    o_d = _dsa_attn(_pad_rows(dq, tq_pad), _pad_rows(iq, tq_pad), _pad_rows(iw, tq_pad), ik2, k2, dv_t,
                    tq=256 if tq_pad % 256 == 0 else LANES, tk=tk, q_start=q_start, s_valid=s_valid)[:, :t_len]

    y = _ffn(x2, row(wts['norm_ffn'][1]), wts['ffn_w_in'][1], wts['ffn_w_out'][1],
             attn=(o_c.reshape(n, 512), o_d.reshape(n, 512), wts['cd_w_out']), g_final=row(wts['norm_final']))
    rows = (ck.reshape(1, bsz, t_len, C_HEADS, C_HD), cv.reshape(1, bsz, t_len, C_HEADS, C_HD),
            logf[None], dk.reshape(1, bsz, t_len, D_KV_HEADS, D_HD), dv.reshape(1, bsz, t_len, D_KV_HEADS, D_HD),
            misc[None, :, :, :IDX_DIM])
    return (y.reshape(bsz, t_len, D_MODEL), sa_new[None], sb_new.reshape(1, bsz, B_HEADS, B_DK, B_DV)) + rows


def kernel(x_prompt, x_sample, state_hgrn, state_gla, cache_fox_k, cache_fox_v, cache_fox_logf, cache_dsa_k, cache_dsa_v, cache_dsa_ik, norm_mix, norm_ffn, norm_final, ab_w_in, ab_w_out, hgrn_lb_logits, hgrn_gnorm, gla_w_gk, gla_b_gk, gla_gnorm, cd_w_in, cd_w_out, fox_f_bias, ffn_w_in, ffn_w_out):
    lbs = jnp.cumsum(jax.nn.softmax(hgrn_lb_logits.astype(F32), axis=0), axis=0)
    w_ab, w_gk = _prep_ab(ab_w_in[0], gla_w_gk[0])
    w_cd, fb = _prep_cd(cd_w_in[0], fox_f_bias[0])
    wts = dict(norm_mix=norm_mix, norm_ffn=norm_ffn, norm_final=norm_final, ab_w_in=w_ab,
               ab_w_out=ab_w_out[0].astype(BF16), lb=lbs[0], hgrn_gnorm=hgrn_gnorm[0], gla_w_gk=w_gk,
               gla_b_gk=gla_b_gk[0], gla_gnorm=gla_gnorm[0], cd_w_in=w_cd, cd_w_out=cd_w_out[0].astype(BF16),
               fox_f_bias=fb, ffn_w_in=ffn_w_in.astype(BF16), ffn_w_out=ffn_w_out.astype(BF16))
    bp = x_prompt.shape[0]
    p_out = _trunk(x_prompt, jnp.zeros((bp, A_HEADS, A_DK, A_DV), F32), jnp.zeros((bp, B_HEADS, B_DK, B_DV), F32),
                   None, wts)
    cache = (cache_fox_k[0], cache_fox_v[0], cache_fox_logf[0], cache_dsa_k[0], cache_dsa_v[0], cache_dsa_ik[0])
    s_out = _trunk(x_sample, state_hgrn[0], state_gla[0], cache, wts)
    return (p_out[0], s_out[0]) + tuple(p_out[1:]) + tuple(s_out[1:])
```

```python
import functools

import numpy as np
import jax
import jax.numpy as jnp
from jax import lax
from jax.experimental import pallas as pl
from jax.experimental.pallas import tpu as pltpu

F32 = jnp.float32
BF16 = jnp.bfloat16
I32 = jnp.int32

D_MODEL = 1024
CHUNK = 64
A_HEADS, A_DK, A_DV = 4, 128, 128
B_HEADS, B_DK, B_DV = 4, 64, 128
B_GATE_RANK = 16
B_GATE_NORM = 16.0
C_HEADS, C_HD = 8, 64
D_HEADS, D_KV_HEADS, D_HD = 8, 2, 64
IDX_HEADS, IDX_DIM = 4, 64
IDX_TOPK_MAX = 256
IDX_SCALE = (IDX_DIM ** -0.5) * (IDX_HEADS ** -0.5)
FFN_HIDDEN = ((8 * D_MODEL // 3 + 255) // 256) * 256

LANES = 128
RSUB = 16
REC_CHUNK = 128
VMEM_LIMIT = 56 * 1024 * 1024
NEG = -1e30
MASKED = -2e30
LOG2E = 1.4426950408889634
ONES_ROWS = 16
INT_MIN = -2 ** 31

AB_COLS = 4 * 512 + 256 + 256 + 512 + 512 + LANES
CD_COLS = 3 * 512 + 8 * LANES + 2 * LANES + 256 + LANES + LANES


def _dot(a, b):
    return jnp.dot(a, b, preferred_element_type=F32)


def _dot_nt(a, b):
    return lax.dot_general(a, b, (((1,), (1,)), ((), ())), preferred_element_type=F32)


def _rms(x, g, eps=1e-6):
    return x * lax.rsqrt(jnp.mean(x * x, axis=-1, keepdims=True) + eps) * g


def _silu(x):
    return x * jax.nn.sigmoid(x)


def _log_sigmoid(x):
    return jnp.minimum(x, 0.0) - jnp.log1p(jnp.exp(-jnp.abs(x)))


def _split3(x):
    hi = x.astype(BF16)
    r = x - hi.astype(F32)
    mid = r.astype(BF16)
    lo = (r - mid.astype(F32)).astype(BF16)
    return hi, mid, lo


def _tri_dot(tri, x):
    hi, mid, lo = _split3(x)
    return _dot(tri, hi) + _dot(tri, mid) + _dot(tri, lo)


def _iota(shape, dim):
    return lax.broadcasted_iota(I32, shape, dim)


def _const_spec(shape):
    zeros = (0,) * len(shape)
    return pl.BlockSpec(shape, lambda *_: zeros, pipeline_mode=pl.Buffered(1))


def _params(sem):
    return pltpu.CompilerParams(dimension_semantics=sem, vmem_limit_bytes=VMEM_LIMIT)


def _ab_kernel(x_ref, sa_ref, sb_ref, g_ref, win_ref, wgk_ref, bgk_ref, lb_ref, agn_ref, bgn_ref, wout_ref,
               xo_ref, sao_ref, sbo_ref, s_sc, h_sc, o_sc, *, t_valid, t_pad):
    C = REC_CHUNK
    tm = x_ref.shape[1]
    t = pl.program_id(1)

    @pl.when(t == 0)
    def _():
        s_sc[0:4] = sa_ref[0]
        s_sc[4:6] = sb_ref[0]

    h_sc[...] = _dot(_rms(x_ref[0], g_ref[...]).astype(BF16), win_ref[...])

    def chunk_body(c, carry):
        r0 = pl.multiple_of(c * C, C)
        _ab_chunk(r0, t * tm + r0, h_sc, o_sc, s_sc, wgk_ref, bgk_ref, lb_ref, agn_ref, bgn_ref,
                  t_valid=t_valid, t_pad=t_pad)
        return carry

    lax.fori_loop(0, tm // C, chunk_body, 0)
    xo_ref[0] = x_ref[0] + _dot(o_sc[...], wout_ref[...])

    @pl.when(t == pl.num_programs(1) - 1)
    def _():
        sao_ref[0] = s_sc[0:4]
        sbo_ref[0] = s_sc[4:6]


def _ab_chunk(r0, row0, h_sc, o_sc, s_sc, wgk_ref, bgk_ref, lb_ref, agn_ref, bgn_ref, *, t_valid, t_pad):
    C = REC_CHUNK
    cols = lambda a, b: h_sc[pl.ds(r0, C), a:b]
    a_q, a_f, a_i, a_g = cols(0, 512), cols(512, 1024), cols(1024, 1536), cols(1536, 2048)
    b_q, b_k, b_v, b_g = cols(2048, 2304), cols(2304, 2560), cols(2560, 3072), cols(3072, 3584)
    b_lr = cols(3584, 3712)

    lb = lb_ref[...]
    f = lb + (1.0 - lb) * jax.nn.sigmoid(a_f)
    gk = _dot(b_lr.astype(BF16), wgk_ref[...]) + bgk_ref[...]
    la = jnp.concatenate([jnp.log(f), _log_sigmoid(gk) * (1.0 / B_GATE_NORM)], axis=1)
    q = jnp.concatenate([_silu(a_q), b_q * (B_DK ** -0.5)], axis=1)
    k = jnp.concatenate([1.0 - f, b_k], axis=1)
    v_a, v_b = a_i, b_v
    if t_valid < t_pad:
        ok = (row0 + _iota((C, 1), 0)) < t_valid
        la = jnp.where(ok, la, 0.0)
        k = jnp.where(ok, k, 0.0)
        v_a = jnp.where(ok, v_a, 0.0)
        v_b = jnp.where(ok, v_b, 0.0)

    row = _iota((C, C), 0)
    col = _iota((C, C), 1)
    causal = col <= row
    tri = jnp.where(causal, 1.0, 0.0).astype(BF16)
    tri_in = jnp.where(causal & ((row >> 4) == (col >> 4)), 1.0, 0.0).astype(BF16)
    bc = _tri_dot(tri, la)
    b_in = _tri_dot(tri_in, la)
    lane = _iota((1, LANES), 1)
    lo_half = lane < 64
    srow = _iota((LANES, 1), 0) < 64

    o_heads = [None] * 8
    for u in range(6):
        sl = slice(u * LANES, (u + 1) * LANES)
        qu, ku, bcu = q[:, sl], k[:, sl], bc[:, sl]
        s_old = s_sc[u]
        bend = bcu[C - 1:C, :]
        qt = qu * jnp.exp(b_in[:, sl])
        qdec = qu * jnp.exp(bcu)
        if u < 4:
            heads = [(u, None, v_a[:, sl])]
        else:
            ha = 4 + 2 * (u - 4)
            heads = [(ha, lo_half, v_b[:, (ha - 4) * LANES:(ha - 3) * LANES]),
                     (ha + 1, jnp.logical_not(lo_half), v_b[:, (ha - 3) * LANES:(ha - 2) * LANES])]
        a_rows = [[] for _ in heads]
        for i in range(C // RSUB):
            n = RSUB * (i + 1)
            if i == 0:
                kt = ku[0:n] * jnp.exp(-bcu[0:n])
            else:
                kt = ku[0:n] * jnp.exp(bcu[RSUB * i - 1:RSUB * i, :] - bcu[0:n])
            if n < C:
                kt = jnp.concatenate([kt, jnp.zeros((C - n, LANES), F32)], axis=0)
            ktb = kt.astype(BF16)
            qi = qt[RSUB * i:RSUB * (i + 1)]
            for hi_, (_, msk, _) in enumerate(heads):
                qim = qi if msk is None else jnp.where(msk, qi, 0.0)
                a_rows[hi_].append(_dot_nt(qim.astype(BF16), ktb))
        sb16 = s_old.astype(BF16)
        for hi_, (hd, msk, vh) in enumerate(heads):
            att = jnp.where(causal, jnp.concatenate(a_rows[hi_], axis=0), 0.0)
            qd = qdec if msk is None else jnp.where(msk, qdec, 0.0)
            o_heads[hd] = _dot(qd.astype(BF16), sb16) + _dot(att.astype(BF16), vh.astype(BF16))
        kht = (ku * jnp.exp(bend - bcu)).T
        dcol = jnp.broadcast_to(jnp.exp(bend), (LANES, LANES)).T
        if u < 4:
            upd = _dot(kht.astype(BF16), heads[0][2].astype(BF16))
        else:
            lhs = jnp.concatenate([jnp.where(srow, kht, 0.0), jnp.where(srow, 0.0, kht)], axis=1)
            rhs = jnp.concatenate([heads[0][2], heads[1][2]], axis=0)
            upd = _dot(lhs.astype(BF16), rhs.astype(BF16))
        s_sc[u] = dcol * s_old + upd

    outs = []
    for hd in range(8):
        if hd < 4:
            gn, gate = agn_ref[...], a_g[:, hd * LANES:(hd + 1) * LANES]
        else:
            gn, gate = bgn_ref[...], b_g[:, (hd - 4) * LANES:(hd - 3) * LANES]
        outs.append(_rms(o_heads[hd], gn) * _silu(gate))
    o_sc[pl.ds(r0, C), :] = jnp.concatenate(outs, axis=1).astype(BF16)


def _ab_layer(x, s_a, s_b, g, w_in, w_gk, b_gk, lb, a_gn, b_gn, w_out, t_valid):
    bsz, t_pad, _ = x.shape
    tm = 512 if t_pad % 512 == 0 else REC_CHUNK
    kern = functools.partial(_ab_kernel, t_valid=t_valid, t_pad=t_pad)
    return pl.pallas_call(
        kern,
        grid=(bsz, t_pad // tm),
        in_specs=[
            pl.BlockSpec((1, tm, D_MODEL), lambda b, t: (b, t, 0)),
            pl.BlockSpec((1, 4, 128, 128), lambda b, t: (b, 0, 0, 0)),
            pl.BlockSpec((1, 2, 128, 128), lambda b, t: (b, 0, 0, 0)),
            _const_spec((1, D_MODEL)),
            _const_spec((D_MODEL, AB_COLS)),
            _const_spec((LANES, 256)),
            _const_spec((1, 256)),
            _const_spec((1, 512)),
            _const_spec((1, 128)),
            _const_spec((1, 128)),
            _const_spec((D_MODEL, D_MODEL)),
        ],
        out_specs=[
            pl.BlockSpec((1, tm, D_MODEL), lambda b, t: (b, t, 0)),
            pl.BlockSpec((1, 4, 128, 128), lambda b, t: (b, 0, 0, 0)),
            pl.BlockSpec((1, 2, 128, 128), lambda b, t: (b, 0, 0, 0)),
        ],
        out_shape=[
            jax.ShapeDtypeStruct((bsz, t_pad, D_MODEL), F32),
            jax.ShapeDtypeStruct((bsz, 4, 128, 128), F32),
            jax.ShapeDtypeStruct((bsz, 2, 128, 128), F32),
        ],
        scratch_shapes=[pltpu.VMEM((6, 128, 128), F32), pltpu.VMEM((tm, AB_COLS), F32),
                        pltpu.VMEM((tm, D_MODEL), BF16)],
        compiler_params=_params(("parallel", "arbitrary")),
        name="ab_layer",
    )(x, s_a, s_b, g, w_in, w_gk, b_gk, lb, a_gn, b_gn, w_out)


FFN_TILE = FFN_HIDDEN // 2


def _ffn_kernel(*refs, has_attn, has_final):
    refs = list(refs)
    x_ref = refs.pop(0)
    x = x_ref[...]
    if has_attn:
        oc_ref, od_ref, wo_ref = refs.pop(0), refs.pop(0), refs.pop(0)
        x = x + _dot(jnp.concatenate([oc_ref[...], od_ref[...]], axis=1), wo_ref[...])
    g_ref, win_ref, wout_ref = refs.pop(0), refs.pop(0), refs.pop(0)
    gf_ref = refs.pop(0) if has_final else None
    out_ref = refs.pop(0)
    xn = _rms(x, g_ref[...]).astype(BF16)
    acc = x
    for j in range(FFN_HIDDEN // FFN_TILE):
        gate = _dot(xn, win_ref[:, j * FFN_TILE:(j + 1) * FFN_TILE])
        up = _dot(xn, win_ref[:, FFN_HIDDEN + j * FFN_TILE:FFN_HIDDEN + (j + 1) * FFN_TILE])
        act = (_silu(gate) * up).astype(BF16)
        acc = acc + _dot(act, wout_ref[j * FFN_TILE:(j + 1) * FFN_TILE, :])
    if has_final:
        acc = _rms(acc, gf_ref[...])
    out_ref[...] = acc


def _ffn(x, g, w_in, w_out, attn=None, g_final=None, tm=512):
    n = x.shape[0]
    tm = min(tm, n)
    row = lambda i: (i, 0)
    args = [x]
    specs = [pl.BlockSpec((tm, D_MODEL), row)]
    if attn is not None:
        oc, od, wo = attn
        args += [oc, od, wo]
        specs += [pl.BlockSpec((tm, 512), row), pl.BlockSpec((tm, 512), row), _const_spec((D_MODEL, D_MODEL))]
    args += [g, w_in, w_out]
    specs += [_const_spec((1, D_MODEL)), _const_spec((D_MODEL, 2 * FFN_HIDDEN)), _const_spec((FFN_HIDDEN, D_MODEL))]
    if g_final is not None:
        args.append(g_final)
        specs.append(_const_spec((1, D_MODEL)))
    kern = functools.partial(_ffn_kernel, has_attn=attn is not None, has_final=g_final is not None)
    return pl.pallas_call(
        kern,
        grid=(n // tm,),
        in_specs=specs,
        out_specs=pl.BlockSpec((tm, D_MODEL), row),
        out_shape=jax.ShapeDtypeStruct((n, D_MODEL), F32),
        compiler_params=_params(("parallel",)),
        name="ffn",
    )(*args)


def _cd_proj_kernel(x_ref, g_ref, w_ref, fb_ref, fq_ref, ck_ref, cv_ref, dq_ref, dk_ref, dv_ref, iq_ref,
                    misc_ref, iw_ref):
    xn = _rms(x_ref[...], g_ref[...]).astype(BF16)
    h = _dot(xn, w_ref[...])
    fq_ref[...] = (h[:, 0:512] * (C_HD ** -0.5 * LOG2E)).astype(BF16)
    ck_ref[...] = h[:, 512:1024]
    cv_ref[...] = h[:, 1024:1536]
    dq_ref[...] = (h[:, 1536:2560] * (D_HD ** -0.5 * LOG2E)).astype(BF16)
    dk_ref[...] = h[:, 2560:2688]
    dv_ref[...] = h[:, 2688:2816]
    iq_ref[...] = h[:, 2816:3072].astype(BF16)
    misc = h[:, 3072:3200]
    lane = _iota((1, LANES), 1)
    is_f = (lane >= IDX_DIM) & (lane < IDX_DIM + C_HEADS)
    misc_ref[...] = jnp.where(is_f, _log_sigmoid(misc + fb_ref[...]), misc)
    iw_ref[...] = h[:, 3200:3328]


def _cd_proj(x, g, w, fb, tm=512):
    n = x.shape[0]
    tm = min(tm, n)
    row = lambda i: (i, 0)
    widths = [(512, BF16), (512, F32), (512, F32), (1024, BF16), (128, F32), (128, F32), (256, BF16),
              (128, F32), (128, F32)]
    return pl.pallas_call(
        _cd_proj_kernel,
        grid=(n // tm,),
        in_specs=[pl.BlockSpec((tm, D_MODEL), row), _const_spec((1, D_MODEL)), _const_spec((D_MODEL, CD_COLS)),
                  _const_spec((1, LANES))],
        out_specs=[pl.BlockSpec((tm, w_), row) for w_, _ in widths],
        out_shape=[jax.ShapeDtypeStruct((n, w_), dt) for w_, dt in widths],
        compiler_params=_params(("parallel",)),
        name="cd_proj",
    )(x, g, w, fb)


def _aug_consts(is_query):
    p = np.zeros((3, LANES, 8 * LANES), np.float32)
    ones = np.zeros((1, 8 * LANES), np.float32)
    for h in range(8):
        off = h * LANES + (64 if h % 2 == 0 else 0)
        for c in range(3):
            if is_query:
                p[c, h, off + c] = 1.0
                ones[0, off + 3 + c] = 1.0
            else:
                p[c, h, off + 3 + c] = -1.0
                ones[0, off + c] = 1.0
    return jnp.asarray(p, BF16), jnp.asarray(ones, F32)


def _aug_lanes(cum, p_ref, ones_ref):
    hi, mid, lo = _split3(cum * LOG2E)
    return _dot(hi, p_ref[0]) + _dot(mid, p_ref[1]) + _dot(lo, p_ref[2]) + ones_ref[...]


def _fox_pack_kernel(k_ref, v_ref, lf_ref, p_ref, ones_ref, ka_ref, vt_ref, cum_ref, carry):
    tm = k_ref.shape[1]

    @pl.when(pl.program_id(1) == 0)
    def _():
        carry[...] = jnp.zeros_like(carry)

    tri = jnp.where(_iota((tm, tm), 1) <= _iota((tm, tm), 0), 1.0, 0.0).astype(BF16)
    cum = _tri_dot(tri, lf_ref[0]) + carry[...]
    carry[...] = cum[tm - 1:tm, :]
    cum_ref[0] = cum
    aug = _aug_lanes(cum, p_ref, ones_ref)
    lane = _iota((1, LANES), 1)
    kk, vv = k_ref[0], v_ref[0]
    for h in range(8):
        pr = slice((h // 2) * LANES, (h // 2 + 1) * LANES)
        own = (lane < 64) if h % 2 == 0 else (lane >= 64)
        ka_ref[0, h] = jnp.where(own, kk[:, pr], aug[:, h * LANES:(h + 1) * LANES]).astype(BF16)
    for pr in range(4):
        vt_ref[0, pr, 0] = vv[:, pr * LANES:(pr + 1) * LANES].T.astype(BF16)


def _fox_pack(k, v, lf, tm):
    bsz, s, _ = k.shape
    p, ones = _aug_consts(False)
    return pl.pallas_call(
        _fox_pack_kernel,
        grid=(bsz, s // tm),
        in_specs=[pl.BlockSpec((1, tm, 512), lambda b, t: (b, t, 0)),
                  pl.BlockSpec((1, tm, 512), lambda b, t: (b, t, 0)),
                  pl.BlockSpec((1, tm, LANES), lambda b, t: (b, t, 0)),
                  _const_spec((3, LANES, 8 * LANES)), _const_spec((1, 8 * LANES))],
        out_specs=[pl.BlockSpec((1, 8, tm, LANES), lambda b, t: (b, 0, t, 0)),
                   pl.BlockSpec((1, 4, 1, LANES, tm), lambda b, t: (b, 0, t, 0, 0)),
                   pl.BlockSpec((1, tm, LANES), lambda b, t: (b, t, 0))],
        out_shape=[jax.ShapeDtypeStruct((bsz, 8, s, LANES), BF16),
                   jax.ShapeDtypeStruct((bsz, 4, s // tm, LANES, tm), BF16),
                   jax.ShapeDtypeStruct((bsz, s, LANES), F32)],
        scratch_shapes=[pltpu.VMEM((1, LANES), F32)],
        compiler_params=_params(("parallel", "arbitrary")),
        name="fox_pack",
    )(k, v, lf, p, ones)


def _q_pack_kernel(q_ref, cum_ref, p_ref, ones_ref, qa_ref):
    aug = _aug_lanes(cum_ref[0], p_ref, ones_ref)
    lane = _iota((1, LANES), 1)
    qq = q_ref[0].astype(F32)
    for h in range(8):
        pr = slice((h // 2) * LANES, (h // 2 + 1) * LANES)
        own = (lane < 64) if h % 2 == 0 else (lane >= 64)
        qa_ref[0, h] = jnp.where(own, qq[:, pr], aug[:, h * LANES:(h + 1) * LANES]).astype(BF16)


def _q_pack(q, cum_q, tm):
    bsz, tq, _ = q.shape
    p, ones = _aug_consts(True)
    return pl.pallas_call(
        _q_pack_kernel,
        grid=(bsz, tq // tm),
        in_specs=[pl.BlockSpec((1, tm, 512), lambda b, t: (b, t, 0)),
                  pl.BlockSpec((1, tm, LANES), lambda b, t: (b, t, 0)),
                  _const_spec((3, LANES, 8 * LANES)), _const_spec((1, 8 * LANES))],
        out_specs=pl.BlockSpec((1, 8, tm, LANES), lambda b, t: (b, 0, t, 0)),
        out_shape=jax.ShapeDtypeStruct((bsz, 8, tq, LANES), BF16),
        compiler_params=_params(("parallel", "parallel")),
        name="q_pack",
    )(q, cum_q, p, ones)


def _fox_attn_kernel(q_ref, k_ref, vt_ref, o_ref, s_sc, *, tq, tk, q_start):
    i = pl.program_id(2)
    q_lo = q_start + i * tq
    n_full = (q_lo + 1) // tk
    n_tiles = (q_lo + tq + tk - 1) // tk
    q_pos = q_lo + _iota((1, tq), 1)
    qs = (q_ref[0, 0], q_ref[0, 1])

    def logits(j):
        off = pl.multiple_of(j * tk, tk)
        return [_dot_nt(k_ref[0, hh, pl.ds(off, tk), :], qs[hh]) for hh in range(2)]

    s_sc[0], s_sc[1] = logits(0)

    def body(j, carry, masked):
        off = pl.multiple_of(j * tk, tk)
        s_next = logits(jnp.minimum(j + 1, n_tiles - 1))
        ss = [s_sc[0], s_sc[1]]
        if masked:
            ok = (off + _iota((tk, 1), 0)) <= q_pos
            ss = [jnp.where(ok, s, MASKED) for s in ss]
        out = []
        for hh in range(2):
            m, acc = carry[hh]
            m_new = jnp.maximum(m, jnp.max(ss[hh], axis=0, keepdims=True))
            p = jnp.exp2(ss[hh] - m_new).astype(BF16)
            vt = jnp.concatenate([vt_ref[0, 0, j, hh * 64:(hh + 1) * 64, :], ones], axis=0)
            acc = jnp.exp2(m - m_new) * acc + _dot(vt, p)
            out.append((m_new, acc))
        s_sc[0], s_sc[1] = s_next
        return tuple(out)

    ones = jnp.ones((ONES_ROWS, tk), BF16)
    init = (jnp.full((1, tq), NEG, F32), jnp.zeros((64 + ONES_ROWS, tq), F32))
    carry = lax.fori_loop(0, n_full, functools.partial(body, masked=False), (init, init))
    carry = lax.fori_loop(n_full, n_tiles, functools.partial(body, masked=True), carry)
    o_t = jnp.concatenate([acc[0:64] / acc[64:65] for _, acc in carry], axis=0)
    o_ref[0] = o_t.T.astype(BF16)


def _fox_attn(q_aug, k_aug, vt, tq, tk, q_start):
    bsz, _, t_q, _ = q_aug.shape
    s = k_aug.shape[2]
    kern = functools.partial(_fox_attn_kernel, tq=tq, tk=tk, q_start=q_start)
    return pl.pallas_call(
        kern,
        grid=(bsz, 4, t_q // tq),
        in_specs=[pl.BlockSpec((1, 2, tq, LANES), lambda b, p, i: (b, p, i, 0)),
                  pl.BlockSpec((1, 2, s, LANES), lambda b, p, i: (b, p, 0, 0)),
                  pl.BlockSpec((1, 1, s // tk, LANES, tk), lambda b, p, i: (b, p, 0, 0, 0))],
        out_specs=pl.BlockSpec((1, tq, LANES), lambda b, p, i: (b, i, p)),
        out_shape=jax.ShapeDtypeStruct((bsz, t_q, 512), BF16),
        scratch_shapes=[pltpu.VMEM((2, tk, tq), F32)],
        compiler_params=_params(("parallel", "parallel", "arbitrary")),
        name="fox_attn",
    )(q_aug, k_aug, vt)


def _dsa_pack_kernel(dk_ref, dv_ref, ik_ref, k2_ref, vt_ref, ik2_ref):
    lo = _iota((1, LANES), 1) < 64
    k2_ref[0] = dk_ref[0].astype(BF16)
    vt_ref[0, 0] = dv_ref[0].T.astype(BF16)
    ik = jnp.where(lo, ik_ref[0], 0.0)
    ik2_ref[0, 0] = ik.astype(BF16)
    ik2_ref[0, 1] = pltpu.roll(ik, 64, 1).astype(BF16)


def _dsa_pack(dk, dv, ik, tm):
    bsz, s, _ = dk.shape
    tok = pl.BlockSpec((1, tm, LANES), lambda b, t: (b, t, 0))
    return pl.pallas_call(
        _dsa_pack_kernel,
        grid=(bsz, s // tm),
        in_specs=[tok, tok, tok],
        out_specs=[tok,
                   pl.BlockSpec((1, 1, LANES, tm), lambda b, t: (b, t, 0, 0)),
                   pl.BlockSpec((1, 2, tm, LANES), lambda b, t: (b, 0, t, 0))],
        out_shape=[jax.ShapeDtypeStruct((bsz, s, LANES), BF16),
                   jax.ShapeDtypeStruct((bsz, s // tm, LANES, tm), BF16),
                   jax.ShapeDtypeStruct((bsz, 2, s, LANES), BF16)],
        compiler_params=_params(("parallel", "parallel")),
        name="dsa_pack",
    )(dk, dv, ik)


def _dsa_attn_kernel(dq_ref, iq_ref, iw_ref, ik2_ref, k2_ref, vt_ref, o_ref, keys_sc, hi_sc, m_sc, mt_sc, acc_sc, s_sc,
                     *, tq, tk, q_start, s_valid, topk):
    i = pl.program_id(1)
    q_lo = q_start + i * tq
    adm_row = jnp.minimum(((q_lo + _iota((1, tq), 1)) // CHUNK + 1) * CHUNK, s_valid)
    adm_end = jnp.minimum(((q_lo + tq - 1) // CHUNK + 1) * CHUNK, s_valid)
    n_tiles = (adm_end + tk - 1) // tk
    iw_t = iw_ref[0].T
    iq = iq_ref[0]

    def score_body(j, c):
        off = pl.multiple_of(j * tk, tk)
        sc = jnp.zeros((tk, tq), F32)
        for hd in range(IDX_HEADS):
            qp = iq[:, (hd // 2) * LANES:(hd // 2 + 1) * LANES]
            r = jnp.maximum(_dot_nt(ik2_ref[0, hd % 2, pl.ds(off, tk), :], qp), 0.0)
            sc = sc + r * iw_t[hd:hd + 1, :]
        sc = sc * IDX_SCALE
        adm = (off + _iota((tk, tq), 0)) < adm_row
        bits = lax.bitcast_convert_type(sc, I32)
        bits = jnp.where(bits == INT_MIN, 0, bits)
        key = bits ^ ((bits >> 31) & 0x7FFFFFFF)
        keys_sc[j] = jnp.where(adm, key, INT_MIN)
        hi = lax.bitcast_convert_type(bits & -65536, F32)
        hi_sc[j] = jnp.where(adm, hi, -jnp.inf).astype(BF16)
        return c

    lax.fori_loop(0, n_tiles, score_body, 0)

    def pairs(one, init):
        part = lax.fori_loop(0, n_tiles // 2, lambda jj, c: one(2 * jj + 1, one(2 * jj, c)), init)
        return lax.fori_loop(2 * (n_tiles // 2), n_tiles, one, part)

    one16 = jnp.ones((tk, tq), BF16)
    zero16 = jnp.zeros((tk, tq), BF16)

    def hi_body(b, pre):
        cand = pre + lax.shift_left(jnp.int32(1), 15 - b)
        cbits = (cand ^ ((cand >> 15) & 0x7FFF)) & 0xFFFF
        subnormal = ((cbits & 0x7F80) == 0) & ((cbits & 0x007F) != 0)
        cbits = jnp.where(subnormal, jnp.where((cbits & 0x8000) != 0, 0x0000, 0x0080), cbits)
        cval = jnp.broadcast_to(lax.bitcast_convert_type(cbits << 16, F32).astype(BF16), (tk, tq))

        def one(j, c):
            hit = jnp.where(hi_sc[j] >= cval, one16, zero16).reshape(tk // 64, 4, 16, tq)
            for r in range(tk // 64):
                c = c + hit[r]
            return c
        part = pairs(one, jnp.zeros((4, 16, tq), BF16)).astype(F32)
        cnt = jnp.sum(jnp.sum(part, axis=0), axis=0, keepdims=True)
        return jnp.where(cnt >= topk, cand, pre)

    pre = lax.fori_loop(0, 16, hi_body, jnp.full((1, tq), -2 ** 15, I32))

    def count(pred):
        def one(j, c):
            hit = jnp.where(pred(j, keys_sc[j]), 1.0, 0.0)
            return c + jnp.sum(hit.reshape(tk // 32, 4, 8, tq), axis=0)

        part = pairs(one, jnp.zeros((4, 8, tq), F32))
        return jnp.sum(jnp.sum(part, axis=0), axis=0, keepdims=True)

    def tau_body(b, tau):
        cand = tau + lax.shift_left(jnp.int32(1), 31 - b)
        cnt = count(lambda j, kk: kk >= cand)
        return jnp.where(cnt >= topk, cand, tau)

    tau = lax.fori_loop(16, 32, tau_body, pre << 16)
    tau = jnp.where(adm_row < int(topk), INT_MIN, tau)
    need = jnp.where(tau == INT_MIN, float(2 ** 20), topk - count(lambda j, kk: kk > tau))
    tied = jnp.max(count(lambda j, kk: kk == tau) - need) > 0.0

    def pos_body(b, p):
        cand = p + lax.shift_left(jnp.int32(1), 12 - b)

        def pred(j, kk):
            kpos = j * tk + _iota((tk, tq), 0)
            return (kk == tau) & (kpos < cand)
        cnt = count(pred)
        return jnp.where(cnt < need, cand, p)

    p_cut = lax.fori_loop(0, jnp.where(tied, 13, 0), pos_body,
                          jnp.full((1, tq), jnp.where(tied, 0, 2 ** 20), I32))

    m_sc[...] = jnp.full(m_sc.shape, NEG, F32)
    acc_sc[...] = jnp.zeros(acc_sc.shape, F32)
    ones = jnp.ones((ONES_ROWS, tk), BF16)
    n_rep = D_HEADS // D_KV_HEADS
    q_stack = [jnp.concatenate([dq_ref[0, :, hd * LANES:(hd + 1) * LANES]
                                for hd in range(g * n_rep, (g + 1) * n_rep)], axis=0) for g in range(D_KV_HEADS)]

    def logits(j):
        kk = keys_sc[j]
        kpos = j * tk + _iota((tk, tq), 0)
        sel = ((kk > tau) | ((kk == tau) & (kpos <= p_cut))) & (kk != INT_MIN)
        bias = jnp.where(sel, 0.0, MASKED)
        bias = jnp.concatenate([bias] * n_rep, axis=1)
        kt = k2_ref[0, pl.ds(pl.multiple_of(j * tk, tk), tk), :]
        ss = [_dot_nt(kt, q_stack[g]) + bias for g in range(D_KV_HEADS)]
        return ss, [jnp.max(s, axis=0, keepdims=True) for s in ss]

    def stage(ss, mts):
        for g in range(D_KV_HEADS):
            s_sc[g] = ss[g]
            mt_sc[g] = mts[g]

    stage(*logits(0))

    def attn_body(j, c):
        nxt = logits(jnp.minimum(j + 1, n_tiles - 1))
        for g in range(D_KV_HEADS):
            m_old = m_sc[g]
            m_new = jnp.maximum(m_old, mt_sc[g])
            p = jnp.exp2(s_sc[g] - m_new).astype(BF16)
            vt = jnp.concatenate([vt_ref[0, j, g * 64:(g + 1) * 64, :], ones], axis=0)
            acc_sc[g] = jnp.exp2(m_old - m_new) * acc_sc[g] + _dot(vt, p)
            m_sc[g] = m_new
        stage(*nxt)
        return c

    lax.fori_loop(0, n_tiles, attn_body, 0)
    for g in range(D_KV_HEADS):
        o_g = acc_sc[g, 0:64] / acc_sc[g, 64:65]
        for pr in range(n_rep // 2):
            o_t = jnp.concatenate([o_g[:, (2 * pr) * tq:(2 * pr + 1) * tq],
                                   o_g[:, (2 * pr + 1) * tq:(2 * pr + 2) * tq]], axis=0)
            col = (g * n_rep // 2 + pr) * LANES
            o_ref[0, :, col:col + LANES] = o_t.T.astype(BF16)


def _dsa_attn(dq, iq, iw, ik2, k2, vt, tq, tk, q_start, s_valid):
    bsz, t_q, _ = dq.shape
    s = k2.shape[1]
    topk = min(IDX_TOPK_MAX, s_valid // 4)
    kern = functools.partial(_dsa_attn_kernel, tq=tq, tk=tk, q_start=q_start, s_valid=s_valid, topk=float(topk))
    return pl.pallas_call(
        kern,
        grid=(bsz, t_q // tq),
        in_specs=[pl.BlockSpec((1, tq, 8 * LANES), lambda b, i: (b, i, 0)),
                  pl.BlockSpec((1, tq, 256), lambda b, i: (b, i, 0)),
                  pl.BlockSpec((1, tq, LANES), lambda b, i: (b, i, 0)),
                  pl.BlockSpec((1, 2, s, LANES), lambda b, i: (b, 0, 0, 0)),
                  pl.BlockSpec((1, s, LANES), lambda b, i: (b, 0, 0)),
                  pl.BlockSpec((1, s // tk, LANES, tk), lambda b, i: (b, 0, 0, 0))],
        out_specs=pl.BlockSpec((1, tq, 512), lambda b, i: (b, i, 0)),
        out_shape=jax.ShapeDtypeStruct((bsz, t_q, 512), BF16),
        scratch_shapes=[pltpu.VMEM((s // tk, tk, tq), I32),
                        pltpu.VMEM((s // tk, tk, tq), BF16),
                        pltpu.VMEM((D_KV_HEADS, 1, D_HEADS // D_KV_HEADS * tq), F32),
                        pltpu.VMEM((D_KV_HEADS, 1, D_HEADS // D_KV_HEADS * tq), F32),
                        pltpu.VMEM((D_KV_HEADS, 64 + ONES_ROWS, D_HEADS // D_KV_HEADS * tq), F32),
                        pltpu.VMEM((D_KV_HEADS, tk, D_HEADS // D_KV_HEADS * tq), F32)],
        compiler_params=_params(("parallel", "arbitrary")),
        name="dsa_attn",
    )(dq, iq, iw, ik2, k2, vt)


def _prep_ab(w_in, w_gk):
    w = jnp.pad(w_in, ((0, 0), (0, AB_COLS - w_in.shape[1]))).astype(BF16)
    wgk = jnp.pad(w_gk, ((0, LANES - B_GATE_RANK), (0, 0))).astype(BF16)
    return w, wgk


def _prep_cd(w_in, f_bias):
    o = np.cumsum([0, 512, 512, 512, C_HEADS, 512, 128, 128, 256, IDX_DIM, IDX_HEADS])
    c_q, c_k, c_v, c_f, d_q, d_k, d_v, d_iq, d_ik, d_iw = (w_in[:, o[i]:o[i + 1]] for i in range(10))
    zeros = lambda n: jnp.zeros((D_MODEL, n), w_in.dtype)
    dq_cols = []
    for h in range(D_HEADS):
        g = h // (D_HEADS // D_KV_HEADS)
        wh = d_q[:, h * 64:(h + 1) * 64]
        dq_cols += [wh, zeros(64)] if g == 0 else [zeros(64), wh]
    w = jnp.concatenate([c_q, c_k, c_v] + dq_cols + [d_k, d_v, d_iq, d_ik, c_f, zeros(LANES - IDX_DIM - C_HEADS),
                                                     d_iw, zeros(LANES - IDX_HEADS)], axis=1).astype(BF16)
    fb = jnp.pad(f_bias.astype(F32), (IDX_DIM, LANES - IDX_DIM - C_HEADS)).reshape(1, LANES)
    return w, fb


def _pad_rows(z, s):
    return jnp.pad(z, ((0, 0), (0, s - z.shape[1]), (0, 0)))


def _pad_lanes(z, n=LANES):
    return jnp.pad(z, ((0, 0), (0, 0), (0, n - z.shape[2])))


def _trunk(x, s_a, s_b, cache, wts):
    bsz, t_len, _ = x.shape
    row = lambda z: z.reshape(1, -1).astype(F32)

    t_pad = -(-t_len // REC_CHUNK) * REC_CHUNK
    x1, sa_new, sb_new = _ab_layer(
        _pad_rows(x, t_pad), s_a, s_b.reshape(bsz, 2, 128, 128), row(wts['norm_mix'][0]), wts['ab_w_in'],
        wts['gla_w_gk'], row(wts['gla_b_gk']), row(wts['lb']), row(wts['hgrn_gnorm']), row(wts['gla_gnorm']),
        wts['ab_w_out'], t_len)
    n = bsz * t_len
    x1 = x1[:, :t_len].reshape(n, D_MODEL)
    x2 = _ffn(x1, row(wts['norm_ffn'][0]), wts['ffn_w_in'][0], wts['ffn_w_out'][0])

    fq, ck, cv, dq, dk, dv, iq, misc, iw = _cd_proj(x2, row(wts['norm_mix'][1]), wts['cd_w_in'], wts['fox_f_bias'])
    per_b = lambda z: z.reshape(bsz, t_len, z.shape[-1])
    fq, ck, cv, dq, dk, dv, iq, misc, iw = map(per_b, (fq, ck, cv, dq, dk, dv, iq, misc, iw))
    logf = misc[:, :, IDX_DIM:IDX_DIM + C_HEADS]
    if cache is None:
        q_start = 0
        k_all, v_all, lf_all, dk_all, dv_all, ik_all = ck, cv, _pad_lanes(logf), dk, dv, misc
    else:
        c_k, c_v, c_lf, c_dk, c_dv, c_ik = cache
        q_start = c_k.shape[1]
        cat = lambda c, r: jnp.concatenate([c.reshape(bsz, q_start, -1).astype(F32), r], axis=1)
        k_all, v_all, dk_all, dv_all = cat(c_k, ck), cat(c_v, cv), cat(c_dk, dk), cat(c_dv, dv)
        lf_all = _pad_lanes(cat(c_lf, logf))
        ik_all = cat(_pad_lanes(c_ik), misc)
    s_valid = q_start + t_len
    tk = 512
    s_pad = -(-s_valid // tk) * tk
    k_all, v_all, lf_all, dk_all, dv_all, ik_all = (_pad_rows(z, s_pad) for z in
                                                    (k_all, v_all, lf_all, dk_all, dv_all, ik_all))
    tq_pad = -(-t_len // LANES) * LANES
    fox_tq = 512 if tq_pad % 512 == 0 else LANES
    k_aug, v_t, cum = _fox_pack(k_all, v_all, lf_all, tm=tk)
    q_aug = _q_pack(_pad_rows(fq, tq_pad), _pad_rows(cum[:, q_start:q_start + t_len], tq_pad),
                    tm=512 if tq_pad % 512 == 0 else LANES)
    o_c = _fox_attn(q_aug, k_aug, v_t, tq=fox_tq, tk=tk, q_start=q_start)[:, :t_len]
    k2, dv_t, ik2 = _dsa_pack(dk_all, dv_all, ik_all, tm=tk)
    o_d = _dsa_attn(_pad_rows(dq, tq_pad), _pad_rows(iq, tq_pad), _pad_rows(iw, tq_pad), ik2, k2, dv_t,
                    tq=256 if tq_pad % 256 == 0 else LANES, tk=tk, q_start=q_start, s_valid=s_valid)[:, :t_len]

    y = _ffn(x2, row(wts['norm_ffn'][1]), wts['ffn_w_in'][1], wts['ffn_w_out'][1],
             attn=(o_c.reshape(n, 512), o_d.reshape(n, 512), wts['cd_w_out']), g_final=row(wts['norm_final']))
    rows = (ck.reshape(1, bsz, t_len, C_HEADS, C_HD), cv.reshape(1, bsz, t_len, C_HEADS, C_HD),
            logf[None], dk.reshape(1, bsz, t_len, D_KV_HEADS, D_HD), dv.reshape(1, bsz, t_len, D_KV_HEADS, D_HD),
            misc[None, :, :, :IDX_DIM])
    return (y.reshape(bsz, t_len, D_MODEL), sa_new[None], sb_new.reshape(1, bsz, B_HEADS, B_DK, B_DV)) + rows


def kernel(x_prompt, x_sample, state_hgrn, state_gla, cache_fox_k, cache_fox_v, cache_fox_logf, cache_dsa_k, cache_dsa_v, cache_dsa_ik, norm_mix, norm_ffn, norm_final, ab_w_in, ab_w_out, hgrn_lb_logits, hgrn_gnorm, gla_w_gk, gla_b_gk, gla_gnorm, cd_w_in, cd_w_out, fox_f_bias, ffn_w_in, ffn_w_out):
    lbs = jnp.cumsum(jax.nn.softmax(hgrn_lb_logits.astype(F32), axis=0), axis=0)
    w_ab, w_gk = _prep_ab(ab_w_in[0], gla_w_gk[0])
    w_cd, fb = _prep_cd(cd_w_in[0], fox_f_bias[0])
    wts = dict(norm_mix=norm_mix, norm_ffn=norm_ffn, norm_final=norm_final, ab_w_in=w_ab,
               ab_w_out=ab_w_out[0].astype(BF16), lb=lbs[0], hgrn_gnorm=hgrn_gnorm[0], gla_w_gk=w_gk,
               gla_b_gk=gla_b_gk[0], gla_gnorm=gla_gnorm[0], cd_w_in=w_cd, cd_w_out=cd_w_out[0].astype(BF16),
               fox_f_bias=fb, ffn_w_in=ffn_w_in.astype(BF16), ffn_w_out=ffn_w_out.astype(BF16))
    bp = x_prompt.shape[0]
    p_out = _trunk(x_prompt, jnp.zeros((bp, A_HEADS, A_DK, A_DV), F32), jnp.zeros((bp, B_HEADS, B_DK, B_DV), F32),
                   None, wts)
    cache = (cache_fox_k[0], cache_fox_v[0], cache_fox_logf[0], cache_dsa_k[0], cache_dsa_v[0], cache_dsa_ik[0])
    s_out = _trunk(x_sample, state_hgrn[0], state_gla[0], cache, wts)
    return (p_out[0], s_out[0]) + tuple(p_out[1:]) + tuple(s_out[1:])
```

```python
import functools

import numpy as np
import jax
import jax.numpy as jnp
from jax import lax
from jax.experimental import pallas as pl
from jax.experimental.pallas import tpu as pltpu

F32 = jnp.float32
BF16 = jnp.bfloat16
I32 = jnp.int32

D_MODEL = 1024
CHUNK = 64
A_HEADS, A_DK, A_DV = 4, 128, 128
B_HEADS, B_DK, B_DV = 4, 64, 128
B_GATE_RANK = 16
B_GATE_NORM = 16.0
C_HEADS, C_HD = 8, 64
D_HEADS, D_KV_HEADS, D_HD = 8, 2, 64
IDX_HEADS, IDX_DIM = 4, 64
IDX_TOPK_MAX = 256
IDX_SCALE = (IDX_DIM ** -0.5) * (IDX_HEADS ** -0.5)
FFN_HIDDEN = ((8 * D_MODEL // 3 + 255) // 256) * 256

LANES = 128
RSUB = 16
REC_CHUNK = 128
VMEM_LIMIT = 56 * 1024 * 1024
NEG = -1e30
MASKED = -2e30
LOG2E = 1.4426950408889634
ONES_ROWS = 16
INT_MIN = -2 ** 31

AB_COLS = 4 * 512 + 256 + 256 + 512 + 512 + LANES
CD_COLS = 3 * 512 + 8 * LANES + 2 * LANES + 256 + LANES + LANES


def _dot(a, b):
    return jnp.dot(a, b, preferred_element_type=F32)


def _dot_nt(a, b):
    return lax.dot_general(a, b, (((1,), (1,)), ((), ())), preferred_element_type=F32)


def _rms(x, g, eps=1e-6):
    return x * lax.rsqrt(jnp.mean(x * x, axis=-1, keepdims=True) + eps) * g


def _silu(x):
    return x * jax.nn.sigmoid(x)


def _log_sigmoid(x):
    return jnp.minimum(x, 0.0) - jnp.log1p(jnp.exp(-jnp.abs(x)))


def _split3(x):
    hi = x.astype(BF16)
    r = x - hi.astype(F32)
    mid = r.astype(BF16)
    lo = (r - mid.astype(F32)).astype(BF16)
    return hi, mid, lo


def _tri_dot(tri, x):
    hi, mid, lo = _split3(x)
    return _dot(tri, hi) + _dot(tri, mid) + _dot(tri, lo)


def _iota(shape, dim):
    return lax.broadcasted_iota(I32, shape, dim)


def _const_spec(shape):
    zeros = (0,) * len(shape)
    return pl.BlockSpec(shape, lambda *_: zeros, pipeline_mode=pl.Buffered(1))


def _params(sem):
    return pltpu.CompilerParams(dimension_semantics=sem, vmem_limit_bytes=VMEM_LIMIT)


def _ab_kernel(x_ref, sa_ref, sb_ref, g_ref, win_ref, wgk_ref, bgk_ref, lb_ref, agn_ref, bgn_ref, wout_ref,
               xo_ref, sao_ref, sbo_ref, s_sc, h_sc, o_sc, *, t_valid, t_pad):
    C = REC_CHUNK
    tm = x_ref.shape[1]
    t = pl.program_id(1)

    @pl.when(t == 0)
    def _():
        s_sc[0:4] = sa_ref[0]
        s_sc[4:6] = sb_ref[0]

    h_sc[...] = _dot(_rms(x_ref[0], g_ref[...]).astype(BF16), win_ref[...])

    def chunk_body(c, carry):
        r0 = pl.multiple_of(c * C, C)
        _ab_chunk(r0, t * tm + r0, h_sc, o_sc, s_sc, wgk_ref, bgk_ref, lb_ref, agn_ref, bgn_ref,
                  t_valid=t_valid, t_pad=t_pad)
        return carry

    lax.fori_loop(0, tm // C, chunk_body, 0)
    xo_ref[0] = x_ref[0] + _dot(o_sc[...], wout_ref[...])

    @pl.when(t == pl.num_programs(1) - 1)
    def _():
        sao_ref[0] = s_sc[0:4]
        sbo_ref[0] = s_sc[4:6]


def _ab_chunk(r0, row0, h_sc, o_sc, s_sc, wgk_ref, bgk_ref, lb_ref, agn_ref, bgn_ref, *, t_valid, t_pad):
    C = REC_CHUNK
    cols = lambda a, b: h_sc[pl.ds(r0, C), a:b]
    a_q, a_f, a_i, a_g = cols(0, 512), cols(512, 1024), cols(1024, 1536), cols(1536, 2048)
    b_q, b_k, b_v, b_g = cols(2048, 2304), cols(2304, 2560), cols(2560, 3072), cols(3072, 3584)
    b_lr = cols(3584, 3712)

    lb = lb_ref[...]
    f = lb + (1.0 - lb) * jax.nn.sigmoid(a_f)
    gk = _dot(b_lr.astype(BF16), wgk_ref[...]) + bgk_ref[...]
    la = jnp.concatenate([jnp.log2(f), _log_sigmoid(gk) * (LOG2E / B_GATE_NORM)], axis=1)
    q = jnp.concatenate([_silu(a_q), b_q * (B_DK ** -0.5)], axis=1)
    k = jnp.concatenate([1.0 - f, b_k], axis=1)
    v_a, v_b = a_i, b_v
    if t_valid < t_pad:
        ok = (row0 + _iota((C, 1), 0)) < t_valid
        la = jnp.where(ok, la, 0.0)
        k = jnp.where(ok, k, 0.0)
        v_a = jnp.where(ok, v_a, 0.0)
        v_b = jnp.where(ok, v_b, 0.0)

    row = _iota((C, C), 0)
    col = _iota((C, C), 1)
    causal = col <= row
    tri = jnp.where(causal, 1.0, 0.0).astype(BF16)
    tri_in = jnp.where(causal & ((row >> 4) == (col >> 4)), 1.0, 0.0).astype(BF16)
    bc = _tri_dot(tri, la)
    b_in = _tri_dot(tri_in, la)
    lane = _iota((1, LANES), 1)
    lo_half = lane < 64
    srow = _iota((LANES, 1), 0) < 64

    o_heads = [None] * 8
    for u in range(6):
        sl = slice(u * LANES, (u + 1) * LANES)
        qu, ku, bcu = q[:, sl], k[:, sl], bc[:, sl]
        s_old = s_sc[u]
        bend = bcu[C - 1:C, :]
        qt = qu * jnp.exp2(b_in[:, sl])
        qdec = qu * jnp.exp2(bcu)
        if u < 4:
            heads = [(u, None, v_a[:, sl])]
        else:
            ha = 4 + 2 * (u - 4)
            heads = [(ha, lo_half, v_b[:, (ha - 4) * LANES:(ha - 3) * LANES]),
                     (ha + 1, jnp.logical_not(lo_half), v_b[:, (ha - 3) * LANES:(ha - 2) * LANES])]
        a_rows = [[] for _ in heads]
        for i in range(C // RSUB):
            n = RSUB * (i + 1)
            if i == 0:
                kt = ku[0:n] * jnp.exp2(-bcu[0:n])
            else:
                kt = ku[0:n] * jnp.exp2(bcu[RSUB * i - 1:RSUB * i, :] - bcu[0:n])
            if n < C:
                kt = jnp.concatenate([kt, jnp.zeros((C - n, LANES), F32)], axis=0)
            ktb = kt.astype(BF16)
            qi = qt[RSUB * i:RSUB * (i + 1)]
            for hi_, (_, msk, _) in enumerate(heads):
                qim = qi if msk is None else jnp.where(msk, qi, 0.0)
                a_rows[hi_].append(_dot_nt(qim.astype(BF16), ktb))
        sb16 = s_old.astype(BF16)
        for hi_, (hd, msk, vh) in enumerate(heads):
            att = jnp.where(causal, jnp.concatenate(a_rows[hi_], axis=0), 0.0)
            qd = qdec if msk is None else jnp.where(msk, qdec, 0.0)
            o_heads[hd] = _dot(qd.astype(BF16), sb16) + _dot(att.astype(BF16), vh.astype(BF16))
        kht = (ku * jnp.exp2(bend - bcu)).T
        dcol = jnp.broadcast_to(jnp.exp2(bend), (LANES, LANES)).T
        if u < 4:
            upd = _dot(kht.astype(BF16), heads[0][2].astype(BF16))
        else:
            lhs = jnp.concatenate([jnp.where(srow, kht, 0.0), jnp.where(srow, 0.0, kht)], axis=1)
            rhs = jnp.concatenate([heads[0][2], heads[1][2]], axis=0)
            upd = _dot(lhs.astype(BF16), rhs.astype(BF16))
        s_sc[u] = dcol * s_old + upd

    outs = []
    for hd in range(8):
        if hd < 4:
            gn, gate = agn_ref[...], a_g[:, hd * LANES:(hd + 1) * LANES]
        else:
            gn, gate = bgn_ref[...], b_g[:, (hd - 4) * LANES:(hd - 3) * LANES]
        outs.append(_rms(o_heads[hd], gn) * _silu(gate))
    o_sc[pl.ds(r0, C), :] = jnp.concatenate(outs, axis=1).astype(BF16)


def _ab_layer(x, s_a, s_b, g, w_in, w_gk, b_gk, lb, a_gn, b_gn, w_out, t_valid):
    bsz, t_pad, _ = x.shape
    tm = 512 if t_pad % 512 == 0 else REC_CHUNK
    kern = functools.partial(_ab_kernel, t_valid=t_valid, t_pad=t_pad)
    return pl.pallas_call(
        kern,
        grid=(bsz, t_pad // tm),
        in_specs=[
            pl.BlockSpec((1, tm, D_MODEL), lambda b, t: (b, t, 0)),
            pl.BlockSpec((1, 4, 128, 128), lambda b, t: (b, 0, 0, 0)),
            pl.BlockSpec((1, 2, 128, 128), lambda b, t: (b, 0, 0, 0)),
            _const_spec((1, D_MODEL)),
            _const_spec((D_MODEL, AB_COLS)),
            _const_spec((LANES, 256)),
            _const_spec((1, 256)),
            _const_spec((1, 512)),
            _const_spec((1, 128)),
            _const_spec((1, 128)),
            _const_spec((D_MODEL, D_MODEL)),
        ],
        out_specs=[
            pl.BlockSpec((1, tm, D_MODEL), lambda b, t: (b, t, 0)),
            pl.BlockSpec((1, 4, 128, 128), lambda b, t: (b, 0, 0, 0)),
            pl.BlockSpec((1, 2, 128, 128), lambda b, t: (b, 0, 0, 0)),
        ],
        out_shape=[
            jax.ShapeDtypeStruct((bsz, t_pad, D_MODEL), F32),
            jax.ShapeDtypeStruct((bsz, 4, 128, 128), F32),
            jax.ShapeDtypeStruct((bsz, 2, 128, 128), F32),
        ],
        scratch_shapes=[pltpu.VMEM((6, 128, 128), F32), pltpu.VMEM((tm, AB_COLS), F32),
                        pltpu.VMEM((tm, D_MODEL), BF16)],
        compiler_params=_params(("parallel", "arbitrary")),
        name="ab_layer",
    )(x, s_a, s_b, g, w_in, w_gk, b_gk, lb, a_gn, b_gn, w_out)


FFN_TILE = FFN_HIDDEN // 2


def _ffn_kernel(*refs, has_attn, has_final):
    refs = list(refs)
    x_ref = refs.pop(0)
    x = x_ref[...]
    if has_attn:
        oc_ref, od_ref, wo_ref = refs.pop(0), refs.pop(0), refs.pop(0)
        x = x + _dot(jnp.concatenate([oc_ref[...], od_ref[...]], axis=1), wo_ref[...])
    g_ref, win_ref, wout_ref = refs.pop(0), refs.pop(0), refs.pop(0)
    gf_ref = refs.pop(0) if has_final else None
    out_ref = refs.pop(0)
    xn = _rms(x, g_ref[...]).astype(BF16)
    acc = x
    for j in range(FFN_HIDDEN // FFN_TILE):
        gate = _dot(xn, win_ref[:, j * FFN_TILE:(j + 1) * FFN_TILE])
        up = _dot(xn, win_ref[:, FFN_HIDDEN + j * FFN_TILE:FFN_HIDDEN + (j + 1) * FFN_TILE])
        act = (_silu(gate) * up).astype(BF16)
        acc = acc + _dot(act, wout_ref[j * FFN_TILE:(j + 1) * FFN_TILE, :])
    if has_final:
        acc = _rms(acc, gf_ref[...])
    out_ref[...] = acc


def _ffn(x, g, w_in, w_out, attn=None, g_final=None, tm=512):
    n = x.shape[0]
    tm = min(tm, n)
    row = lambda i: (i, 0)
    args = [x]
    specs = [pl.BlockSpec((tm, D_MODEL), row)]
    if attn is not None:
        oc, od, wo = attn
        args += [oc, od, wo]
        specs += [pl.BlockSpec((tm, 512), row), pl.BlockSpec((tm, 512), row), _const_spec((D_MODEL, D_MODEL))]
    args += [g, w_in, w_out]
    specs += [_const_spec((1, D_MODEL)), _const_spec((D_MODEL, 2 * FFN_HIDDEN)), _const_spec((FFN_HIDDEN, D_MODEL))]
    if g_final is not None:
        args.append(g_final)
        specs.append(_const_spec((1, D_MODEL)))
    kern = functools.partial(_ffn_kernel, has_attn=attn is not None, has_final=g_final is not None)
    return pl.pallas_call(
        kern,
        grid=(n // tm,),
        in_specs=specs,
        out_specs=pl.BlockSpec((tm, D_MODEL), row),
        out_shape=jax.ShapeDtypeStruct((n, D_MODEL), F32),
        compiler_params=_params(("parallel",)),
        name="ffn",
    )(*args)


def _cd_proj_kernel(x_ref, g_ref, w_ref, fb_ref, fq_ref, ck_ref, cv_ref, dq_ref, dk_ref, dv_ref, iq_ref,
                    misc_ref, iw_ref):
    xn = _rms(x_ref[...], g_ref[...]).astype(BF16)
    h = _dot(xn, w_ref[...])
    fq_ref[...] = (h[:, 0:512] * (C_HD ** -0.5 * LOG2E)).astype(BF16)
    ck_ref[...] = h[:, 512:1024]
    cv_ref[...] = h[:, 1024:1536]
    dq_ref[...] = (h[:, 1536:2560] * (D_HD ** -0.5 * LOG2E)).astype(BF16)
    dk_ref[...] = h[:, 2560:2688]
    dv_ref[...] = h[:, 2688:2816]
    iq_ref[...] = h[:, 2816:3072].astype(BF16)
    misc = h[:, 3072:3200]
    lane = _iota((1, LANES), 1)
    is_f = (lane >= IDX_DIM) & (lane < IDX_DIM + C_HEADS)
    misc_ref[...] = jnp.where(is_f, _log_sigmoid(misc + fb_ref[...]), misc)
    iw_ref[...] = h[:, 3200:3328]


def _cd_proj(x, g, w, fb, tm=512):
    n = x.shape[0]
    tm = min(tm, n)
    row = lambda i: (i, 0)
    widths = [(512, BF16), (512, F32), (512, F32), (1024, BF16), (128, F32), (128, F32), (256, BF16),
              (128, F32), (128, F32)]
    return pl.pallas_call(
        _cd_proj_kernel,
        grid=(n // tm,),
        in_specs=[pl.BlockSpec((tm, D_MODEL), row), _const_spec((1, D_MODEL)), _const_spec((D_MODEL, CD_COLS)),
                  _const_spec((1, LANES))],
        out_specs=[pl.BlockSpec((tm, w_), row) for w_, _ in widths],
        out_shape=[jax.ShapeDtypeStruct((n, w_), dt) for w_, dt in widths],
        compiler_params=_params(("parallel",)),
        name="cd_proj",
    )(x, g, w, fb)


def _aug_consts(is_query):
    p = np.zeros((3, LANES, 8 * LANES), np.float32)
    ones = np.zeros((1, 8 * LANES), np.float32)
    for h in range(8):
        off = h * LANES + (64 if h % 2 == 0 else 0)
        for c in range(3):
            if is_query:
                p[c, h, off + c] = 1.0
                ones[0, off + 3 + c] = 1.0
            else:
                p[c, h, off + 3 + c] = -1.0
                ones[0, off + c] = 1.0
    return jnp.asarray(p, BF16), jnp.asarray(ones, F32)


def _aug_lanes(cum, p_ref, ones_ref):
    hi, mid, lo = _split3(cum * LOG2E)
    return _dot(hi, p_ref[0]) + _dot(mid, p_ref[1]) + _dot(lo, p_ref[2]) + ones_ref[...]


def _running_sum(lf, carry):
    tm = lf.shape[0]

    @pl.when(pl.program_id(1) == 0)
    def _():
        carry[...] = jnp.zeros_like(carry)

    tri = jnp.where(_iota((tm, tm), 1) <= _iota((tm, tm), 0), 1.0, 0.0).astype(BF16)
    cum = _tri_dot(tri, lf) + carry[...]
    carry[...] = cum[tm - 1:tm, :]
    return cum


def _fox_rows(kk, vv, cum, p_ref, ones_ref, ka_ref, vt_ref):
    aug = _aug_lanes(cum, p_ref, ones_ref)
    lane = _iota((1, LANES), 1)
    for h in range(8):
        pr = slice((h // 2) * LANES, (h // 2 + 1) * LANES)
        own = (lane < 64) if h % 2 == 0 else (lane >= 64)
        ka_ref[0, h] = jnp.where(own, kk[:, pr], aug[:, h * LANES:(h + 1) * LANES]).astype(BF16)
    for pr in range(4):
        vt_ref[0, pr, 0] = vv[:, pr * LANES:(pr + 1) * LANES].T.astype(BF16)


def _fox_pack_kernel(k_ref, v_ref, lf_ref, p_ref, ones_ref, ka_ref, vt_ref, cum_ref, carry):
    cum = _running_sum(lf_ref[0], carry)
    cum_ref[0] = cum
    _fox_rows(k_ref[0], v_ref[0], cum, p_ref, ones_ref, ka_ref, vt_ref)


def _fox_pack(k, v, lf, tm):
    bsz, s, _ = k.shape
    p, ones = _aug_consts(False)
    return pl.pallas_call(
        _fox_pack_kernel,
        grid=(bsz, s // tm),
        in_specs=[pl.BlockSpec((1, tm, 512), lambda b, t: (b, t, 0)),
                  pl.BlockSpec((1, tm, 512), lambda b, t: (b, t, 0)),
                  pl.BlockSpec((1, tm, LANES), lambda b, t: (b, t, 0)),
                  _const_spec((3, LANES, 8 * LANES)), _const_spec((1, 8 * LANES))],
        out_specs=[pl.BlockSpec((1, 8, tm, LANES), lambda b, t: (b, 0, t, 0)),
                   pl.BlockSpec((1, 4, 1, LANES, tm), lambda b, t: (b, 0, t, 0, 0)),
                   pl.BlockSpec((1, tm, LANES), lambda b, t: (b, t, 0))],
        out_shape=[jax.ShapeDtypeStruct((bsz, 8, s, LANES), BF16),
                   jax.ShapeDtypeStruct((bsz, 4, s // tm, LANES, tm), BF16),
                   jax.ShapeDtypeStruct((bsz, s, LANES), F32)],
        scratch_shapes=[pltpu.VMEM((1, LANES), F32)],
        compiler_params=_params(("parallel", "arbitrary")),
        name="fox_pack",
    )(k, v, lf, p, ones)


def _q_rows(qq, cum, p_ref, ones_ref, qa_ref):
    aug = _aug_lanes(cum, p_ref, ones_ref)
    lane = _iota((1, LANES), 1)
    for h in range(8):
        pr = slice((h // 2) * LANES, (h // 2 + 1) * LANES)
        own = (lane < 64) if h % 2 == 0 else (lane >= 64)
        qa_ref[0, h] = jnp.where(own, qq[:, pr], aug[:, h * LANES:(h + 1) * LANES]).astype(BF16)


def _q_pack_kernel(q_ref, cum_ref, p_ref, ones_ref, qa_ref):
    _q_rows(q_ref[0].astype(F32), cum_ref[0], p_ref, ones_ref, qa_ref)


def _q_pack(q, cum_q, tm):
    bsz, tq, _ = q.shape
    p, ones = _aug_consts(True)
    return pl.pallas_call(
        _q_pack_kernel,
        grid=(bsz, tq // tm),
        in_specs=[pl.BlockSpec((1, tm, 512), lambda b, t: (b, t, 0)),
                  pl.BlockSpec((1, tm, LANES), lambda b, t: (b, t, 0)),
                  _const_spec((3, LANES, 8 * LANES)), _const_spec((1, 8 * LANES))],
        out_specs=pl.BlockSpec((1, 8, tm, LANES), lambda b, t: (b, 0, t, 0)),
        out_shape=jax.ShapeDtypeStruct((bsz, 8, tq, LANES), BF16),
        compiler_params=_params(("parallel", "parallel")),
        name="q_pack",
    )(q, cum_q, p, ones)


def _fox_attn_kernel(q_ref, k_ref, vt_ref, o_ref, s_sc, *, tq, tk, q_start):
    i = pl.program_id(2)
    q_lo = q_start + i * tq
    n_full = (q_lo + 1) // tk
    n_tiles = (q_lo + tq + tk - 1) // tk
    q_pos = q_lo + _iota((1, tq), 1)
    qs = (q_ref[0, 0], q_ref[0, 1])

    def logits(j):
        off = pl.multiple_of(j * tk, tk)
        return [_dot_nt(k_ref[0, hh, pl.ds(off, tk), :], qs[hh]) for hh in range(2)]

    s_sc[0], s_sc[1] = logits(0)

    def body(j, carry, masked):
        off = pl.multiple_of(j * tk, tk)
        s_next = logits(jnp.minimum(j + 1, n_tiles - 1))
        ss = [s_sc[0], s_sc[1]]
        if masked:
            ok = (off + _iota((tk, 1), 0)) <= q_pos
            ss = [jnp.where(ok, s, MASKED) for s in ss]
        out = []
        for hh in range(2):
            m, acc = carry[hh]
            m_new = jnp.maximum(m, jnp.max(ss[hh], axis=0, keepdims=True))
            p = jnp.exp2(ss[hh] - m_new).astype(BF16)
            vt = jnp.concatenate([vt_ref[0, 0, j, hh * 64:(hh + 1) * 64, :], ones], axis=0)
            acc = jnp.exp2(m - m_new) * acc + _dot(vt, p)
            out.append((m_new, acc))
        s_sc[0], s_sc[1] = s_next
        return tuple(out)

    ones = jnp.ones((ONES_ROWS, tk), BF16)
    init = (jnp.full((1, tq), NEG, F32), jnp.zeros((64 + ONES_ROWS, tq), F32))
    carry = lax.fori_loop(0, n_full, functools.partial(body, masked=False), (init, init))
    carry = lax.fori_loop(n_full, n_tiles, functools.partial(body, masked=True), carry)
    o_t = jnp.concatenate([acc[0:64] / acc[64:65] for _, acc in carry], axis=0)
    o_ref[0] = o_t.T.astype(BF16)


def _fox_attn(q_aug, k_aug, vt, tq, tk, q_start):
    bsz, _, t_q, _ = q_aug.shape
    s = k_aug.shape[2]
    kern = functools.partial(_fox_attn_kernel, tq=tq, tk=tk, q_start=q_start)
    return pl.pallas_call(
        kern,
        grid=(bsz, 4, t_q // tq),
        in_specs=[pl.BlockSpec((1, 2, tq, LANES), lambda b, p, i: (b, p, i, 0)),
                  pl.BlockSpec((1, 2, s, LANES), lambda b, p, i: (b, p, 0, 0)),
                  pl.BlockSpec((1, 1, s // tk, LANES, tk), lambda b, p, i: (b, p, 0, 0, 0))],
        out_specs=pl.BlockSpec((1, tq, LANES), lambda b, p, i: (b, i, p)),
        out_shape=jax.ShapeDtypeStruct((bsz, t_q, 512), BF16),
        scratch_shapes=[pltpu.VMEM((2, tk, tq), F32)],
        compiler_params=_params(("parallel", "parallel", "arbitrary")),
        name="fox_attn",
    )(q_aug, k_aug, vt)


def _dsa_rows(dk, dv, ik, k2_ref, vt_ref, ik2_ref):
    lo = _iota((1, LANES), 1) < 64
    k2_ref[0] = dk.astype(BF16)
    vt_ref[0, 0] = dv.T.astype(BF16)
    ik = jnp.where(lo, ik, 0.0)
    ik2_ref[0, 0] = ik.astype(BF16)
    ik2_ref[0, 1] = pltpu.roll(ik, 64, 1).astype(BF16)


def _dsa_pack_kernel(dk_ref, dv_ref, ik_ref, k2_ref, vt_ref, ik2_ref):
    _dsa_rows(dk_ref[0], dv_ref[0], ik_ref[0], k2_ref, vt_ref, ik2_ref)


def _dsa_pack(dk, dv, ik, tm):
    bsz, s, _ = dk.shape
    tok = pl.BlockSpec((1, tm, LANES), lambda b, t: (b, t, 0))
    return pl.pallas_call(
        _dsa_pack_kernel,
        grid=(bsz, s // tm),
        in_specs=[tok, tok, tok],
        out_specs=[tok,
                   pl.BlockSpec((1, 1, LANES, tm), lambda b, t: (b, t, 0, 0)),
                   pl.BlockSpec((1, 2, tm, LANES), lambda b, t: (b, 0, t, 0))],
        out_shape=[jax.ShapeDtypeStruct((bsz, s, LANES), BF16),
                   jax.ShapeDtypeStruct((bsz, s // tm, LANES, tm), BF16),
                   jax.ShapeDtypeStruct((bsz, 2, s, LANES), BF16)],
        compiler_params=_params(("parallel", "parallel")),
        name="dsa_pack",
    )(dk, dv, ik)


def _cd_fused_kernel(x_ref, g_ref, w_ref, fb_ref, pk_ref, onesk_ref, pq_ref, onesq_ref,
                     ck_ref, cv_ref, dk_ref, dv_ref, misc_ref, iw_ref, iq_ref, dq_ref,
                     qa_ref, ka_ref, vt_ref, k2_ref, dvt_ref, ik2_ref, carry):
    xn = _rms(x_ref[0], g_ref[...]).astype(BF16)
    h = _dot(xn, w_ref[...])
    ck, cv = h[:, 512:1024], h[:, 1024:1536]
    dk, dv = h[:, 2560:2688], h[:, 2688:2816]
    misc = h[:, 3072:3200]
    lane = _iota((1, LANES), 1)
    is_f = (lane >= IDX_DIM) & (lane < IDX_DIM + C_HEADS)
    logf = _log_sigmoid(misc + fb_ref[...])
    misc = jnp.where(is_f, logf, misc)
    ck_ref[0], cv_ref[0], dk_ref[0], dv_ref[0], misc_ref[0] = ck, cv, dk, dv, misc
    iw_ref[0] = h[:, 3200:3328]
    iq_ref[0] = h[:, 2816:3072].astype(BF16)
    dq_ref[0] = (h[:, 1536:2560] * (D_HD ** -0.5 * LOG2E)).astype(BF16)
    cum = _running_sum(pltpu.roll(jnp.where(is_f, logf, 0.0), LANES - IDX_DIM, 1), carry)
    _fox_rows(ck, cv, cum, pk_ref, onesk_ref, ka_ref, vt_ref)
    _q_rows(h[:, 0:512] * (C_HD ** -0.5 * LOG2E), cum, pq_ref, onesq_ref, qa_ref)
    _dsa_rows(dk, dv, misc, k2_ref, dvt_ref, ik2_ref)


def _cd_fused(x, g, w, fb, tm):
    bsz, t_len, _ = x.shape
    pk, onesk = _aug_consts(False)
    pq, onesq = _aug_consts(True)
    tok = lambda w_: pl.BlockSpec((1, tm, w_), lambda b, t: (b, t, 0))
    hm = pl.BlockSpec((1, 8, tm, LANES), lambda b, t: (b, 0, t, 0))
    sds = jax.ShapeDtypeStruct
    n_t = t_len // tm
    return pl.pallas_call(
        _cd_fused_kernel,
        grid=(bsz, n_t),
        in_specs=[tok(D_MODEL), _const_spec((1, D_MODEL)), _const_spec((D_MODEL, CD_COLS)), _const_spec((1, LANES)),
                  _const_spec((3, LANES, 8 * LANES)), _const_spec((1, 8 * LANES)),
                  _const_spec((3, LANES, 8 * LANES)), _const_spec((1, 8 * LANES))],
        out_specs=[tok(512), tok(512), tok(LANES), tok(LANES), tok(LANES), tok(LANES), tok(256), tok(8 * LANES),
                   hm, hm, pl.BlockSpec((1, 4, 1, LANES, tm), lambda b, t: (b, 0, t, 0, 0)),
                   tok(LANES), pl.BlockSpec((1, 1, LANES, tm), lambda b, t: (b, t, 0, 0)),
                   pl.BlockSpec((1, 2, tm, LANES), lambda b, t: (b, 0, t, 0))],
        out_shape=[sds((bsz, t_len, 512), F32), sds((bsz, t_len, 512), F32), sds((bsz, t_len, LANES), F32),
                   sds((bsz, t_len, LANES), F32), sds((bsz, t_len, LANES), F32), sds((bsz, t_len, LANES), F32),
                   sds((bsz, t_len, 256), BF16), sds((bsz, t_len, 8 * LANES), BF16),
                   sds((bsz, 8, t_len, LANES), BF16), sds((bsz, 8, t_len, LANES), BF16),
                   sds((bsz, 4, n_t, LANES, tm), BF16), sds((bsz, t_len, LANES), BF16),
                   sds((bsz, n_t, LANES, tm), BF16), sds((bsz, 2, t_len, LANES), BF16)],
        scratch_shapes=[pltpu.VMEM((1, LANES), F32)],
        compiler_params=_params(("parallel", "arbitrary")),
        name="cd_fused",
    )(x, g, w, fb, pk, onesk, pq, onesq)


def _dsa_attn_kernel(dq_ref, iq_ref, iw_ref, ik2_ref, k2_ref, vt_ref, o_ref, keys_sc, hi_sc, m_sc, mt_sc, acc_sc, s_sc,
                     *, tq, tk, q_start, s_valid, topk):
    i = pl.program_id(1)
    q_lo = q_start + i * tq
    adm_row = jnp.minimum(((q_lo + _iota((1, tq), 1)) // CHUNK + 1) * CHUNK, s_valid)
    adm_end = jnp.minimum(((q_lo + tq - 1) // CHUNK + 1) * CHUNK, s_valid)
    n_tiles = (adm_end + tk - 1) // tk
    iw_t = iw_ref[0].T
    iq = iq_ref[0]

    def score_body(j, c):
        off = pl.multiple_of(j * tk, tk)
        sc = jnp.zeros((tk, tq), F32)
        for hd in range(IDX_HEADS):
            qp = iq[:, (hd // 2) * LANES:(hd // 2 + 1) * LANES]
            r = jnp.maximum(_dot_nt(ik2_ref[0, hd % 2, pl.ds(off, tk), :], qp), 0.0)
            sc = sc + r * iw_t[hd:hd + 1, :]
        sc = sc * IDX_SCALE
        adm = (off + _iota((tk, tq), 0)) < adm_row
        bits = lax.bitcast_convert_type(sc, I32)
        bits = jnp.where(bits == INT_MIN, 0, bits)
        key = bits ^ ((bits >> 31) & 0x7FFFFFFF)
        keys_sc[j] = jnp.where(adm, key, INT_MIN)
        hi = lax.bitcast_convert_type(bits & -65536, F32)
        hi_sc[j] = jnp.where(adm, hi, -jnp.inf).astype(BF16)
        return c

    lax.fori_loop(0, n_tiles, score_body, 0)

    def pairs(one, init):
        part = lax.fori_loop(0, n_tiles // 2, lambda jj, c: one(2 * jj + 1, one(2 * jj, c)), init)
        return lax.fori_loop(2 * (n_tiles // 2), n_tiles, one, part)

    one16 = jnp.ones((tk, tq), BF16)
    zero16 = jnp.zeros((tk, tq), BF16)

    def hi_body(b, pre):
        cand = pre + lax.shift_left(jnp.int32(1), 15 - b)
        cbits = (cand ^ ((cand >> 15) & 0x7FFF)) & 0xFFFF
        subnormal = ((cbits & 0x7F80) == 0) & ((cbits & 0x007F) != 0)
        cbits = jnp.where(subnormal, jnp.where((cbits & 0x8000) != 0, 0x0000, 0x0080), cbits)
        cval = jnp.broadcast_to(lax.bitcast_convert_type(cbits << 16, F32).astype(BF16), (tk, tq))

        def one(j, c):
            hit = jnp.where(hi_sc[j] >= cval, one16, zero16).reshape(tk // 64, 4, 16, tq)
            for r in range(tk // 64):
                c = c + hit[r]
            return c
        part = pairs(one, jnp.zeros((4, 16, tq), BF16)).astype(F32)
        cnt = jnp.sum(jnp.sum(part, axis=0), axis=0, keepdims=True)
        return jnp.where(cnt >= topk, cand, pre)

    pre = lax.fori_loop(0, 16, hi_body, jnp.full((1, tq), -2 ** 15, I32))

    def count(pred):
        def one(j, c):
            hit = jnp.where(pred(j, keys_sc[j]), 1.0, 0.0)
            return c + jnp.sum(hit.reshape(tk // 32, 4, 8, tq), axis=0)

        part = pairs(one, jnp.zeros((4, 8, tq), F32))
        return jnp.sum(jnp.sum(part, axis=0), axis=0, keepdims=True)

    def tau_body(b, tau):
        cand = tau + lax.shift_left(jnp.int32(1), 31 - b)
        cnt = count(lambda j, kk: kk >= cand)
        return jnp.where(cnt >= topk, cand, tau)

    tau = lax.fori_loop(16, 32, tau_body, pre << 16)
    tau = jnp.where(adm_row < int(topk), INT_MIN, tau)
    need = jnp.where(tau == INT_MIN, float(2 ** 20), topk - count(lambda j, kk: kk > tau))
    tied = jnp.max(count(lambda j, kk: kk == tau) - need) > 0.0

    def pos_body(b, p):
        cand = p + lax.shift_left(jnp.int32(1), 12 - b)

        def pred(j, kk):
            kpos = j * tk + _iota((tk, tq), 0)
            return (kk == tau) & (kpos < cand)
        cnt = count(pred)
        return jnp.where(cnt < need, cand, p)

    p_cut = lax.fori_loop(0, jnp.where(tied, 13, 0), pos_body,
                          jnp.full((1, tq), jnp.where(tied, 0, 2 ** 20), I32))

    m_sc[...] = jnp.full(m_sc.shape, NEG, F32)
    acc_sc[...] = jnp.zeros(acc_sc.shape, F32)
    ones = jnp.ones((ONES_ROWS, tk), BF16)
    n_rep = D_HEADS // D_KV_HEADS
    q_stack = [jnp.concatenate([dq_ref[0, :, hd * LANES:(hd + 1) * LANES]
                                for hd in range(g * n_rep, (g + 1) * n_rep)], axis=0) for g in range(D_KV_HEADS)]

    def logits(j):
        kk = keys_sc[j]
        kpos = j * tk + _iota((tk, tq), 0)
        sel = ((kk > tau) | ((kk == tau) & (kpos <= p_cut))) & (kk != INT_MIN)
        bias = jnp.where(sel, 0.0, MASKED)
        bias = jnp.concatenate([bias] * n_rep, axis=1)
        kt = k2_ref[0, pl.ds(pl.multiple_of(j * tk, tk), tk), :]
        ss = [_dot_nt(kt, q_stack[g]) + bias for g in range(D_KV_HEADS)]
        return ss, [jnp.max(s, axis=0, keepdims=True) for s in ss]

    def stage(slot, ss, mts):
        for g in range(D_KV_HEADS):
            s_sc[slot, g] = ss[g]
            mt_sc[slot, g] = mts[g]

    stage(0, *logits(0))

    def step(j, cur):
        stage(1 - cur, *logits(jnp.minimum(j + 1, n_tiles - 1)))
        for g in range(D_KV_HEADS):
            m_old = m_sc[g]
            m_new = jnp.maximum(m_old, mt_sc[cur, g])
            p = jnp.exp2(s_sc[cur, g] - m_new).astype(BF16)
            vt = jnp.concatenate([vt_ref[0, j, g * 64:(g + 1) * 64, :], ones], axis=0)
            acc_sc[g] = jnp.exp2(m_old - m_new) * acc_sc[g] + _dot(vt, p)
            m_sc[g] = m_new

    def attn_body(jj, c):
        step(2 * jj, 0)

        @pl.when(2 * jj + 1 < n_tiles)
        def _():
            step(2 * jj + 1, 1)
        return c

    lax.fori_loop(0, (n_tiles + 1) // 2, attn_body, 0)
    for g in range(D_KV_HEADS):
        o_g = acc_sc[g, 0:64] / acc_sc[g, 64:65]
        for pr in range(n_rep // 2):
            o_t = jnp.concatenate([o_g[:, (2 * pr) * tq:(2 * pr + 1) * tq],
                                   o_g[:, (2 * pr + 1) * tq:(2 * pr + 2) * tq]], axis=0)
            col = (g * n_rep // 2 + pr) * LANES
            o_ref[0, :, col:col + LANES] = o_t.T.astype(BF16)


def _dsa_attn(dq, iq, iw, ik2, k2, vt, tq, tk, q_start, s_valid):
    bsz, t_q, _ = dq.shape
    s = k2.shape[1]
    topk = min(IDX_TOPK_MAX, s_valid // 4)
    kern = functools.partial(_dsa_attn_kernel, tq=tq, tk=tk, q_start=q_start, s_valid=s_valid, topk=float(topk))
    return pl.pallas_call(
        kern,
        grid=(bsz, t_q // tq),
        in_specs=[pl.BlockSpec((1, tq, 8 * LANES), lambda b, i: (b, i, 0)),
                  pl.BlockSpec((1, tq, 256), lambda b, i: (b, i, 0)),
                  pl.BlockSpec((1, tq, LANES), lambda b, i: (b, i, 0)),
                  pl.BlockSpec((1, 2, s, LANES), lambda b, i: (b, 0, 0, 0)),
                  pl.BlockSpec((1, s, LANES), lambda b, i: (b, 0, 0)),
                  pl.BlockSpec((1, s // tk, LANES, tk), lambda b, i: (b, 0, 0, 0))],
        out_specs=pl.BlockSpec((1, tq, 512), lambda b, i: (b, i, 0)),
        out_shape=jax.ShapeDtypeStruct((bsz, t_q, 512), BF16),
        scratch_shapes=[pltpu.VMEM((s // tk, tk, tq), I32),
                        pltpu.VMEM((s // tk, tk, tq), BF16),
                        pltpu.VMEM((D_KV_HEADS, 1, D_HEADS // D_KV_HEADS * tq), F32),
                        pltpu.VMEM((2, D_KV_HEADS, 1, D_HEADS // D_KV_HEADS * tq), F32),
                        pltpu.VMEM((D_KV_HEADS, 64 + ONES_ROWS, D_HEADS // D_KV_HEADS * tq), F32),
                        pltpu.VMEM((2, D_KV_HEADS, tk, D_HEADS // D_KV_HEADS * tq), F32)],
        compiler_params=_params(("parallel", "arbitrary")),
        name="dsa_attn",
    )(dq, iq, iw, ik2, k2, vt)


def _prep_ab(w_in, w_gk):
    w = jnp.pad(w_in, ((0, 0), (0, AB_COLS - w_in.shape[1]))).astype(BF16)
    wgk = jnp.pad(w_gk, ((0, LANES - B_GATE_RANK), (0, 0))).astype(BF16)
    return w, wgk


def _prep_cd(w_in, f_bias):
    o = np.cumsum([0, 512, 512, 512, C_HEADS, 512, 128, 128, 256, IDX_DIM, IDX_HEADS])
    c_q, c_k, c_v, c_f, d_q, d_k, d_v, d_iq, d_ik, d_iw = (w_in[:, o[i]:o[i + 1]] for i in range(10))
    zeros = lambda n: jnp.zeros((D_MODEL, n), w_in.dtype)
    dq_cols = []
    for h in range(D_HEADS):
        g = h // (D_HEADS // D_KV_HEADS)
        wh = d_q[:, h * 64:(h + 1) * 64]
        dq_cols += [wh, zeros(64)] if g == 0 else [zeros(64), wh]
    w = jnp.concatenate([c_q, c_k, c_v] + dq_cols + [d_k, d_v, d_iq, d_ik, c_f, zeros(LANES - IDX_DIM - C_HEADS),
                                                     d_iw, zeros(LANES - IDX_HEADS)], axis=1).astype(BF16)
    fb = jnp.pad(f_bias.astype(F32), (IDX_DIM, LANES - IDX_DIM - C_HEADS)).reshape(1, LANES)
    return w, fb


def _pad_rows(z, s):
    return jnp.pad(z, ((0, 0), (0, s - z.shape[1]), (0, 0)))


def _pad_lanes(z, n=LANES):
    return jnp.pad(z, ((0, 0), (0, 0), (0, n - z.shape[2])))


def _trunk(x, s_a, s_b, cache, wts):
    bsz, t_len, _ = x.shape
    row = lambda z: z.reshape(1, -1).astype(F32)

    t_pad = -(-t_len // REC_CHUNK) * REC_CHUNK
    x1, sa_new, sb_new = _ab_layer(
        _pad_rows(x, t_pad), s_a, s_b.reshape(bsz, 2, 128, 128), row(wts['norm_mix'][0]), wts['ab_w_in'],
        wts['gla_w_gk'], row(wts['gla_b_gk']), row(wts['lb']), row(wts['hgrn_gnorm']), row(wts['gla_gnorm']),
        wts['ab_w_out'], t_len)
    n = bsz * t_len
    x1 = x1[:, :t_len].reshape(n, D_MODEL)
    x2 = _ffn(x1, row(wts['norm_ffn'][0]), wts['ffn_w_in'][0], wts['ffn_w_out'][0])

    tk = 512
    q_start = 0 if cache is None else cache[0].shape[1]
    s_valid = q_start + t_len
    tq_pad = -(-t_len // LANES) * LANES
    if cache is None and t_len % tk == 0:
        (ck, cv, dk, dv, misc, iw, iq, dq, q_aug, k_aug, v_t, k2, dv_t, ik2) = _cd_fused(
            x2.reshape(bsz, t_len, D_MODEL), row(wts['norm_mix'][1]), wts['cd_w_in'], wts['fox_f_bias'], tm=tk)
        logf = misc[:, :, IDX_DIM:IDX_DIM + C_HEADS]
    else:
        fq, ck, cv, dq, dk, dv, iq, misc, iw = _cd_proj(x2, row(wts['norm_mix'][1]), wts['cd_w_in'],
                                                        wts['fox_f_bias'])
        per_b = lambda z: z.reshape(bsz, t_len, z.shape[-1])
        fq, ck, cv, dq, dk, dv, iq, misc, iw = map(per_b, (fq, ck, cv, dq, dk, dv, iq, misc, iw))
        logf = misc[:, :, IDX_DIM:IDX_DIM + C_HEADS]
        if cache is None:
            k_all, v_all, lf_all, dk_all, dv_all, ik_all = ck, cv, _pad_lanes(logf), dk, dv, misc
        else:
            c_k, c_v, c_lf, c_dk, c_dv, c_ik = cache
            cat = lambda c, r: jnp.concatenate([c.reshape(bsz, q_start, -1).astype(F32), r], axis=1)
            k_all, v_all, dk_all, dv_all = cat(c_k, ck), cat(c_v, cv), cat(c_dk, dk), cat(c_dv, dv)
            lf_all = _pad_lanes(cat(c_lf, logf))
            ik_all = cat(_pad_lanes(c_ik), misc)
        s_pad = -(-s_valid // tk) * tk
        k_all, v_all, lf_all, dk_all, dv_all, ik_all = (_pad_rows(z, s_pad) for z in
                                                        (k_all, v_all, lf_all, dk_all, dv_all, ik_all))
        k_aug, v_t, cum = _fox_pack(k_all, v_all, lf_all, tm=tk)
        q_aug = _q_pack(_pad_rows(fq, tq_pad), _pad_rows(cum[:, q_start:q_start + t_len], tq_pad),
                        tm=512 if tq_pad % 512 == 0 else LANES)
        k2, dv_t, ik2 = _dsa_pack(dk_all, dv_all, ik_all, tm=tk)
    o_c = _fox_attn(q_aug, k_aug, v_t, tq=512 if tq_pad % 512 == 0 else LANES, tk=tk, q_start=q_start)[:, :t_len]
    o_d = _dsa_attn(_pad_rows(dq, tq_pad), _pad_rows(iq, tq_pad), _pad_rows(iw, tq_pad), ik2, k2, dv_t,
                    tq=256 if tq_pad % 256 == 0 else LANES, tk=tk, q_start=q_start, s_valid=s_valid)[:, :t_len]

    y = _ffn(x2, row(wts['norm_ffn'][1]), wts['ffn_w_in'][1], wts['ffn_w_out'][1],
             attn=(o_c.reshape(n, 512), o_d.reshape(n, 512), wts['cd_w_out']), g_final=row(wts['norm_final']))
    rows = (ck.reshape(1, bsz, t_len, C_HEADS, C_HD), cv.reshape(1, bsz, t_len, C_HEADS, C_HD),
            logf[None], dk.reshape(1, bsz, t_len, D_KV_HEADS, D_HD), dv.reshape(1, bsz, t_len, D_KV_HEADS, D_HD),
            misc[None, :, :, :IDX_DIM])
    return (y.reshape(bsz, t_len, D_MODEL), sa_new[None], sb_new.reshape(1, bsz, B_HEADS, B_DK, B_DV)) + rows


def kernel(x_prompt, x_sample, state_hgrn, state_gla, cache_fox_k, cache_fox_v, cache_fox_logf, cache_dsa_k, cache_dsa_v, cache_dsa_ik, norm_mix, norm_ffn, norm_final, ab_w_in, ab_w_out, hgrn_lb_logits, hgrn_gnorm, gla_w_gk, gla_b_gk, gla_gnorm, cd_w_in, cd_w_out, fox_f_bias, ffn_w_in, ffn_w_out):
    lbs = jnp.cumsum(jax.nn.softmax(hgrn_lb_logits.astype(F32), axis=0), axis=0)
    w_ab, w_gk = _prep_ab(ab_w_in[0], gla_w_gk[0])
    w_cd, fb = _prep_cd(cd_w_in[0], fox_f_bias[0])
    wts = dict(norm_mix=norm_mix, norm_ffn=norm_ffn, norm_final=norm_final, ab_w_in=w_ab,
               ab_w_out=ab_w_out[0].astype(BF16), lb=lbs[0], hgrn_gnorm=hgrn_gnorm[0], gla_w_gk=w_gk,
               gla_b_gk=gla_b_gk[0], gla_gnorm=gla_gnorm[0], cd_w_in=w_cd, cd_w_out=cd_w_out[0].astype(BF16),
               fox_f_bias=fb, ffn_w_in=ffn_w_in.astype(BF16), ffn_w_out=ffn_w_out.astype(BF16))
    bp = x_prompt.shape[0]
    p_out = _trunk(x_prompt, jnp.zeros((bp, A_HEADS, A_DK, A_DV), F32), jnp.zeros((bp, B_HEADS, B_DK, B_DV), F32),
                   None, wts)
    cache = (cache_fox_k[0], cache_fox_v[0], cache_fox_logf[0], cache_dsa_k[0], cache_dsa_v[0], cache_dsa_ik[0])
    s_out = _trunk(x_sample, state_hgrn[0], state_gla[0], cache, wts)
    return (p_out[0], s_out[0]) + tuple(p_out[1:]) + tuple(s_out[1:])
```

```python
import functools

import numpy as np
import jax
import jax.numpy as jnp
from jax import lax
from jax.experimental import pallas as pl
from jax.experimental.pallas import tpu as pltpu

F32 = jnp.float32
BF16 = jnp.bfloat16
I32 = jnp.int32

D_MODEL = 1024
CHUNK = 64
A_HEADS, A_DK, A_DV = 4, 128, 128
B_HEADS, B_DK, B_DV = 4, 64, 128
B_GATE_RANK = 16
B_GATE_NORM = 16.0
C_HEADS, C_HD = 8, 64
D_HEADS, D_KV_HEADS, D_HD = 8, 2, 64
IDX_HEADS, IDX_DIM = 4, 64
IDX_TOPK_MAX = 256
IDX_SCALE = (IDX_DIM ** -0.5) * (IDX_HEADS ** -0.5)
FFN_HIDDEN = ((8 * D_MODEL // 3 + 255) // 256) * 256

LANES = 128
RSUB = 16
REC_CHUNK = 128
VMEM_LIMIT = 56 * 1024 * 1024
NEG = -1e30
MASKED = -2e30
LOG2E = 1.4426950408889634
ONES_ROWS = 16
INT_MIN = -2 ** 31

AB_COLS = 4 * 512 + 256 + 256 + 512 + 512 + LANES
CD_COLS = 3 * 512 + 8 * LANES + 2 * LANES + 256 + LANES + LANES


def _dot(a, b):
    return jnp.dot(a, b, preferred_element_type=F32)


def _dot_nt(a, b):
    return lax.dot_general(a, b, (((1,), (1,)), ((), ())), preferred_element_type=F32)


def _rms(x, g, eps=1e-6):
    return x * lax.rsqrt(jnp.mean(x * x, axis=-1, keepdims=True) + eps) * g


def _silu(x):
    return x * jax.nn.sigmoid(x)


def _log_sigmoid(x):
    return jnp.minimum(x, 0.0) - jnp.log1p(jnp.exp(-jnp.abs(x)))


def _split3(x):
    hi = x.astype(BF16)
    r = x - hi.astype(F32)
    mid = r.astype(BF16)
    lo = (r - mid.astype(F32)).astype(BF16)
    return hi, mid, lo


def _tri_dot(tri, x):
    hi, mid, lo = _split3(x)
    return _dot(tri, hi) + _dot(tri, mid) + _dot(tri, lo)


def _iota(shape, dim):
    return lax.broadcasted_iota(I32, shape, dim)


def _const_spec(shape):
    zeros = (0,) * len(shape)
    return pl.BlockSpec(shape, lambda *_: zeros, pipeline_mode=pl.Buffered(1))


def _params(sem):
    return pltpu.CompilerParams(dimension_semantics=sem, vmem_limit_bytes=VMEM_LIMIT)


def _ab_kernel(x_ref, sa_ref, sb_ref, g_ref, win_ref, wgk_ref, bgk_ref, lb_ref, agn_ref, bgn_ref, wout_ref,
               xo_ref, sao_ref, sbo_ref, s_sc, h_sc, o_sc, *, t_valid, t_pad):
    C = REC_CHUNK
    tm = x_ref.shape[1]
    t = pl.program_id(1)

    @pl.when(t == 0)
    def _():
        s_sc[0:4] = sa_ref[0]
        s_sc[4:6] = sb_ref[0]

    h_sc[...] = _dot(_rms(x_ref[0], g_ref[...]).astype(BF16), win_ref[...])

    def chunk_body(c, carry):
        r0 = pl.multiple_of(c * C, C)
        _ab_chunk(r0, t * tm + r0, h_sc, o_sc, s_sc, wgk_ref, bgk_ref, lb_ref, agn_ref, bgn_ref,
                  t_valid=t_valid, t_pad=t_pad)
        return carry

    lax.fori_loop(0, tm // C, chunk_body, 0)
    xo_ref[0] = x_ref[0] + _dot(o_sc[...], wout_ref[...])

    @pl.when(t == pl.num_programs(1) - 1)
    def _():
        sao_ref[0] = s_sc[0:4]
        sbo_ref[0] = s_sc[4:6]


def _ab_chunk(r0, row0, h_sc, o_sc, s_sc, wgk_ref, bgk_ref, lb_ref, agn_ref, bgn_ref, *, t_valid, t_pad):
    C = REC_CHUNK
    cols = lambda a, b: h_sc[pl.ds(r0, C), a:b]
    a_q, a_f, a_i, a_g = cols(0, 512), cols(512, 1024), cols(1024, 1536), cols(1536, 2048)
    b_q, b_k, b_v, b_g = cols(2048, 2304), cols(2304, 2560), cols(2560, 3072), cols(3072, 3584)
    b_lr = cols(3584, 3712)

    lb = lb_ref[...]
    f = lb + (1.0 - lb) * jax.nn.sigmoid(a_f)
    gk = _dot(b_lr.astype(BF16), wgk_ref[...]) + bgk_ref[...]
    la = jnp.concatenate([jnp.log2(f), _log_sigmoid(gk) * (LOG2E / B_GATE_NORM)], axis=1)
    q = jnp.concatenate([_silu(a_q), b_q * (B_DK ** -0.5)], axis=1)
    k = jnp.concatenate([1.0 - f, b_k], axis=1)
    v_a, v_b = a_i, b_v
    if t_valid < t_pad:
        ok = (row0 + _iota((C, 1), 0)) < t_valid
        la = jnp.where(ok, la, 0.0)
        k = jnp.where(ok, k, 0.0)
        v_a = jnp.where(ok, v_a, 0.0)
        v_b = jnp.where(ok, v_b, 0.0)

    row = _iota((C, C), 0)
    col = _iota((C, C), 1)
    causal = col <= row
    tri = jnp.where(causal, 1.0, 0.0).astype(BF16)
    tri_in = jnp.where(causal & ((row >> 4) == (col >> 4)), 1.0, 0.0).astype(BF16)
    bc = _tri_dot(tri, la)
    b_in = _tri_dot(tri_in, la)
    lane = _iota((1, LANES), 1)
    lo_half = lane < 64
    srow = _iota((LANES, 1), 0) < 64

    o_heads = [None] * 8
    for u in range(6):
        sl = slice(u * LANES, (u + 1) * LANES)
        qu, ku, bcu = q[:, sl], k[:, sl], bc[:, sl]
        s_old = s_sc[u]
        bend = bcu[C - 1:C, :]
        qt = qu * jnp.exp2(b_in[:, sl])
        qdec = qu * jnp.exp2(bcu)
        if u < 4:
            heads = [(u, None, v_a[:, sl])]
        else:
            ha = 4 + 2 * (u - 4)
            heads = [(ha, lo_half, v_b[:, (ha - 4) * LANES:(ha - 3) * LANES]),
                     (ha + 1, jnp.logical_not(lo_half), v_b[:, (ha - 3) * LANES:(ha - 2) * LANES])]
        a_rows = [[] for _ in heads]
        for i in range(C // RSUB):
            n = RSUB * (i + 1)
            if i == 0:
                kt = ku[0:n] * jnp.exp2(-bcu[0:n])
            else:
                kt = ku[0:n] * jnp.exp2(bcu[RSUB * i - 1:RSUB * i, :] - bcu[0:n])
            if n < C:
                kt = jnp.concatenate([kt, jnp.zeros((C - n, LANES), F32)], axis=0)
            ktb = kt.astype(BF16)
            qi = qt[RSUB * i:RSUB * (i + 1)]
            for hi_, (_, msk, _) in enumerate(heads):
                qim = qi if msk is None else jnp.where(msk, qi, 0.0)
                a_rows[hi_].append(_dot_nt(qim.astype(BF16), ktb))
        sb16 = s_old.astype(BF16)
        for hi_, (hd, msk, vh) in enumerate(heads):
            att = jnp.where(causal, jnp.concatenate(a_rows[hi_], axis=0), 0.0)
            qd = qdec if msk is None else jnp.where(msk, qdec, 0.0)
            o_heads[hd] = _dot(qd.astype(BF16), sb16) + _dot(att.astype(BF16), vh.astype(BF16))
        kht = (ku * jnp.exp2(bend - bcu)).T
        dcol = jnp.broadcast_to(jnp.exp2(bend), (LANES, LANES)).T
        if u < 4:
            upd = _dot(kht.astype(BF16), heads[0][2].astype(BF16))
        else:
            lhs = jnp.concatenate([jnp.where(srow, kht, 0.0), jnp.where(srow, 0.0, kht)], axis=1)
            rhs = jnp.concatenate([heads[0][2], heads[1][2]], axis=0)
            upd = _dot(lhs.astype(BF16), rhs.astype(BF16))
        s_sc[u] = dcol * s_old + upd

    outs = []
    for hd in range(8):
        if hd < 4:
            gn, gate = agn_ref[...], a_g[:, hd * LANES:(hd + 1) * LANES]
        else:
            gn, gate = bgn_ref[...], b_g[:, (hd - 4) * LANES:(hd - 3) * LANES]
        outs.append(_rms(o_heads[hd], gn) * _silu(gate))
    o_sc[pl.ds(r0, C), :] = jnp.concatenate(outs, axis=1).astype(BF16)


def _ab_layer(x, s_a, s_b, g, w_in, w_gk, b_gk, lb, a_gn, b_gn, w_out, t_valid):
    bsz, t_pad, _ = x.shape
    tm = 512 if t_pad % 512 == 0 else REC_CHUNK
    kern = functools.partial(_ab_kernel, t_valid=t_valid, t_pad=t_pad)
    return pl.pallas_call(
        kern,
        grid=(bsz, t_pad // tm),
        in_specs=[
            pl.BlockSpec((1, tm, D_MODEL), lambda b, t: (b, t, 0)),
            pl.BlockSpec((1, 4, 128, 128), lambda b, t: (b, 0, 0, 0)),
            pl.BlockSpec((1, 2, 128, 128), lambda b, t: (b, 0, 0, 0)),
            _const_spec((1, D_MODEL)),
            _const_spec((D_MODEL, AB_COLS)),
            _const_spec((LANES, 256)),
            _const_spec((1, 256)),
            _const_spec((1, 512)),
            _const_spec((1, 128)),
            _const_spec((1, 128)),
            _const_spec((D_MODEL, D_MODEL)),
        ],
        out_specs=[
            pl.BlockSpec((1, tm, D_MODEL), lambda b, t: (b, t, 0)),
            pl.BlockSpec((1, 4, 128, 128), lambda b, t: (b, 0, 0, 0)),
            pl.BlockSpec((1, 2, 128, 128), lambda b, t: (b, 0, 0, 0)),
        ],
        out_shape=[
            jax.ShapeDtypeStruct((bsz, t_pad, D_MODEL), F32),
            jax.ShapeDtypeStruct((bsz, 4, 128, 128), F32),
            jax.ShapeDtypeStruct((bsz, 2, 128, 128), F32),
        ],
        scratch_shapes=[pltpu.VMEM((6, 128, 128), F32), pltpu.VMEM((tm, AB_COLS), F32),
                        pltpu.VMEM((tm, D_MODEL), BF16)],
        compiler_params=_params(("parallel", "arbitrary")),
        name="ab_layer",
    )(x, s_a, s_b, g, w_in, w_gk, b_gk, lb, a_gn, b_gn, w_out)


FFN_TILE = FFN_HIDDEN // 2


def _ffn_kernel(*refs, has_attn, has_final):
    refs = list(refs)
    x_ref = refs.pop(0)
    x = x_ref[...]
    if has_attn:
        oc_ref, od_ref, wo_ref = refs.pop(0), refs.pop(0), refs.pop(0)
        x = x + _dot(jnp.concatenate([oc_ref[...], od_ref[...]], axis=1), wo_ref[...])
    g_ref, win_ref, wout_ref = refs.pop(0), refs.pop(0), refs.pop(0)
    gf_ref = refs.pop(0) if has_final else None
    out_ref = refs.pop(0)
    xn = _rms(x, g_ref[...]).astype(BF16)
    acc = x
    for j in range(FFN_HIDDEN // FFN_TILE):
        gate = _dot(xn, win_ref[:, j * FFN_TILE:(j + 1) * FFN_TILE])
        up = _dot(xn, win_ref[:, FFN_HIDDEN + j * FFN_TILE:FFN_HIDDEN + (j + 1) * FFN_TILE])
        act = (_silu(gate) * up).astype(BF16)
        acc = acc + _dot(act, wout_ref[j * FFN_TILE:(j + 1) * FFN_TILE, :])
    if has_final:
        acc = _rms(acc, gf_ref[...])
    out_ref[...] = acc


def _ffn(x, g, w_in, w_out, attn=None, g_final=None, tm=512):
    n = x.shape[0]
    tm = min(tm, n)
    row = lambda i: (i, 0)
    args = [x]
    specs = [pl.BlockSpec((tm, D_MODEL), row)]
    if attn is not None:
        oc, od, wo = attn
        args += [oc, od, wo]
        specs += [pl.BlockSpec((tm, 512), row), pl.BlockSpec((tm, 512), row), _const_spec((D_MODEL, D_MODEL))]
    args += [g, w_in, w_out]
    specs += [_const_spec((1, D_MODEL)), _const_spec((D_MODEL, 2 * FFN_HIDDEN)), _const_spec((FFN_HIDDEN, D_MODEL))]
    if g_final is not None:
        args.append(g_final)
        specs.append(_const_spec((1, D_MODEL)))
    kern = functools.partial(_ffn_kernel, has_attn=attn is not None, has_final=g_final is not None)
    return pl.pallas_call(
        kern,
        grid=(n // tm,),
        in_specs=specs,
        out_specs=pl.BlockSpec((tm, D_MODEL), row),
        out_shape=jax.ShapeDtypeStruct((n, D_MODEL), F32),
        compiler_params=_params(("parallel",)),
        name="ffn",
    )(*args)


def _cd_proj_kernel(x_ref, g_ref, w_ref, fb_ref, fq_ref, ck_ref, cv_ref, dq_ref, dk_ref, dv_ref, iq_ref,
                    misc_ref, iw_ref):
    xn = _rms(x_ref[...], g_ref[...]).astype(BF16)
    h = _dot(xn, w_ref[...])
    fq_ref[...] = (h[:, 0:512] * (C_HD ** -0.5 * LOG2E)).astype(BF16)
    ck_ref[...] = h[:, 512:1024]
    cv_ref[...] = h[:, 1024:1536]
    dq_ref[...] = (h[:, 1536:2560] * (D_HD ** -0.5 * LOG2E)).astype(BF16)
    dk_ref[...] = h[:, 2560:2688]
    dv_ref[...] = h[:, 2688:2816]
    iq_ref[...] = h[:, 2816:3072].astype(BF16)
    misc = h[:, 3072:3200]
    lane = _iota((1, LANES), 1)
    is_f = (lane >= IDX_DIM) & (lane < IDX_DIM + C_HEADS)
    misc_ref[...] = jnp.where(is_f, _log_sigmoid(misc + fb_ref[...]), misc)
    iw_ref[...] = h[:, 3200:3328]


def _cd_proj(x, g, w, fb, tm=512):
    n = x.shape[0]
    tm = min(tm, n)
    row = lambda i: (i, 0)
    widths = [(512, BF16), (512, F32), (512, F32), (1024, BF16), (128, F32), (128, F32), (256, BF16),
              (128, F32), (128, F32)]
    return pl.pallas_call(
        _cd_proj_kernel,
        grid=(n // tm,),
        in_specs=[pl.BlockSpec((tm, D_MODEL), row), _const_spec((1, D_MODEL)), _const_spec((D_MODEL, CD_COLS)),
                  _const_spec((1, LANES))],
        out_specs=[pl.BlockSpec((tm, w_), row) for w_, _ in widths],
        out_shape=[jax.ShapeDtypeStruct((n, w_), dt) for w_, dt in widths],
        compiler_params=_params(("parallel",)),
        name="cd_proj",
    )(x, g, w, fb)


def _aug_consts(is_query):
    p = np.zeros((3, LANES, 8 * LANES), np.float32)
    ones = np.zeros((1, 8 * LANES), np.float32)
    for h in range(8):
        off = h * LANES + (64 if h % 2 == 0 else 0)
        for c in range(3):
            if is_query:
                p[c, h, off + c] = 1.0
                ones[0, off + 3 + c] = 1.0
            else:
                p[c, h, off + 3 + c] = -1.0
                ones[0, off + c] = 1.0
    return jnp.asarray(p, BF16), jnp.asarray(ones, F32)


def _aug_lanes(cum, p_ref, ones_ref):
    hi, mid, lo = _split3(cum * LOG2E)
    return _dot(hi, p_ref[0]) + _dot(mid, p_ref[1]) + _dot(lo, p_ref[2]) + ones_ref[...]


def _running_sum(lf, carry):
    tm = lf.shape[0]

    @pl.when(pl.program_id(1) == 0)
    def _():
        carry[...] = jnp.zeros_like(carry)

    tri = jnp.where(_iota((tm, tm), 1) <= _iota((tm, tm), 0), 1.0, 0.0).astype(BF16)
    cum = _tri_dot(tri, lf) + carry[...]
    carry[...] = cum[tm - 1:tm, :]
    return cum


def _fox_rows(kk, vv, cum, p_ref, ones_ref, ka_ref, vt_ref):
    aug = _aug_lanes(cum, p_ref, ones_ref)
    lane = _iota((1, LANES), 1)
    for h in range(8):
        pr = slice((h // 2) * LANES, (h // 2 + 1) * LANES)
        own = (lane < 64) if h % 2 == 0 else (lane >= 64)
        ka_ref[0, h] = jnp.where(own, kk[:, pr], aug[:, h * LANES:(h + 1) * LANES]).astype(BF16)
    for pr in range(4):
        vt_ref[0, pr, 0] = vv[:, pr * LANES:(pr + 1) * LANES].T.astype(BF16)


def _fox_pack_kernel(k_ref, v_ref, lf_ref, p_ref, ones_ref, ka_ref, vt_ref, cum_ref, carry):
    cum = _running_sum(lf_ref[0], carry)
    cum_ref[0] = cum
    _fox_rows(k_ref[0], v_ref[0], cum, p_ref, ones_ref, ka_ref, vt_ref)


def _fox_pack(k, v, lf, tm):
    bsz, s, _ = k.shape
    p, ones = _aug_consts(False)
    return pl.pallas_call(
        _fox_pack_kernel,
        grid=(bsz, s // tm),
        in_specs=[pl.BlockSpec((1, tm, 512), lambda b, t: (b, t, 0)),
                  pl.BlockSpec((1, tm, 512), lambda b, t: (b, t, 0)),
                  pl.BlockSpec((1, tm, LANES), lambda b, t: (b, t, 0)),
                  _const_spec((3, LANES, 8 * LANES)), _const_spec((1, 8 * LANES))],
        out_specs=[pl.BlockSpec((1, 8, tm, LANES), lambda b, t: (b, 0, t, 0)),
                   pl.BlockSpec((1, 4, 1, LANES, tm), lambda b, t: (b, 0, t, 0, 0)),
                   pl.BlockSpec((1, tm, LANES), lambda b, t: (b, t, 0))],
        out_shape=[jax.ShapeDtypeStruct((bsz, 8, s, LANES), BF16),
                   jax.ShapeDtypeStruct((bsz, 4, s // tm, LANES, tm), BF16),
                   jax.ShapeDtypeStruct((bsz, s, LANES), F32)],
        scratch_shapes=[pltpu.VMEM((1, LANES), F32)],
        compiler_params=_params(("parallel", "arbitrary")),
        name="fox_pack",
    )(k, v, lf, p, ones)


def _q_rows(qq, cum, p_ref, ones_ref, qa_ref):
    aug = _aug_lanes(cum, p_ref, ones_ref)
    lane = _iota((1, LANES), 1)
    for h in range(8):
        pr = slice((h // 2) * LANES, (h // 2 + 1) * LANES)
        own = (lane < 64) if h % 2 == 0 else (lane >= 64)
        qa_ref[0, h] = jnp.where(own, qq[:, pr], aug[:, h * LANES:(h + 1) * LANES]).astype(BF16)


def _q_pack_kernel(q_ref, cum_ref, p_ref, ones_ref, qa_ref):
    _q_rows(q_ref[0].astype(F32), cum_ref[0], p_ref, ones_ref, qa_ref)


def _q_pack(q, cum_q, tm):
    bsz, tq, _ = q.shape
    p, ones = _aug_consts(True)
    return pl.pallas_call(
        _q_pack_kernel,
        grid=(bsz, tq // tm),
        in_specs=[pl.BlockSpec((1, tm, 512), lambda b, t: (b, t, 0)),
                  pl.BlockSpec((1, tm, LANES), lambda b, t: (b, t, 0)),
                  _const_spec((3, LANES, 8 * LANES)), _const_spec((1, 8 * LANES))],
        out_specs=pl.BlockSpec((1, 8, tm, LANES), lambda b, t: (b, 0, t, 0)),
        out_shape=jax.ShapeDtypeStruct((bsz, 8, tq, LANES), BF16),
        compiler_params=_params(("parallel", "parallel")),
        name="q_pack",
    )(q, cum_q, p, ones)


def _fox_attn_kernel(q_ref, k_ref, vt_ref, o_ref, s_sc, *, tq, tk, q_start):
    i = pl.program_id(2)
    q_lo = q_start + i * tq
    n_full = (q_lo + 1) // tk
    n_tiles = (q_lo + tq + tk - 1) // tk
    q_pos = q_lo + _iota((1, tq), 1)
    qs = (q_ref[0, 0], q_ref[0, 1])

    def logits(j):
        off = pl.multiple_of(j * tk, tk)
        return [_dot_nt(k_ref[0, hh, pl.ds(off, tk), :], qs[hh]) for hh in range(2)]

    s_sc[0], s_sc[1] = logits(0)

    def body(j, carry, masked):
        off = pl.multiple_of(j * tk, tk)
        s_next = logits(jnp.minimum(j + 1, n_tiles - 1))
        ss = [s_sc[0], s_sc[1]]
        if masked:
            ok = (off + _iota((tk, 1), 0)) <= q_pos
            ss = [jnp.where(ok, s, MASKED) for s in ss]
        out = []
        for hh in range(2):
            m, acc = carry[hh]
            m_new = jnp.maximum(m, jnp.max(ss[hh], axis=0, keepdims=True))
            p = jnp.exp2(ss[hh] - m_new).astype(BF16)
            vt = jnp.concatenate([vt_ref[0, 0, j, hh * 64:(hh + 1) * 64, :], ones], axis=0)
            acc = jnp.exp2(m - m_new) * acc + _dot(vt, p)
            out.append((m_new, acc))
        s_sc[0], s_sc[1] = s_next
        return tuple(out)

    ones = jnp.ones((ONES_ROWS, tk), BF16)
    init = (jnp.full((1, tq), NEG, F32), jnp.zeros((64 + ONES_ROWS, tq), F32))
    carry = lax.fori_loop(0, n_full, functools.partial(body, masked=False), (init, init))
    carry = lax.fori_loop(n_full, n_tiles, functools.partial(body, masked=True), carry)
    o_t = jnp.concatenate([acc[0:64] / acc[64:65] for _, acc in carry], axis=0)
    o_ref[0] = o_t.T.astype(BF16)


def _fox_attn(q_aug, k_aug, vt, tq, tk, q_start):
    bsz, _, t_q, _ = q_aug.shape
    s = k_aug.shape[2]
    kern = functools.partial(_fox_attn_kernel, tq=tq, tk=tk, q_start=q_start)
    return pl.pallas_call(
        kern,
        grid=(bsz, 4, t_q // tq),
        in_specs=[pl.BlockSpec((1, 2, tq, LANES), lambda b, p, i: (b, p, i, 0)),
                  pl.BlockSpec((1, 2, s, LANES), lambda b, p, i: (b, p, 0, 0)),
                  pl.BlockSpec((1, 1, s // tk, LANES, tk), lambda b, p, i: (b, p, 0, 0, 0))],
        out_specs=pl.BlockSpec((1, tq, LANES), lambda b, p, i: (b, i, p)),
        out_shape=jax.ShapeDtypeStruct((bsz, t_q, 512), BF16),
        scratch_shapes=[pltpu.VMEM((2, tk, tq), F32)],
        compiler_params=_params(("parallel", "parallel", "arbitrary")),
        name="fox_attn",
    )(q_aug, k_aug, vt)


def _dsa_rows(dk, dv, ik, k2_ref, vt_ref, ik2_ref):
    lo = _iota((1, LANES), 1) < 64
    k2_ref[0] = dk.astype(BF16)
    vt_ref[0, 0] = dv.T.astype(BF16)
    ik = jnp.where(lo, ik, 0.0)
    ik2_ref[0, 0] = ik.astype(BF16)
    ik2_ref[0, 1] = pltpu.roll(ik, 64, 1).astype(BF16)


def _dsa_pack_kernel(dk_ref, dv_ref, ik_ref, k2_ref, vt_ref, ik2_ref):
    _dsa_rows(dk_ref[0], dv_ref[0], ik_ref[0], k2_ref, vt_ref, ik2_ref)


def _dsa_pack(dk, dv, ik, tm):
    bsz, s, _ = dk.shape
    tok = pl.BlockSpec((1, tm, LANES), lambda b, t: (b, t, 0))
    return pl.pallas_call(
        _dsa_pack_kernel,
        grid=(bsz, s // tm),
        in_specs=[tok, tok, tok],
        out_specs=[tok,
                   pl.BlockSpec((1, 1, LANES, tm), lambda b, t: (b, t, 0, 0)),
                   pl.BlockSpec((1, 2, tm, LANES), lambda b, t: (b, 0, t, 0))],
        out_shape=[jax.ShapeDtypeStruct((bsz, s, LANES), BF16),
                   jax.ShapeDtypeStruct((bsz, s // tm, LANES, tm), BF16),
                   jax.ShapeDtypeStruct((bsz, 2, s, LANES), BF16)],
        compiler_params=_params(("parallel", "parallel")),
        name="dsa_pack",
    )(dk, dv, ik)


def _cd_fused_kernel(x_ref, g_ref, w_ref, fb_ref, pk_ref, onesk_ref, pq_ref, onesq_ref,
                     ck_ref, cv_ref, dk_ref, dv_ref, misc_ref, iw_ref, iq_ref, dq_ref,
                     qa_ref, ka_ref, vt_ref, k2_ref, dvt_ref, ik2_ref, carry):
    xn = _rms(x_ref[0], g_ref[...]).astype(BF16)
    h = _dot(xn, w_ref[...])
    ck, cv = h[:, 512:1024], h[:, 1024:1536]
    dk, dv = h[:, 2560:2688], h[:, 2688:2816]
    misc = h[:, 3072:3200]
    lane = _iota((1, LANES), 1)
    is_f = (lane >= IDX_DIM) & (lane < IDX_DIM + C_HEADS)
    logf = _log_sigmoid(misc + fb_ref[...])
    misc = jnp.where(is_f, logf, misc)
    ck_ref[0], cv_ref[0], dk_ref[0], dv_ref[0], misc_ref[0] = ck, cv, dk, dv, misc
    iw_ref[0] = h[:, 3200:3328]
    iq_ref[0] = h[:, 2816:3072].astype(BF16)
    dq_ref[0] = (h[:, 1536:2560] * (D_HD ** -0.5 * LOG2E)).astype(BF16)
    cum = _running_sum(pltpu.roll(jnp.where(is_f, logf, 0.0), LANES - IDX_DIM, 1), carry)
    _fox_rows(ck, cv, cum, pk_ref, onesk_ref, ka_ref, vt_ref)
    _q_rows(h[:, 0:512] * (C_HD ** -0.5 * LOG2E), cum, pq_ref, onesq_ref, qa_ref)
    _dsa_rows(dk, dv, misc, k2_ref, dvt_ref, ik2_ref)


def _cd_fused(x, g, w, fb, tm):
    bsz, t_len, _ = x.shape
    pk, onesk = _aug_consts(False)
    pq, onesq = _aug_consts(True)
    tok = lambda w_: pl.BlockSpec((1, tm, w_), lambda b, t: (b, t, 0))
    hm = pl.BlockSpec((1, 8, tm, LANES), lambda b, t: (b, 0, t, 0))
    sds = jax.ShapeDtypeStruct
    n_t = t_len // tm
    return pl.pallas_call(
        _cd_fused_kernel,
        grid=(bsz, n_t),
        in_specs=[tok(D_MODEL), _const_spec((1, D_MODEL)), _const_spec((D_MODEL, CD_COLS)), _const_spec((1, LANES)),
                  _const_spec((3, LANES, 8 * LANES)), _const_spec((1, 8 * LANES)),
                  _const_spec((3, LANES, 8 * LANES)), _const_spec((1, 8 * LANES))],
        out_specs=[tok(512), tok(512), tok(LANES), tok(LANES), tok(LANES), tok(LANES), tok(256), tok(8 * LANES),
                   hm, hm, pl.BlockSpec((1, 4, 1, LANES, tm), lambda b, t: (b, 0, t, 0, 0)),
                   tok(LANES), pl.BlockSpec((1, 1, LANES, tm), lambda b, t: (b, t, 0, 0)),
                   pl.BlockSpec((1, 2, tm, LANES), lambda b, t: (b, 0, t, 0))],
        out_shape=[sds((bsz, t_len, 512), F32), sds((bsz, t_len, 512), F32), sds((bsz, t_len, LANES), F32),
                   sds((bsz, t_len, LANES), F32), sds((bsz, t_len, LANES), F32), sds((bsz, t_len, LANES), F32),
                   sds((bsz, t_len, 256), BF16), sds((bsz, t_len, 8 * LANES), BF16),
                   sds((bsz, 8, t_len, LANES), BF16), sds((bsz, 8, t_len, LANES), BF16),
                   sds((bsz, 4, n_t, LANES, tm), BF16), sds((bsz, t_len, LANES), BF16),
                   sds((bsz, n_t, LANES, tm), BF16), sds((bsz, 2, t_len, LANES), BF16)],
        scratch_shapes=[pltpu.VMEM((1, LANES), F32)],
        compiler_params=_params(("parallel", "arbitrary")),
        name="cd_fused",
    )(x, g, w, fb, pk, onesk, pq, onesq)


def _dsa_attn_kernel(dq_ref, iq_ref, iw_ref, ik2_ref, k2_ref, vt_ref, o_ref, keys_sc, hi_sc, eq_sc, m_sc, mt_sc, acc_sc, s_sc,
                     *, tq, tk, q_start, s_valid, topk):
    i = pl.program_id(1)
    q_lo = q_start + i * tq
    adm_row = jnp.minimum(((q_lo + _iota((1, tq), 1)) // CHUNK + 1) * CHUNK, s_valid)
    adm_end = jnp.minimum(((q_lo + tq - 1) // CHUNK + 1) * CHUNK, s_valid)
    n_tiles = (adm_end + tk - 1) // tk
    iw_t = iw_ref[0].T
    iq = iq_ref[0]

    def score_body(j, c):
        off = pl.multiple_of(j * tk, tk)
        sc = jnp.zeros((tk, tq), F32)
        for hd in range(IDX_HEADS):
            qp = iq[:, (hd // 2) * LANES:(hd // 2 + 1) * LANES]
            r = jnp.maximum(_dot_nt(ik2_ref[0, hd % 2, pl.ds(off, tk), :], qp), 0.0)
            sc = sc + r * iw_t[hd:hd + 1, :]
        sc = sc * IDX_SCALE
        adm = (off + _iota((tk, tq), 0)) < adm_row
        bits = lax.bitcast_convert_type(sc, I32)
        bits = jnp.where(bits == INT_MIN, 0, bits)
        key = bits ^ ((bits >> 31) & 0x7FFFFFFF)
        keys_sc[j] = jnp.where(adm, key, INT_MIN)
        hi = lax.bitcast_convert_type(bits & -65536, F32)
        hi_sc[j] = jnp.where(adm, hi, -jnp.inf).astype(BF16)
        return c

    lax.fori_loop(0, n_tiles, score_body, 0)

    def pairs(one, init):
        part = lax.fori_loop(0, n_tiles // 2, lambda jj, c: one(2 * jj + 1, one(2 * jj, c)), init)
        return lax.fori_loop(2 * (n_tiles // 2), n_tiles, one, part)

    one16 = jnp.ones((tk, tq), BF16)
    zero16 = jnp.zeros((tk, tq), BF16)

    def hi_body(b, pre):
        cand = pre + lax.shift_left(jnp.int32(1), 15 - b)
        cbits = (cand ^ ((cand >> 15) & 0x7FFF)) & 0xFFFF
        subnormal = ((cbits & 0x7F80) == 0) & ((cbits & 0x007F) != 0)
        cbits = jnp.where(subnormal, jnp.where((cbits & 0x8000) != 0, 0x0000, 0x0080), cbits)
        cval = jnp.broadcast_to(lax.bitcast_convert_type(cbits << 16, F32).astype(BF16), (tk, tq))

        def one(j, c):
            hit = jnp.where(hi_sc[j] >= cval, one16, zero16).reshape(tk // 64, 4, 16, tq)
            for r in range(tk // 64):
                c = c + hit[r]
            return c
        part = pairs(one, jnp.zeros((4, 16, tq), BF16)).astype(F32)
        cnt = jnp.sum(jnp.sum(part, axis=0), axis=0, keepdims=True)
        return jnp.where(cnt >= topk, cand, pre)

    pre = lax.fori_loop(0, 16, hi_body, jnp.full((1, tq), -2 ** 15, I32))

    def count(pred):
        def one(j, c):
            hit = jnp.where(pred(j, keys_sc[j]), 1.0, 0.0)
            return c + jnp.sum(hit.reshape(tk // 32, 4, 8, tq), axis=0)

        part = pairs(one, jnp.zeros((4, 8, tq), F32))
        return jnp.sum(jnp.sum(part, axis=0), axis=0, keepdims=True)

    def tau_body(b, tau):
        cand = tau + lax.shift_left(jnp.int32(1), 31 - b)
        cnt = count(lambda j, kk: kk >= cand)
        return jnp.where(cnt >= topk, cand, tau)

    tau = lax.fori_loop(16, 32, tau_body, pre << 16)
    tau = jnp.where(adm_row < int(topk), INT_MIN, tau)
    need = jnp.where(tau == INT_MIN, -1.0, topk - count(lambda j, kk: kk > tau))
    eq_sc[...] = jnp.zeros(eq_sc.shape, F32)
    tri = jnp.where(_iota((tk, tk), 1) <= _iota((tk, tk), 0), 1.0, 0.0).astype(BF16)

    m_sc[...] = jnp.full(m_sc.shape, NEG, F32)
    acc_sc[...] = jnp.zeros(acc_sc.shape, F32)
    ones = jnp.ones((ONES_ROWS, tk), BF16)
    n_rep = D_HEADS // D_KV_HEADS
    q_stack = [jnp.concatenate([dq_ref[0, :, hd * LANES:(hd + 1) * LANES]
                                for hd in range(g * n_rep, (g + 1) * n_rep)], axis=0) for g in range(D_KV_HEADS)]

    def logits(j):
        kk = keys_sc[j]
        eq = kk == tau
        run = _dot(tri, jnp.where(eq, 1.0, 0.0).astype(BF16)) + eq_sc[...]
        eq_sc[...] = run[tk - 1:tk, :]
        sel = (kk > tau) | (eq & (run <= need))
        bias = jnp.where(sel, 0.0, MASKED)
        bias = jnp.concatenate([bias] * n_rep, axis=1)
        kt = k2_ref[0, pl.ds(pl.multiple_of(j * tk, tk), tk), :]
        ss = [_dot_nt(kt, q_stack[g]) + bias for g in range(D_KV_HEADS)]
        return ss, [jnp.max(s, axis=0, keepdims=True) for s in ss]

    def stage(slot, ss, mts):
        for g in range(D_KV_HEADS):
            s_sc[slot, g] = ss[g]
            mt_sc[slot, g] = mts[g]

    stage(0, *logits(0))

    def step(j, cur):
        stage(1 - cur, *logits(jnp.minimum(j + 1, n_tiles - 1)))
        for g in range(D_KV_HEADS):
            m_old = m_sc[g]
            m_new = jnp.maximum(m_old, mt_sc[cur, g])
            p = jnp.exp2(s_sc[cur, g] - m_new).astype(BF16)
            vt = jnp.concatenate([vt_ref[0, j, g * 64:(g + 1) * 64, :], ones], axis=0)
            acc_sc[g] = jnp.exp2(m_old - m_new) * acc_sc[g] + _dot(vt, p)
            m_sc[g] = m_new

    def attn_body(jj, c):
        step(2 * jj, 0)

        @pl.when(2 * jj + 1 < n_tiles)
        def _():
            step(2 * jj + 1, 1)
        return c

    lax.fori_loop(0, (n_tiles + 1) // 2, attn_body, 0)
    for g in range(D_KV_HEADS):
        o_g = acc_sc[g, 0:64] / acc_sc[g, 64:65]
        for pr in range(n_rep // 2):
            o_t = jnp.concatenate([o_g[:, (2 * pr) * tq:(2 * pr + 1) * tq],
                                   o_g[:, (2 * pr + 1) * tq:(2 * pr + 2) * tq]], axis=0)
            col = (g * n_rep // 2 + pr) * LANES
            o_ref[0, :, col:col + LANES] = o_t.T.astype(BF16)


def _dsa_attn(dq, iq, iw, ik2, k2, vt, tq, tk, q_start, s_valid):
    bsz, t_q, _ = dq.shape
    s = k2.shape[1]
    topk = min(IDX_TOPK_MAX, s_valid // 4)
    kern = functools.partial(_dsa_attn_kernel, tq=tq, tk=tk, q_start=q_start, s_valid=s_valid, topk=float(topk))
    return pl.pallas_call(
        kern,
        grid=(bsz, t_q // tq),
        in_specs=[pl.BlockSpec((1, tq, 8 * LANES), lambda b, i: (b, i, 0)),
                  pl.BlockSpec((1, tq, 256), lambda b, i: (b, i, 0)),
                  pl.BlockSpec((1, tq, LANES), lambda b, i: (b, i, 0)),
                  pl.BlockSpec((1, 2, s, LANES), lambda b, i: (b, 0, 0, 0)),
                  pl.BlockSpec((1, s, LANES), lambda b, i: (b, 0, 0)),
                  pl.BlockSpec((1, s // tk, LANES, tk), lambda b, i: (b, 0, 0, 0))],
        out_specs=pl.BlockSpec((1, tq, 512), lambda b, i: (b, i, 0)),
        out_shape=jax.ShapeDtypeStruct((bsz, t_q, 512), BF16),
        scratch_shapes=[pltpu.VMEM((s // tk, tk, tq), I32),
                        pltpu.VMEM((s // tk, tk, tq), BF16),
                        pltpu.VMEM((1, tq), F32),
                        pltpu.VMEM((D_KV_HEADS, 1, D_HEADS // D_KV_HEADS * tq), F32),
                        pltpu.VMEM((2, D_KV_HEADS, 1, D_HEADS // D_KV_HEADS * tq), F32),
                        pltpu.VMEM((D_KV_HEADS, 64 + ONES_ROWS, D_HEADS // D_KV_HEADS * tq), F32),
                        pltpu.VMEM((2, D_KV_HEADS, tk, D_HEADS // D_KV_HEADS * tq), F32)],
        compiler_params=_params(("parallel", "arbitrary")),
        name="dsa_attn",
    )(dq, iq, iw, ik2, k2, vt)


def _prep_ab(w_in, w_gk):
    w = jnp.pad(w_in, ((0, 0), (0, AB_COLS - w_in.shape[1]))).astype(BF16)
    wgk = jnp.pad(w_gk, ((0, LANES - B_GATE_RANK), (0, 0))).astype(BF16)
    return w, wgk


def _prep_cd(w_in, f_bias):
    o = np.cumsum([0, 512, 512, 512, C_HEADS, 512, 128, 128, 256, IDX_DIM, IDX_HEADS])
    c_q, c_k, c_v, c_f, d_q, d_k, d_v, d_iq, d_ik, d_iw = (w_in[:, o[i]:o[i + 1]] for i in range(10))
    zeros = lambda n: jnp.zeros((D_MODEL, n), w_in.dtype)
    dq_cols = []
    for h in range(D_HEADS):
        g = h // (D_HEADS // D_KV_HEADS)
        wh = d_q[:, h * 64:(h + 1) * 64]
        dq_cols += [wh, zeros(64)] if g == 0 else [zeros(64), wh]
    w = jnp.concatenate([c_q, c_k, c_v] + dq_cols + [d_k, d_v, d_iq, d_ik, c_f, zeros(LANES - IDX_DIM - C_HEADS),
                                                     d_iw, zeros(LANES - IDX_HEADS)], axis=1).astype(BF16)
    fb = jnp.pad(f_bias.astype(F32), (IDX_DIM, LANES - IDX_DIM - C_HEADS)).reshape(1, LANES)
    return w, fb


def _pad_rows(z, s):
    return jnp.pad(z, ((0, 0), (0, s - z.shape[1]), (0, 0)))


def _pad_lanes(z, n=LANES):
    return jnp.pad(z, ((0, 0), (0, 0), (0, n - z.shape[2])))


def _trunk(x, s_a, s_b, cache, wts):
    bsz, t_len, _ = x.shape
    row = lambda z: z.reshape(1, -1).astype(F32)

    t_pad = -(-t_len // REC_CHUNK) * REC_CHUNK
    x1, sa_new, sb_new = _ab_layer(
        _pad_rows(x, t_pad), s_a, s_b.reshape(bsz, 2, 128, 128), row(wts['norm_mix'][0]), wts['ab_w_in'],
        wts['gla_w_gk'], row(wts['gla_b_gk']), row(wts['lb']), row(wts['hgrn_gnorm']), row(wts['gla_gnorm']),
        wts['ab_w_out'], t_len)
    n = bsz * t_len
    x1 = x1[:, :t_len].reshape(n, D_MODEL)
    x2 = _ffn(x1, row(wts['norm_ffn'][0]), wts['ffn_w_in'][0], wts['ffn_w_out'][0])

    tk = 512
    q_start = 0 if cache is None else cache[0].shape[1]
    s_valid = q_start + t_len
    tq_pad = -(-t_len // LANES) * LANES
    if cache is None and t_len % tk == 0:
        (ck, cv, dk, dv, misc, iw, iq, dq, q_aug, k_aug, v_t, k2, dv_t, ik2) = _cd_fused(
            x2.reshape(bsz, t_len, D_MODEL), row(wts['norm_mix'][1]), wts['cd_w_in'], wts['fox_f_bias'], tm=tk)
        logf = misc[:, :, IDX_DIM:IDX_DIM + C_HEADS]
    else:
        fq, ck, cv, dq, dk, dv, iq, misc, iw = _cd_proj(x2, row(wts['norm_mix'][1]), wts['cd_w_in'],
                                                        wts['fox_f_bias'])
        per_b = lambda z: z.reshape(bsz, t_len, z.shape[-1])
        fq, ck, cv, dq, dk, dv, iq, misc, iw = map(per_b, (fq, ck, cv, dq, dk, dv, iq, misc, iw))
        logf = misc[:, :, IDX_DIM:IDX_DIM + C_HEADS]
        if cache is None:
            k_all, v_all, lf_all, dk_all, dv_all, ik_all = ck, cv, _pad_lanes(logf), dk, dv, misc
        else:
            c_k, c_v, c_lf, c_dk, c_dv, c_ik = cache
            cat = lambda c, r: jnp.concatenate([c.reshape(bsz, q_start, -1).astype(F32), r], axis=1)
            k_all, v_all, dk_all, dv_all = cat(c_k, ck), cat(c_v, cv), cat(c_dk, dk), cat(c_dv, dv)
            lf_all = _pad_lanes(cat(c_lf, logf))
            ik_all = cat(_pad_lanes(c_ik), misc)
        s_pad = -(-s_valid // tk) * tk
        k_all, v_all, lf_all, dk_all, dv_all, ik_all = (_pad_rows(z, s_pad) for z in
                                                        (k_all, v_all, lf_all, dk_all, dv_all, ik_all))
        k_aug, v_t, cum = _fox_pack(k_all, v_all, lf_all, tm=tk)
        q_aug = _q_pack(_pad_rows(fq, tq_pad), _pad_rows(cum[:, q_start:q_start + t_len], tq_pad),
                        tm=512 if tq_pad % 512 == 0 else LANES)
        k2, dv_t, ik2 = _dsa_pack(dk_all, dv_all, ik_all, tm=tk)
    o_c = _fox_attn(q_aug, k_aug, v_t, tq=512 if tq_pad % 512 == 0 else LANES, tk=tk, q_start=q_start)[:, :t_len]
    o_d = _dsa_attn(_pad_rows(dq, tq_pad), _pad_rows(iq, tq_pad), _pad_rows(iw, tq_pad), ik2, k2, dv_t,
                    tq=256 if tq_pad % 256 == 0 else LANES, tk=tk, q_start=q_start, s_valid=s_valid)[:, :t_len]

    y = _ffn(x2, row(wts['norm_ffn'][1]), wts['ffn_w_in'][1], wts['ffn_w_out'][1],
             attn=(o_c.reshape(n, 512), o_d.reshape(n, 512), wts['cd_w_out']), g_final=row(wts['norm_final']))
    rows = (ck.reshape(1, bsz, t_len, C_HEADS, C_HD), cv.reshape(1, bsz, t_len, C_HEADS, C_HD),
            logf[None], dk.reshape(1, bsz, t_len, D_KV_HEADS, D_HD), dv.reshape(1, bsz, t_len, D_KV_HEADS, D_HD),
            misc[None, :, :, :IDX_DIM])
    return (y.reshape(bsz, t_len, D_MODEL), sa_new[None], sb_new.reshape(1, bsz, B_HEADS, B_DK, B_DV)) + rows


def kernel(x_prompt, x_sample, state_hgrn, state_gla, cache_fox_k, cache_fox_v, cache_fox_logf, cache_dsa_k, cache_dsa_v, cache_dsa_ik, norm_mix, norm_ffn, norm_final, ab_w_in, ab_w_out, hgrn_lb_logits, hgrn_gnorm, gla_w_gk, gla_b_gk, gla_gnorm, cd_w_in, cd_w_out, fox_f_bias, ffn_w_in, ffn_w_out):
    lbs = jnp.cumsum(jax.nn.softmax(hgrn_lb_logits.astype(F32), axis=0), axis=0)
    w_ab, w_gk = _prep_ab(ab_w_in[0], gla_w_gk[0])
    w_cd, fb = _prep_cd(cd_w_in[0], fox_f_bias[0])
    wts = dict(norm_mix=norm_mix, norm_ffn=norm_ffn, norm_final=norm_final, ab_w_in=w_ab,
               ab_w_out=ab_w_out[0].astype(BF16), lb=lbs[0], hgrn_gnorm=hgrn_gnorm[0], gla_w_gk=w_gk,
               gla_b_gk=gla_b_gk[0], gla_gnorm=gla_gnorm[0], cd_w_in=w_cd, cd_w_out=cd_w_out[0].astype(BF16),
               fox_f_bias=fb, ffn_w_in=ffn_w_in.astype(BF16), ffn_w_out=ffn_w_out.astype(BF16))
    bp = x_prompt.shape[0]
    p_out = _trunk(x_prompt, jnp.zeros((bp, A_HEADS, A_DK, A_DV), F32), jnp.zeros((bp, B_HEADS, B_DK, B_DV), F32),
                   None, wts)
    cache = (cache_fox_k[0], cache_fox_v[0], cache_fox_logf[0], cache_dsa_k[0], cache_dsa_v[0], cache_dsa_ik[0])
    s_out = _trunk(x_sample, state_hgrn[0], state_gla[0], cache, wts)
    return (p_out[0], s_out[0]) + tuple(p_out[1:]) + tuple(s_out[1:])
```

```python
import functools

import numpy as np
import jax
import jax.numpy as jnp
from jax import lax
from jax.experimental import pallas as pl
from jax.experimental.pallas import tpu as pltpu

F32 = jnp.float32
BF16 = jnp.bfloat16
I32 = jnp.int32

D_MODEL = 1024
CHUNK = 64
A_HEADS, A_DK, A_DV = 4, 128, 128
B_HEADS, B_DK, B_DV = 4, 64, 128
B_GATE_RANK = 16
B_GATE_NORM = 16.0
C_HEADS, C_HD = 8, 64
D_HEADS, D_KV_HEADS, D_HD = 8, 2, 64
IDX_HEADS, IDX_DIM = 4, 64
IDX_TOPK_MAX = 256
IDX_SCALE = (IDX_DIM ** -0.5) * (IDX_HEADS ** -0.5)
FFN_HIDDEN = ((8 * D_MODEL // 3 + 255) // 256) * 256

LANES = 128
RSUB = 16
REC_CHUNK = 128
VMEM_LIMIT = 56 * 1024 * 1024
NEG = -1e30
MASKED = -2e30
LOG2E = 1.4426950408889634
ONES_ROWS = 16
INT_MIN = -2 ** 31

AB_COLS = 4 * 512 + 256 + 256 + 512 + 512 + LANES
CD_COLS = 3 * 512 + 8 * LANES + 2 * LANES + 256 + LANES + LANES


def _dot(a, b):
    return jnp.dot(a, b, preferred_element_type=F32)


def _dot_nt(a, b):
    return lax.dot_general(a, b, (((1,), (1,)), ((), ())), preferred_element_type=F32)


def _rms(x, g, eps=1e-6):
    return x * lax.rsqrt(jnp.mean(x * x, axis=-1, keepdims=True) + eps) * g


def _silu(x):
    return x * jax.nn.sigmoid(x)


def _log_sigmoid(x):
    return jnp.minimum(x, 0.0) - jnp.log1p(jnp.exp(-jnp.abs(x)))


def _split3(x):
    hi = x.astype(BF16)
    r = x - hi.astype(F32)
    mid = r.astype(BF16)
    lo = (r - mid.astype(F32)).astype(BF16)
    return hi, mid, lo


def _tri_dot(tri, x):
    hi, mid, lo = _split3(x)
    return _dot(tri, hi) + _dot(tri, mid) + _dot(tri, lo)


def _iota(shape, dim):
    return lax.broadcasted_iota(I32, shape, dim)


def _const_spec(shape):
    zeros = (0,) * len(shape)
    return pl.BlockSpec(shape, lambda *_: zeros, pipeline_mode=pl.Buffered(1))


def _params(sem):
    return pltpu.CompilerParams(dimension_semantics=sem, vmem_limit_bytes=VMEM_LIMIT)


def _ab_kernel(x_ref, sa_ref, sb_ref, g_ref, win_ref, wgk_ref, bgk_ref, lb_ref, agn_ref, bgn_ref, wout_ref,
               xo_ref, sao_ref, sbo_ref, s_sc, h_sc, o_sc, *, t_valid, t_pad):
    C = REC_CHUNK
    tm = x_ref.shape[1]
    t = pl.program_id(1)

    @pl.when(t == 0)
    def _():
        s_sc[0:4] = sa_ref[0]
        s_sc[4:6] = sb_ref[0]

    h_sc[...] = _dot(_rms(x_ref[0], g_ref[...]).astype(BF16), win_ref[...])

    def chunk_body(c, carry):
        r0 = pl.multiple_of(c * C, C)
        _ab_chunk(r0, t * tm + r0, h_sc, o_sc, s_sc, wgk_ref, bgk_ref, lb_ref, agn_ref, bgn_ref,
                  t_valid=t_valid, t_pad=t_pad)
        return carry

    lax.fori_loop(0, tm // C, chunk_body, 0)
    xo_ref[0] = x_ref[0] + _dot(o_sc[...], wout_ref[...])

    @pl.when(t == pl.num_programs(1) - 1)
    def _():
        sao_ref[0] = s_sc[0:4]
        sbo_ref[0] = s_sc[4:6]


def _ab_chunk(r0, row0, h_sc, o_sc, s_sc, wgk_ref, bgk_ref, lb_ref, agn_ref, bgn_ref, *, t_valid, t_pad):
    C = REC_CHUNK
    cols = lambda a, b: h_sc[pl.ds(r0, C), a:b]
    a_q, a_f, a_i, a_g = cols(0, 512), cols(512, 1024), cols(1024, 1536), cols(1536, 2048)
    b_q, b_k, b_v, b_g = cols(2048, 2304), cols(2304, 2560), cols(2560, 3072), cols(3072, 3584)
    b_lr = cols(3584, 3712)

    lb = lb_ref[...]
    f = lb + (1.0 - lb) * jax.nn.sigmoid(a_f)
    gk = _dot(b_lr.astype(BF16), wgk_ref[...]) + bgk_ref[...]
    la = jnp.concatenate([jnp.log2(f), _log_sigmoid(gk) * (LOG2E / B_GATE_NORM)], axis=1)
    q = jnp.concatenate([_silu(a_q), b_q * (B_DK ** -0.5)], axis=1)
    k = jnp.concatenate([1.0 - f, b_k], axis=1)
    v_a, v_b = a_i, b_v
    if t_valid < t_pad:
        ok = (row0 + _iota((C, 1), 0)) < t_valid
        la = jnp.where(ok, la, 0.0)
        k = jnp.where(ok, k, 0.0)
        v_a = jnp.where(ok, v_a, 0.0)
        v_b = jnp.where(ok, v_b, 0.0)

    row = _iota((C, C), 0)
    col = _iota((C, C), 1)
    causal = col <= row
    tri = jnp.where(causal, 1.0, 0.0).astype(BF16)
    tri_in = jnp.where(causal & ((row >> 4) == (col >> 4)), 1.0, 0.0).astype(BF16)
    bc = _tri_dot(tri, la)
    b_in = _tri_dot(tri_in, la)
    lane = _iota((1, LANES), 1)
    lo_half = lane < 64
    srow = _iota((LANES, 1), 0) < 64

    o_heads = [None] * 8
    for u in range(6):
        sl = slice(u * LANES, (u + 1) * LANES)
        qu, ku, bcu = q[:, sl], k[:, sl], bc[:, sl]
        s_old = s_sc[u]
        bend = bcu[C - 1:C, :]
        qt = qu * jnp.exp2(b_in[:, sl])
        qdec = qu * jnp.exp2(bcu)
        if u < 4:
            heads = [(u, None, v_a[:, sl])]
        else:
            ha = 4 + 2 * (u - 4)
            heads = [(ha, lo_half, v_b[:, (ha - 4) * LANES:(ha - 3) * LANES]),
                     (ha + 1, jnp.logical_not(lo_half), v_b[:, (ha - 3) * LANES:(ha - 2) * LANES])]
        a_rows = [[] for _ in heads]
        for i in range(C // RSUB):
            n = RSUB * (i + 1)
            if i == 0:
                kt = ku[0:n] * jnp.exp2(-bcu[0:n])
            else:
                kt = ku[0:n] * jnp.exp2(bcu[RSUB * i - 1:RSUB * i, :] - bcu[0:n])
            if n < C:
                kt = jnp.concatenate([kt, jnp.zeros((C - n, LANES), F32)], axis=0)
            ktb = kt.astype(BF16)
            qi = qt[RSUB * i:RSUB * (i + 1)]
            for hi_, (_, msk, _) in enumerate(heads):
                qim = qi if msk is None else jnp.where(msk, qi, 0.0)
                a_rows[hi_].append(_dot_nt(qim.astype(BF16), ktb))
        sb16 = s_old.astype(BF16)
        for hi_, (hd, msk, vh) in enumerate(heads):
            att = jnp.where(causal, jnp.concatenate(a_rows[hi_], axis=0), 0.0)
            qd = qdec if msk is None else jnp.where(msk, qdec, 0.0)
            o_heads[hd] = _dot(qd.astype(BF16), sb16) + _dot(att.astype(BF16), vh.astype(BF16))
        kht = (ku * jnp.exp2(bend - bcu)).T
        dcol = jnp.broadcast_to(jnp.exp2(bend), (LANES, LANES)).T
        if u < 4:
            upd = _dot(kht.astype(BF16), heads[0][2].astype(BF16))
        else:
            lhs = jnp.concatenate([jnp.where(srow, kht, 0.0), jnp.where(srow, 0.0, kht)], axis=1)
            rhs = jnp.concatenate([heads[0][2], heads[1][2]], axis=0)
            upd = _dot(lhs.astype(BF16), rhs.astype(BF16))
        s_sc[u] = dcol * s_old + upd

    outs = []
    for hd in range(8):
        if hd < 4:
            gn, gate = agn_ref[...], a_g[:, hd * LANES:(hd + 1) * LANES]
        else:
            gn, gate = bgn_ref[...], b_g[:, (hd - 4) * LANES:(hd - 3) * LANES]
        outs.append(_rms(o_heads[hd], gn) * _silu(gate))
    o_sc[pl.ds(r0, C), :] = jnp.concatenate(outs, axis=1).astype(BF16)


def _ab_layer(x, s_a, s_b, g, w_in, w_gk, b_gk, lb, a_gn, b_gn, w_out, t_valid):
    bsz, t_pad, _ = x.shape
    tm = 512 if t_pad % 512 == 0 else REC_CHUNK
    kern = functools.partial(_ab_kernel, t_valid=t_valid, t_pad=t_pad)
    return pl.pallas_call(
        kern,
        grid=(bsz, t_pad // tm),
        in_specs=[
            pl.BlockSpec((1, tm, D_MODEL), lambda b, t: (b, t, 0)),
            pl.BlockSpec((1, 4, 128, 128), lambda b, t: (b, 0, 0, 0)),
            pl.BlockSpec((1, 2, 128, 128), lambda b, t: (b, 0, 0, 0)),
            _const_spec((1, D_MODEL)),
            _const_spec((D_MODEL, AB_COLS)),
            _const_spec((LANES, 256)),
            _const_spec((1, 256)),
            _const_spec((1, 512)),
            _const_spec((1, 128)),
            _const_spec((1, 128)),
            _const_spec((D_MODEL, D_MODEL)),
        ],
        out_specs=[
            pl.BlockSpec((1, tm, D_MODEL), lambda b, t: (b, t, 0)),
            pl.BlockSpec((1, 4, 128, 128), lambda b, t: (b, 0, 0, 0)),
            pl.BlockSpec((1, 2, 128, 128), lambda b, t: (b, 0, 0, 0)),
        ],
        out_shape=[
            jax.ShapeDtypeStruct((bsz, t_pad, D_MODEL), F32),
            jax.ShapeDtypeStruct((bsz, 4, 128, 128), F32),
            jax.ShapeDtypeStruct((bsz, 2, 128, 128), F32),
        ],
        scratch_shapes=[pltpu.VMEM((6, 128, 128), F32), pltpu.VMEM((tm, AB_COLS), F32),
                        pltpu.VMEM((tm, D_MODEL), BF16)],
        compiler_params=_params(("parallel", "arbitrary")),
        name="ab_layer",
    )(x, s_a, s_b, g, w_in, w_gk, b_gk, lb, a_gn, b_gn, w_out)


FFN_TILE = FFN_HIDDEN // 2


def _ffn_kernel(*refs, has_attn, has_final):
    refs = list(refs)
    x_ref = refs.pop(0)
    x = x_ref[...]
    if has_attn:
        oc_ref, od_ref, wo_ref = refs.pop(0), refs.pop(0), refs.pop(0)
        x = x + _dot(jnp.concatenate([oc_ref[...], od_ref[...]], axis=1), wo_ref[...])
    g_ref, win_ref, wout_ref = refs.pop(0), refs.pop(0), refs.pop(0)
    gf_ref = refs.pop(0) if has_final else None
    out_ref = refs.pop(0)
    xn = _rms(x, g_ref[...]).astype(BF16)
    acc = x
    for j in range(FFN_HIDDEN // FFN_TILE):
        gate = _dot(xn, win_ref[:, j * FFN_TILE:(j + 1) * FFN_TILE])
        up = _dot(xn, win_ref[:, FFN_HIDDEN + j * FFN_TILE:FFN_HIDDEN + (j + 1) * FFN_TILE])
        act = (_silu(gate) * up).astype(BF16)
        acc = acc + _dot(act, wout_ref[j * FFN_TILE:(j + 1) * FFN_TILE, :])
    if has_final:
        acc = _rms(acc, gf_ref[...])
    out_ref[...] = acc


def _ffn(x, g, w_in, w_out, attn=None, g_final=None, tm=512):
    n = x.shape[0]
    tm = min(tm, n)
    row = lambda i: (i, 0)
    args = [x]
    specs = [pl.BlockSpec((tm, D_MODEL), row)]
    if attn is not None:
        oc, od, wo = attn
        args += [oc, od, wo]
        specs += [pl.BlockSpec((tm, 512), row), pl.BlockSpec((tm, 512), row), _const_spec((D_MODEL, D_MODEL))]
    args += [g, w_in, w_out]
    specs += [_const_spec((1, D_MODEL)), _const_spec((D_MODEL, 2 * FFN_HIDDEN)), _const_spec((FFN_HIDDEN, D_MODEL))]
    if g_final is not None:
        args.append(g_final)
        specs.append(_const_spec((1, D_MODEL)))
    kern = functools.partial(_ffn_kernel, has_attn=attn is not None, has_final=g_final is not None)
    return pl.pallas_call(
        kern,
        grid=(n // tm,),
        in_specs=specs,
        out_specs=pl.BlockSpec((tm, D_MODEL), row),
        out_shape=jax.ShapeDtypeStruct((n, D_MODEL), F32),
        compiler_params=_params(("parallel",)),
        name="ffn",
    )(*args)


def _cd_proj_kernel(x_ref, g_ref, w_ref, fb_ref, fq_ref, ck_ref, cv_ref, dq_ref, dk_ref, dv_ref, iq_ref,
                    misc_ref, iw_ref):
    xn = _rms(x_ref[...], g_ref[...]).astype(BF16)
    h = _dot(xn, w_ref[...])
    fq_ref[...] = (h[:, 0:512] * (C_HD ** -0.5 * LOG2E)).astype(BF16)
    ck_ref[...] = h[:, 512:1024]
    cv_ref[...] = h[:, 1024:1536]
    dq_ref[...] = (h[:, 1536:2560] * (D_HD ** -0.5 * LOG2E)).astype(BF16)
    dk_ref[...] = h[:, 2560:2688]
    dv_ref[...] = h[:, 2688:2816]
    iq_ref[...] = h[:, 2816:3072].astype(BF16)
    misc = h[:, 3072:3200]
    lane = _iota((1, LANES), 1)
    is_f = (lane >= IDX_DIM) & (lane < IDX_DIM + C_HEADS)
    misc_ref[...] = jnp.where(is_f, _log_sigmoid(misc + fb_ref[...]), misc)
    iw_ref[...] = h[:, 3200:3328]


def _cd_proj(x, g, w, fb, tm=512):
    n = x.shape[0]
    tm = min(tm, n)
    row = lambda i: (i, 0)
    widths = [(512, BF16), (512, F32), (512, F32), (1024, BF16), (128, F32), (128, F32), (256, BF16),
              (128, F32), (128, F32)]
    return pl.pallas_call(
        _cd_proj_kernel,
        grid=(n // tm,),
        in_specs=[pl.BlockSpec((tm, D_MODEL), row), _const_spec((1, D_MODEL)), _const_spec((D_MODEL, CD_COLS)),
                  _const_spec((1, LANES))],
        out_specs=[pl.BlockSpec((tm, w_), row) for w_, _ in widths],
        out_shape=[jax.ShapeDtypeStruct((n, w_), dt) for w_, dt in widths],
        compiler_params=_params(("parallel",)),
        name="cd_proj",
    )(x, g, w, fb)


def _aug_consts(is_query):
    p = np.zeros((3, LANES, 8 * LANES), np.float32)
    ones = np.zeros((1, 8 * LANES), np.float32)
    for h in range(8):
        off = h * LANES + (64 if h % 2 == 0 else 0)
        for c in range(3):
            if is_query:
                p[c, h, off + c] = 1.0
                ones[0, off + 3 + c] = 1.0
            else:
                p[c, h, off + 3 + c] = -1.0
                ones[0, off + c] = 1.0
    return jnp.asarray(p, BF16), jnp.asarray(ones, F32)


def _aug_lanes(cum, p_ref, ones_ref):
    hi, mid, lo = _split3(cum * LOG2E)
    return _dot(hi, p_ref[0]) + _dot(mid, p_ref[1]) + _dot(lo, p_ref[2]) + ones_ref[...]


def _running_sum(lf, carry):
    tm = lf.shape[0]

    @pl.when(pl.program_id(1) == 0)
    def _():
        carry[...] = jnp.zeros_like(carry)

    tri = jnp.where(_iota((tm, tm), 1) <= _iota((tm, tm), 0), 1.0, 0.0).astype(BF16)
    cum = _tri_dot(tri, lf) + carry[...]
    carry[...] = cum[tm - 1:tm, :]
    return cum


def _fox_rows(kk, vv, cum, p_ref, ones_ref, ka_ref, vt_ref):
    aug = _aug_lanes(cum, p_ref, ones_ref)
    lane = _iota((1, LANES), 1)
    for h in range(8):
        pr = slice((h // 2) * LANES, (h // 2 + 1) * LANES)
        own = (lane < 64) if h % 2 == 0 else (lane >= 64)
        ka_ref[0, h] = jnp.where(own, kk[:, pr], aug[:, h * LANES:(h + 1) * LANES]).astype(BF16)
    for pr in range(4):
        vt_ref[0, pr, 0] = vv[:, pr * LANES:(pr + 1) * LANES].T.astype(BF16)


def _fox_pack_kernel(k_ref, v_ref, lf_ref, p_ref, ones_ref, ka_ref, vt_ref, cum_ref, carry):
    cum = _running_sum(lf_ref[0], carry)
    cum_ref[0] = cum
    _fox_rows(k_ref[0], v_ref[0], cum, p_ref, ones_ref, ka_ref, vt_ref)


def _fox_pack(k, v, lf, tm):
    bsz, s, _ = k.shape
    p, ones = _aug_consts(False)
    return pl.pallas_call(
        _fox_pack_kernel,
        grid=(bsz, s // tm),
        in_specs=[pl.BlockSpec((1, tm, 512), lambda b, t: (b, t, 0)),
                  pl.BlockSpec((1, tm, 512), lambda b, t: (b, t, 0)),
                  pl.BlockSpec((1, tm, LANES), lambda b, t: (b, t, 0)),
                  _const_spec((3, LANES, 8 * LANES)), _const_spec((1, 8 * LANES))],
        out_specs=[pl.BlockSpec((1, 8, tm, LANES), lambda b, t: (b, 0, t, 0)),
                   pl.BlockSpec((1, 4, 1, LANES, tm), lambda b, t: (b, 0, t, 0, 0)),
                   pl.BlockSpec((1, tm, LANES), lambda b, t: (b, t, 0))],
        out_shape=[jax.ShapeDtypeStruct((bsz, 8, s, LANES), BF16),
                   jax.ShapeDtypeStruct((bsz, 4, s // tm, LANES, tm), BF16),
                   jax.ShapeDtypeStruct((bsz, s, LANES), F32)],
        scratch_shapes=[pltpu.VMEM((1, LANES), F32)],
        compiler_params=_params(("parallel", "arbitrary")),
        name="fox_pack",
    )(k, v, lf, p, ones)


def _q_rows(qq, cum, p_ref, ones_ref, qa_ref):
    aug = _aug_lanes(cum, p_ref, ones_ref)
    lane = _iota((1, LANES), 1)
    for h in range(8):
        pr = slice((h // 2) * LANES, (h // 2 + 1) * LANES)
        own = (lane < 64) if h % 2 == 0 else (lane >= 64)
        qa_ref[0, h] = jnp.where(own, qq[:, pr], aug[:, h * LANES:(h + 1) * LANES]).astype(BF16)


def _q_pack_kernel(q_ref, cum_ref, p_ref, ones_ref, qa_ref):
    _q_rows(q_ref[0].astype(F32), cum_ref[0], p_ref, ones_ref, qa_ref)


def _q_pack(q, cum_q, tm):
    bsz, tq, _ = q.shape
    p, ones = _aug_consts(True)
    return pl.pallas_call(
        _q_pack_kernel,
        grid=(bsz, tq // tm),
        in_specs=[pl.BlockSpec((1, tm, 512), lambda b, t: (b, t, 0)),
                  pl.BlockSpec((1, tm, LANES), lambda b, t: (b, t, 0)),
                  _const_spec((3, LANES, 8 * LANES)), _const_spec((1, 8 * LANES))],
        out_specs=pl.BlockSpec((1, 8, tm, LANES), lambda b, t: (b, 0, t, 0)),
        out_shape=jax.ShapeDtypeStruct((bsz, 8, tq, LANES), BF16),
        compiler_params=_params(("parallel", "parallel")),
        name="q_pack",
    )(q, cum_q, p, ones)


def _fox_attn_kernel(q_ref, k_ref, vt_ref, o_ref, s_sc, *, tq, tk, q_start):
    i = pl.program_id(2)
    q_lo = q_start + i * tq
    n_full = (q_lo + 1) // tk
    n_tiles = (q_lo + tq + tk - 1) // tk
    q_pos = q_lo + _iota((1, tq), 1)
    qs = (q_ref[0, 0], q_ref[0, 1])

    def logits(j):
        off = pl.multiple_of(j * tk, tk)
        return [_dot_nt(k_ref[0, hh, pl.ds(off, tk), :], qs[hh]) for hh in range(2)]

    s_sc[0], s_sc[1] = logits(0)

    def body(j, carry, masked):
        off = pl.multiple_of(j * tk, tk)
        s_next = logits(jnp.minimum(j + 1, n_tiles - 1))
        ss = [s_sc[0], s_sc[1]]
        if masked:
            ok = (off + _iota((tk, 1), 0)) <= q_pos
            ss = [jnp.where(ok, s, MASKED) for s in ss]
        out = []
        for hh in range(2):
            m, acc = carry[hh]
            m_new = jnp.maximum(m, jnp.max(ss[hh], axis=0, keepdims=True))
            p = jnp.exp2(ss[hh] - m_new).astype(BF16)
            vt = jnp.concatenate([vt_ref[0, 0, j, hh * 64:(hh + 1) * 64, :], ones], axis=0)
            acc = jnp.exp2(m - m_new) * acc + _dot(vt, p)
            out.append((m_new, acc))
        s_sc[0], s_sc[1] = s_next
        return tuple(out)

    ones = jnp.ones((ONES_ROWS, tk), BF16)
    init = (jnp.full((1, tq), NEG, F32), jnp.zeros((64 + ONES_ROWS, tq), F32))
    carry = lax.fori_loop(0, n_full, functools.partial(body, masked=False), (init, init))
    carry = lax.fori_loop(n_full, n_tiles, functools.partial(body, masked=True), carry)
    o_t = jnp.concatenate([acc[0:64] / acc[64:65] for _, acc in carry], axis=0)
    o_ref[0] = o_t.T.astype(BF16)


def _fox_attn(q_aug, k_aug, vt, tq, tk, q_start):
    bsz, _, t_q, _ = q_aug.shape
    s = k_aug.shape[2]
    kern = functools.partial(_fox_attn_kernel, tq=tq, tk=tk, q_start=q_start)
    return pl.pallas_call(
        kern,
        grid=(bsz, 4, t_q // tq),
        in_specs=[pl.BlockSpec((1, 2, tq, LANES), lambda b, p, i: (b, p, i, 0)),
                  pl.BlockSpec((1, 2, s, LANES), lambda b, p, i: (b, p, 0, 0)),
                  pl.BlockSpec((1, 1, s // tk, LANES, tk), lambda b, p, i: (b, p, 0, 0, 0))],
        out_specs=pl.BlockSpec((1, tq, LANES), lambda b, p, i: (b, i, p)),
        out_shape=jax.ShapeDtypeStruct((bsz, t_q, 512), BF16),
        scratch_shapes=[pltpu.VMEM((2, tk, tq), F32)],
        compiler_params=_params(("parallel", "parallel", "arbitrary")),
        name="fox_attn",
    )(q_aug, k_aug, vt)


def _dsa_rows(dk, dv, ik, k2_ref, vt_ref, ik2_ref):
    lo = _iota((1, LANES), 1) < 64
    k2_ref[0] = dk.astype(BF16)
    vt_ref[0, 0] = dv.T.astype(BF16)
    ik = jnp.where(lo, ik, 0.0)
    ik2_ref[0, 0] = ik.astype(BF16)
    ik2_ref[0, 1] = pltpu.roll(ik, 64, 1).astype(BF16)


def _dsa_pack_kernel(dk_ref, dv_ref, ik_ref, k2_ref, vt_ref, ik2_ref):
    _dsa_rows(dk_ref[0], dv_ref[0], ik_ref[0], k2_ref, vt_ref, ik2_ref)


def _dsa_pack(dk, dv, ik, tm):
    bsz, s, _ = dk.shape
    tok = pl.BlockSpec((1, tm, LANES), lambda b, t: (b, t, 0))
    return pl.pallas_call(
        _dsa_pack_kernel,
        grid=(bsz, s // tm),
        in_specs=[tok, tok, tok],
        out_specs=[tok,
                   pl.BlockSpec((1, 1, LANES, tm), lambda b, t: (b, t, 0, 0)),
                   pl.BlockSpec((1, 2, tm, LANES), lambda b, t: (b, 0, t, 0))],
        out_shape=[jax.ShapeDtypeStruct((bsz, s, LANES), BF16),
                   jax.ShapeDtypeStruct((bsz, s // tm, LANES, tm), BF16),
                   jax.ShapeDtypeStruct((bsz, 2, s, LANES), BF16)],
        compiler_params=_params(("parallel", "parallel")),
        name="dsa_pack",
    )(dk, dv, ik)


def _cd_fused_kernel(x_ref, g_ref, w_ref, fb_ref, pk_ref, onesk_ref, pq_ref, onesq_ref,
                     ck_ref, cv_ref, dk_ref, dv_ref, misc_ref, iw_ref, iq_ref, dq_ref,
                     qa_ref, ka_ref, vt_ref, k2_ref, dvt_ref, ik2_ref, carry):
    xn = _rms(x_ref[0], g_ref[...]).astype(BF16)
    h = _dot(xn, w_ref[...])
    ck, cv = h[:, 512:1024], h[:, 1024:1536]
    dk, dv = h[:, 2560:2688], h[:, 2688:2816]
    misc = h[:, 3072:3200]
    lane = _iota((1, LANES), 1)
    is_f = (lane >= IDX_DIM) & (lane < IDX_DIM + C_HEADS)
    logf = _log_sigmoid(misc + fb_ref[...])
    misc = jnp.where(is_f, logf, misc)
    ck_ref[0], cv_ref[0], dk_ref[0], dv_ref[0], misc_ref[0] = ck, cv, dk, dv, misc
    iw_ref[0] = h[:, 3200:3328]
    iq_ref[0] = h[:, 2816:3072].astype(BF16)
    dq_ref[0] = (h[:, 1536:2560] * (D_HD ** -0.5 * LOG2E)).astype(BF16)
    cum = _running_sum(pltpu.roll(jnp.where(is_f, logf, 0.0), LANES - IDX_DIM, 1), carry)
    _fox_rows(ck, cv, cum, pk_ref, onesk_ref, ka_ref, vt_ref)
    _q_rows(h[:, 0:512] * (C_HD ** -0.5 * LOG2E), cum, pq_ref, onesq_ref, qa_ref)
    _dsa_rows(dk, dv, misc, k2_ref, dvt_ref, ik2_ref)


def _cd_fused(x, g, w, fb, tm):
    bsz, t_len, _ = x.shape
    pk, onesk = _aug_consts(False)
    pq, onesq = _aug_consts(True)
    tok = lambda w_: pl.BlockSpec((1, tm, w_), lambda b, t: (b, t, 0))
    hm = pl.BlockSpec((1, 8, tm, LANES), lambda b, t: (b, 0, t, 0))
    sds = jax.ShapeDtypeStruct
    n_t = t_len // tm
    return pl.pallas_call(
        _cd_fused_kernel,
        grid=(bsz, n_t),
        in_specs=[tok(D_MODEL), _const_spec((1, D_MODEL)), _const_spec((D_MODEL, CD_COLS)), _const_spec((1, LANES)),
                  _const_spec((3, LANES, 8 * LANES)), _const_spec((1, 8 * LANES)),
                  _const_spec((3, LANES, 8 * LANES)), _const_spec((1, 8 * LANES))],
        out_specs=[tok(512), tok(512), tok(LANES), tok(LANES), tok(LANES), tok(LANES), tok(256), tok(8 * LANES),
                   hm, hm, pl.BlockSpec((1, 4, 1, LANES, tm), lambda b, t: (b, 0, t, 0, 0)),
                   tok(LANES), pl.BlockSpec((1, 1, LANES, tm), lambda b, t: (b, t, 0, 0)),
                   pl.BlockSpec((1, 2, tm, LANES), lambda b, t: (b, 0, t, 0))],
        out_shape=[sds((bsz, t_len, 512), F32), sds((bsz, t_len, 512), F32), sds((bsz, t_len, LANES), F32),
                   sds((bsz, t_len, LANES), F32), sds((bsz, t_len, LANES), F32), sds((bsz, t_len, LANES), F32),
                   sds((bsz, t_len, 256), BF16), sds((bsz, t_len, 8 * LANES), BF16),
                   sds((bsz, 8, t_len, LANES), BF16), sds((bsz, 8, t_len, LANES), BF16),
                   sds((bsz, 4, n_t, LANES, tm), BF16), sds((bsz, t_len, LANES), BF16),
                   sds((bsz, n_t, LANES, tm), BF16), sds((bsz, 2, t_len, LANES), BF16)],
        scratch_shapes=[pltpu.VMEM((1, LANES), F32)],
        compiler_params=_params(("parallel", "arbitrary")),
        name="cd_fused",
    )(x, g, w, fb, pk, onesk, pq, onesq)


def _dsa_attn_kernel(dq_ref, iq_ref, iw_ref, ik2_ref, k2_ref, vt_ref, o_ref, keys_sc, hi_sc, lo_sc, eq_sc, m_sc, mt_sc, acc_sc,
                     s_sc,
                     *, tq, tk, q_start, s_valid, topk):
    i = pl.program_id(1)
    q_lo = q_start + i * tq
    adm_row = jnp.minimum(((q_lo + _iota((1, tq), 1)) // CHUNK + 1) * CHUNK, s_valid)
    adm_end = jnp.minimum(((q_lo + tq - 1) // CHUNK + 1) * CHUNK, s_valid)
    n_tiles = (adm_end + tk - 1) // tk
    iw_t = iw_ref[0].T
    iq = iq_ref[0]

    def score_body(j, c):
        off = pl.multiple_of(j * tk, tk)
        sc = jnp.zeros((tk, tq), F32)
        for hd in range(IDX_HEADS):
            qp = iq[:, (hd // 2) * LANES:(hd // 2 + 1) * LANES]
            r = jnp.maximum(_dot_nt(ik2_ref[0, hd % 2, pl.ds(off, tk), :], qp), 0.0)
            sc = sc + r * iw_t[hd:hd + 1, :]
        sc = sc * IDX_SCALE
        adm = (off + _iota((tk, tq), 0)) < adm_row
        bits = lax.bitcast_convert_type(sc, I32)
        bits = jnp.where(bits == INT_MIN, 0, bits)
        key = bits ^ ((bits >> 31) & 0x7FFFFFFF)
        keys_sc[j] = jnp.where(adm, key, INT_MIN)
        hi = lax.bitcast_convert_type(bits & -65536, F32)
        hi_sc[j] = jnp.where(adm, hi, -jnp.inf).astype(BF16)
        return c

    lax.fori_loop(0, n_tiles, score_body, 0)

    def pairs(one, init):
        part = lax.fori_loop(0, n_tiles // 2, lambda jj, c: one(2 * jj + 1, one(2 * jj, c)), init)
        return lax.fori_loop(2 * (n_tiles // 2), n_tiles, one, part)

    one16 = jnp.ones((tk, tq), BF16)
    zero16 = jnp.zeros((tk, tq), BF16)

    def hi_body(b, pre):
        cand = pre + lax.shift_left(jnp.int32(1), 15 - b)
        cbits = (cand ^ ((cand >> 15) & 0x7FFF)) & 0xFFFF
        subnormal = ((cbits & 0x7F80) == 0) & ((cbits & 0x007F) != 0)
        cbits = jnp.where(subnormal, jnp.where((cbits & 0x8000) != 0, 0x0000, 0x0080), cbits)
        cval = jnp.broadcast_to(lax.bitcast_convert_type(cbits << 16, F32).astype(BF16), (tk, tq))

        def one(j, c):
            hit = jnp.where(hi_sc[j] >= cval, one16, zero16).reshape(tk // 64, 4, 16, tq)
            for r in range(tk // 64):
                c = c + hit[r]
            return c
        part = pairs(one, jnp.zeros((4, 16, tq), BF16)).astype(F32)
        cnt = jnp.sum(jnp.sum(part, axis=0), axis=0, keepdims=True)
        return jnp.where(cnt >= topk, cand, pre)

    pre = lax.fori_loop(0, 16, hi_body, jnp.full((1, tq), -2 ** 15, I32))

    def count(pred):
        def one(j, c):
            hit = jnp.where(pred(j, keys_sc[j]), 1.0, 0.0)
            return c + jnp.sum(hit.reshape(tk // 32, 4, 8, tq), axis=0)

        part = pairs(one, jnp.zeros((4, 8, tq), F32))
        return jnp.sum(jnp.sum(part, axis=0), axis=0, keepdims=True)

    def lo_prep(j, c):
        kk = keys_sc[j]
        lo_sc[j] = jnp.where((kk >> 16) == pre, (kk & 0xFFFF) - 2 ** 15, -2 ** 15).astype(jnp.int16)
        hit = jnp.where((kk >> 16) > pre, 1.0, 0.0)
        return c + jnp.sum(hit.reshape(tk // 32, 4, 8, tq), axis=0)

    above = lax.fori_loop(0, n_tiles, lo_prep, jnp.zeros((4, 8, tq), F32))
    above = jnp.sum(jnp.sum(above, axis=0), axis=0, keepdims=True)
    one_i16 = jnp.ones((tk, tq), jnp.int16)
    zero_i16 = jnp.zeros((tk, tq), jnp.int16)

    def lo_body(b, low):
        cand = low + lax.shift_left(jnp.int32(1), 15 - b)
        cval = jnp.broadcast_to(cand.astype(jnp.int16), (tk, tq))

        def one(j, c):
            hit = jnp.where(lo_sc[j] >= cval, one_i16, zero_i16).reshape(tk // 64, 4, 16, tq)
            for r in range(tk // 64):
                c = c + hit[r]
            return c
        part = pairs(one, jnp.zeros((4, 16, tq), jnp.int16)).astype(I32).astype(F32)
        cnt = above + jnp.sum(jnp.sum(part, axis=0), axis=0, keepdims=True)
        return jnp.where(cnt >= topk, cand, low)

    low = lax.fori_loop(0, 16, lo_body, jnp.full((1, tq), -2 ** 15, I32))
    tau = (pre << 16) + (low + 2 ** 15)
    tau = jnp.where(adm_row < int(topk), INT_MIN, tau)
    need = jnp.where(tau == INT_MIN, -1.0, topk - count(lambda j, kk: kk > tau))
    eq_sc[...] = jnp.zeros(eq_sc.shape, F32)
    tri = jnp.where(_iota((tk, tk), 1) <= _iota((tk, tk), 0), 1.0, 0.0).astype(BF16)

    m_sc[...] = jnp.full(m_sc.shape, NEG, F32)
    acc_sc[...] = jnp.zeros(acc_sc.shape, F32)
    ones = jnp.ones((ONES_ROWS, tk), BF16)
    n_rep = D_HEADS // D_KV_HEADS
    q_stack = [jnp.concatenate([dq_ref[0, :, hd * LANES:(hd + 1) * LANES]
                                for hd in range(g * n_rep, (g + 1) * n_rep)], axis=0) for g in range(D_KV_HEADS)]

    def logits(j):
        kk = keys_sc[j]
        eq = kk == tau
        run = _dot(tri, jnp.where(eq, 1.0, 0.0).astype(BF16)) + eq_sc[...]
        eq_sc[...] = run[tk - 1:tk, :]
        sel = (kk > tau) | (eq & (run <= need))
        bias = jnp.where(sel, 0.0, MASKED)
        bias = jnp.concatenate([bias] * n_rep, axis=1)
        kt = k2_ref[0, pl.ds(pl.multiple_of(j * tk, tk), tk), :]
        ss = [_dot_nt(kt, q_stack[g]) + bias for g in range(D_KV_HEADS)]
        return ss, [jnp.max(s, axis=0, keepdims=True) for s in ss]

    def stage(slot, ss, mts):
        for g in range(D_KV_HEADS):
            s_sc[slot, g] = ss[g]
            mt_sc[slot, g] = mts[g]

    stage(0, *logits(0))

    def step(j, cur):
        stage(1 - cur, *logits(jnp.minimum(j + 1, n_tiles - 1)))
        for g in range(D_KV_HEADS):
            m_old = m_sc[g]
            m_new = jnp.maximum(m_old, mt_sc[cur, g])
            p = jnp.exp2(s_sc[cur, g] - m_new).astype(BF16)
            vt = jnp.concatenate([vt_ref[0, j, g * 64:(g + 1) * 64, :], ones], axis=0)
            acc_sc[g] = jnp.exp2(m_old - m_new) * acc_sc[g] + _dot(vt, p)
            m_sc[g] = m_new

    def attn_body(jj, c):
        step(2 * jj, 0)

        @pl.when(2 * jj + 1 < n_tiles)
        def _():
            step(2 * jj + 1, 1)
        return c

    lax.fori_loop(0, (n_tiles + 1) // 2, attn_body, 0)
    for g in range(D_KV_HEADS):
        o_g = acc_sc[g, 0:64] / acc_sc[g, 64:65]
        for pr in range(n_rep // 2):
            o_t = jnp.concatenate([o_g[:, (2 * pr) * tq:(2 * pr + 1) * tq],
                                   o_g[:, (2 * pr + 1) * tq:(2 * pr + 2) * tq]], axis=0)
            col = (g * n_rep // 2 + pr) * LANES
            o_ref[0, :, col:col + LANES] = o_t.T.astype(BF16)


def _dsa_attn(dq, iq, iw, ik2, k2, vt, tq, tk, q_start, s_valid):
    bsz, t_q, _ = dq.shape
    s = k2.shape[1]
    topk = min(IDX_TOPK_MAX, s_valid // 4)
    kern = functools.partial(_dsa_attn_kernel, tq=tq, tk=tk, q_start=q_start, s_valid=s_valid, topk=float(topk))
    return pl.pallas_call(
        kern,
        grid=(bsz, t_q // tq),
        in_specs=[pl.BlockSpec((1, tq, 8 * LANES), lambda b, i: (b, i, 0)),
                  pl.BlockSpec((1, tq, 256), lambda b, i: (b, i, 0)),
                  pl.BlockSpec((1, tq, LANES), lambda b, i: (b, i, 0)),
                  pl.BlockSpec((1, 2, s, LANES), lambda b, i: (b, 0, 0, 0)),
                  pl.BlockSpec((1, s, LANES), lambda b, i: (b, 0, 0)),
                  pl.BlockSpec((1, s // tk, LANES, tk), lambda b, i: (b, 0, 0, 0))],
        out_specs=pl.BlockSpec((1, tq, 512), lambda b, i: (b, i, 0)),
        out_shape=jax.ShapeDtypeStruct((bsz, t_q, 512), BF16),
        scratch_shapes=[pltpu.VMEM((s // tk, tk, tq), I32),
                        pltpu.VMEM((s // tk, tk, tq), BF16),
                        pltpu.VMEM((s // tk, tk, tq), jnp.int16),
                        pltpu.VMEM((1, tq), F32),
                        pltpu.VMEM((D_KV_HEADS, 1, D_HEADS // D_KV_HEADS * tq), F32),
                        pltpu.VMEM((2, D_KV_HEADS, 1, D_HEADS // D_KV_HEADS * tq), F32),
                        pltpu.VMEM((D_KV_HEADS, 64 + ONES_ROWS, D_HEADS // D_KV_HEADS * tq), F32),
                        pltpu.VMEM((2, D_KV_HEADS, tk, D_HEADS // D_KV_HEADS * tq), F32)],
        compiler_params=_params(("parallel", "arbitrary")),
        name="dsa_attn",
    )(dq, iq, iw, ik2, k2, vt)


def _prep_ab(w_in, w_gk):
    w = jnp.pad(w_in, ((0, 0), (0, AB_COLS - w_in.shape[1]))).astype(BF16)
    wgk = jnp.pad(w_gk, ((0, LANES - B_GATE_RANK), (0, 0))).astype(BF16)
    return w, wgk


def _prep_cd(w_in, f_bias):
    o = np.cumsum([0, 512, 512, 512, C_HEADS, 512, 128, 128, 256, IDX_DIM, IDX_HEADS])
    c_q, c_k, c_v, c_f, d_q, d_k, d_v, d_iq, d_ik, d_iw = (w_in[:, o[i]:o[i + 1]] for i in range(10))
    zeros = lambda n: jnp.zeros((D_MODEL, n), w_in.dtype)
    dq_cols = []
    for h in range(D_HEADS):
        g = h // (D_HEADS // D_KV_HEADS)
        wh = d_q[:, h * 64:(h + 1) * 64]
        dq_cols += [wh, zeros(64)] if g == 0 else [zeros(64), wh]
    w = jnp.concatenate([c_q, c_k, c_v] + dq_cols + [d_k, d_v, d_iq, d_ik, c_f, zeros(LANES - IDX_DIM - C_HEADS),
                                                     d_iw, zeros(LANES - IDX_HEADS)], axis=1).astype(BF16)
    fb = jnp.pad(f_bias.astype(F32), (IDX_DIM, LANES - IDX_DIM - C_HEADS)).reshape(1, LANES)
    return w, fb


def _pad_rows(z, s):
    return jnp.pad(z, ((0, 0), (0, s - z.shape[1]), (0, 0)))


def _pad_lanes(z, n=LANES):
    return jnp.pad(z, ((0, 0), (0, 0), (0, n - z.shape[2])))


def _trunk(x, s_a, s_b, cache, wts):
    bsz, t_len, _ = x.shape
    row = lambda z: z.reshape(1, -1).astype(F32)

    t_pad = -(-t_len // REC_CHUNK) * REC_CHUNK
    x1, sa_new, sb_new = _ab_layer(
        _pad_rows(x, t_pad), s_a, s_b.reshape(bsz, 2, 128, 128), row(wts['norm_mix'][0]), wts['ab_w_in'],
        wts['gla_w_gk'], row(wts['gla_b_gk']), row(wts['lb']), row(wts['hgrn_gnorm']), row(wts['gla_gnorm']),
        wts['ab_w_out'], t_len)
    n = bsz * t_len
    x1 = x1[:, :t_len].reshape(n, D_MODEL)
    x2 = _ffn(x1, row(wts['norm_ffn'][0]), wts['ffn_w_in'][0], wts['ffn_w_out'][0])

    tk = 512
    q_start = 0 if cache is None else cache[0].shape[1]
    s_valid = q_start + t_len
    tq_pad = -(-t_len // LANES) * LANES
    if cache is None and t_len % tk == 0:
        (ck, cv, dk, dv, misc, iw, iq, dq, q_aug, k_aug, v_t, k2, dv_t, ik2) = _cd_fused(
            x2.reshape(bsz, t_len, D_MODEL), row(wts['norm_mix'][1]), wts['cd_w_in'], wts['fox_f_bias'], tm=tk)
        logf = misc[:, :, IDX_DIM:IDX_DIM + C_HEADS]
    else:
        fq, ck, cv, dq, dk, dv, iq, misc, iw = _cd_proj(x2, row(wts['norm_mix'][1]), wts['cd_w_in'],
                                                        wts['fox_f_bias'])
        per_b = lambda z: z.reshape(bsz, t_len, z.shape[-1])
        fq, ck, cv, dq, dk, dv, iq, misc, iw = map(per_b, (fq, ck, cv, dq, dk, dv, iq, misc, iw))
        logf = misc[:, :, IDX_DIM:IDX_DIM + C_HEADS]
        if cache is None:
            k_all, v_all, lf_all, dk_all, dv_all, ik_all = ck, cv, _pad_lanes(logf), dk, dv, misc
        else:
            c_k, c_v, c_lf, c_dk, c_dv, c_ik = cache
            cat = lambda c, r: jnp.concatenate([c.reshape(bsz, q_start, -1).astype(F32), r], axis=1)
            k_all, v_all, dk_all, dv_all = cat(c_k, ck), cat(c_v, cv), cat(c_dk, dk), cat(c_dv, dv)
            lf_all = _pad_lanes(cat(c_lf, logf))
            ik_all = cat(_pad_lanes(c_ik), misc)
        s_pad = -(-s_valid // tk) * tk
        k_all, v_all, lf_all, dk_all, dv_all, ik_all = (_pad_rows(z, s_pad) for z in
                                                        (k_all, v_all, lf_all, dk_all, dv_all, ik_all))
        k_aug, v_t, cum = _fox_pack(k_all, v_all, lf_all, tm=tk)
        q_aug = _q_pack(_pad_rows(fq, tq_pad), _pad_rows(cum[:, q_start:q_start + t_len], tq_pad),
                        tm=512 if tq_pad % 512 == 0 else LANES)
        k2, dv_t, ik2 = _dsa_pack(dk_all, dv_all, ik_all, tm=tk)
    o_c = _fox_attn(q_aug, k_aug, v_t, tq=512 if tq_pad % 512 == 0 else LANES, tk=tk, q_start=q_start)[:, :t_len]
    o_d = _dsa_attn(_pad_rows(dq, tq_pad), _pad_rows(iq, tq_pad), _pad_rows(iw, tq_pad), ik2, k2, dv_t,
                    tq=256 if tq_pad % 256 == 0 else LANES, tk=tk, q_start=q_start, s_valid=s_valid)[:, :t_len]

    y = _ffn(x2, row(wts['norm_ffn'][1]), wts['ffn_w_in'][1], wts['ffn_w_out'][1],
             attn=(o_c.reshape(n, 512), o_d.reshape(n, 512), wts['cd_w_out']), g_final=row(wts['norm_final']))
    rows = (ck.reshape(1, bsz, t_len, C_HEADS, C_HD), cv.reshape(1, bsz, t_len, C_HEADS, C_HD),
            logf[None], dk.reshape(1, bsz, t_len, D_KV_HEADS, D_HD), dv.reshape(1, bsz, t_len, D_KV_HEADS, D_HD),
            misc[None, :, :, :IDX_DIM])
    return (y.reshape(bsz, t_len, D_MODEL), sa_new[None], sb_new.reshape(1, bsz, B_HEADS, B_DK, B_DV)) + rows


def kernel(x_prompt, x_sample, state_hgrn, state_gla, cache_fox_k, cache_fox_v, cache_fox_logf, cache_dsa_k, cache_dsa_v, cache_dsa_ik, norm_mix, norm_ffn, norm_final, ab_w_in, ab_w_out, hgrn_lb_logits, hgrn_gnorm, gla_w_gk, gla_b_gk, gla_gnorm, cd_w_in, cd_w_out, fox_f_bias, ffn_w_in, ffn_w_out):
    lbs = jnp.cumsum(jax.nn.softmax(hgrn_lb_logits.astype(F32), axis=0), axis=0)
    w_ab, w_gk = _prep_ab(ab_w_in[0], gla_w_gk[0])
    w_cd, fb = _prep_cd(cd_w_in[0], fox_f_bias[0])
    wts = dict(norm_mix=norm_mix, norm_ffn=norm_ffn, norm_final=norm_final, ab_w_in=w_ab,
               ab_w_out=ab_w_out[0].astype(BF16), lb=lbs[0], hgrn_gnorm=hgrn_gnorm[0], gla_w_gk=w_gk,
               gla_b_gk=gla_b_gk[0], gla_gnorm=gla_gnorm[0], cd_w_in=w_cd, cd_w_out=cd_w_out[0].astype(BF16),
               fox_f_bias=fb, ffn_w_in=ffn_w_in.astype(BF16), ffn_w_out=ffn_w_out.astype(BF16))
    bp = x_prompt.shape[0]
    p_out = _trunk(x_prompt, jnp.zeros((bp, A_HEADS, A_DK, A_DV), F32), jnp.zeros((bp, B_HEADS, B_DK, B_DV), F32),
                   None, wts)
    cache = (cache_fox_k[0], cache_fox_v[0], cache_fox_logf[0], cache_dsa_k[0], cache_dsa_v[0], cache_dsa_ik[0])
    s_out = _trunk(x_sample, state_hgrn[0], state_gla[0], cache, wts)
    return (p_out[0], s_out[0]) + tuple(p_out[1:]) + tuple(s_out[1:])
```

```python
import functools

import numpy as np
import jax
import jax.numpy as jnp
from jax import lax
from jax.experimental import pallas as pl
from jax.experimental.pallas import tpu as pltpu

F32 = jnp.float32
BF16 = jnp.bfloat16
I32 = jnp.int32

D_MODEL = 1024
CHUNK = 64
A_HEADS, A_DK, A_DV = 4, 128, 128
B_HEADS, B_DK, B_DV = 4, 64, 128
B_GATE_RANK = 16
B_GATE_NORM = 16.0
C_HEADS, C_HD = 8, 64
D_HEADS, D_KV_HEADS, D_HD = 8, 2, 64
IDX_HEADS, IDX_DIM = 4, 64
IDX_TOPK_MAX = 256
IDX_SCALE = (IDX_DIM ** -0.5) * (IDX_HEADS ** -0.5)
FFN_HIDDEN = ((8 * D_MODEL // 3 + 255) // 256) * 256

LANES = 128
RSUB = 16
REC_CHUNK = 128
VMEM_LIMIT = 56 * 1024 * 1024
NEG = -1e30
MASKED = -2e30
LOG2E = 1.4426950408889634
ONES_ROWS = 16
INT_MIN = -2 ** 31

AB_COLS = 4 * 512 + 256 + 256 + 512 + 512 + LANES
CD_COLS = 3 * 512 + 8 * LANES + 2 * LANES + 256 + LANES + LANES


def _dot(a, b):
    return jnp.dot(a, b, preferred_element_type=F32)


def _dot_nt(a, b):
    return lax.dot_general(a, b, (((1,), (1,)), ((), ())), preferred_element_type=F32)


def _rms(x, g, eps=1e-6):
    return x * lax.rsqrt(jnp.mean(x * x, axis=-1, keepdims=True) + eps) * g


def _silu(x):
    return x * jax.nn.sigmoid(x)


def _log_sigmoid(x):
    return jnp.minimum(x, 0.0) - jnp.log1p(jnp.exp(-jnp.abs(x)))


def _split3(x):
    hi = x.astype(BF16)
    r = x - hi.astype(F32)
    mid = r.astype(BF16)
    lo = (r - mid.astype(F32)).astype(BF16)
    return hi, mid, lo


def _tri_dot(tri, x):
    hi, mid, lo = _split3(x)
    return _dot(tri, hi) + _dot(tri, mid) + _dot(tri, lo)


def _iota(shape, dim):
    return lax.broadcasted_iota(I32, shape, dim)


def _const_spec(shape):
    zeros = (0,) * len(shape)
    return pl.BlockSpec(shape, lambda *_: zeros, pipeline_mode=pl.Buffered(1))


def _params(sem):
    return pltpu.CompilerParams(dimension_semantics=sem, vmem_limit_bytes=VMEM_LIMIT)


def _ab_kernel(x_ref, sa_ref, sb_ref, g_ref, win_ref, wgk_ref, bgk_ref, lb_ref, agn_ref, bgn_ref, wout_ref,
               xo_ref, sao_ref, sbo_ref, s_sc, h_sc, o_sc, *, t_valid, t_pad):
    C = REC_CHUNK
    tm = x_ref.shape[1]
    t = pl.program_id(1)

    @pl.when(t == 0)
    def _():
        s_sc[0:4] = sa_ref[0]
        s_sc[4:6] = sb_ref[0]

    h_sc[...] = _dot(_rms(x_ref[0], g_ref[...]).astype(BF16), win_ref[...])

    def chunk_body(c, carry):
        r0 = pl.multiple_of(c * C, C)
        _ab_chunk(r0, t * tm + r0, h_sc, o_sc, s_sc, wgk_ref, bgk_ref, lb_ref, agn_ref, bgn_ref,
                  t_valid=t_valid, t_pad=t_pad)
        return carry

    lax.fori_loop(0, tm // C, chunk_body, 0)
    xo_ref[0] = x_ref[0] + _dot(o_sc[...], wout_ref[...])

    @pl.when(t == pl.num_programs(1) - 1)
    def _():
        sao_ref[0] = s_sc[0:4]
        sbo_ref[0] = s_sc[4:6]


def _ab_chunk(r0, row0, h_sc, o_sc, s_sc, wgk_ref, bgk_ref, lb_ref, agn_ref, bgn_ref, *, t_valid, t_pad):
    C = REC_CHUNK
    cols = lambda a, b: h_sc[pl.ds(r0, C), a:b]
    a_q, a_f, a_i, a_g = cols(0, 512), cols(512, 1024), cols(1024, 1536), cols(1536, 2048)
    b_q, b_k, b_v, b_g = cols(2048, 2304), cols(2304, 2560), cols(2560, 3072), cols(3072, 3584)
    b_lr = cols(3584, 3712)

    lb = lb_ref[...]
    f = lb + (1.0 - lb) * jax.nn.sigmoid(a_f)
    gk = _dot(b_lr.astype(BF16), wgk_ref[...]) + bgk_ref[...]
    la = jnp.concatenate([jnp.log2(f), _log_sigmoid(gk) * (LOG2E / B_GATE_NORM)], axis=1)
    q = jnp.concatenate([_silu(a_q), b_q * (B_DK ** -0.5)], axis=1)
    k = jnp.concatenate([1.0 - f, b_k], axis=1)
    v_a, v_b = a_i, b_v
    if t_valid < t_pad:
        ok = (row0 + _iota((C, 1), 0)) < t_valid
        la = jnp.where(ok, la, 0.0)
        k = jnp.where(ok, k, 0.0)
        v_a = jnp.where(ok, v_a, 0.0)
        v_b = jnp.where(ok, v_b, 0.0)

    row = _iota((C, C), 0)
    col = _iota((C, C), 1)
    causal = col <= row
    tri = jnp.where(causal, 1.0, 0.0).astype(BF16)
    tri_in = jnp.where(causal & ((row >> 4) == (col >> 4)), 1.0, 0.0).astype(BF16)
    bc = _tri_dot(tri, la)
    b_in = _tri_dot(tri_in, la)
    lane = _iota((1, LANES), 1)
    lo_half = lane < 64
    srow = _iota((LANES, 1), 0) < 64

    o_heads = [None] * 8
    for u in range(6):
        sl = slice(u * LANES, (u + 1) * LANES)
        qu, ku, bcu = q[:, sl], k[:, sl], bc[:, sl]
        s_old = s_sc[u]
        bend = bcu[C - 1:C, :]
        qt = qu * jnp.exp2(b_in[:, sl])
        qdec = qu * jnp.exp2(bcu)
        if u < 4:
            heads = [(u, None, v_a[:, sl])]
        else:
            ha = 4 + 2 * (u - 4)
            heads = [(ha, lo_half, v_b[:, (ha - 4) * LANES:(ha - 3) * LANES]),
                     (ha + 1, jnp.logical_not(lo_half), v_b[:, (ha - 3) * LANES:(ha - 2) * LANES])]
        a_rows = [[] for _ in heads]
        for i in range(C // RSUB):
            n = RSUB * (i + 1)
            if i == 0:
                kt = ku[0:n] * jnp.exp2(-bcu[0:n])
            else:
                kt = ku[0:n] * jnp.exp2(bcu[RSUB * i - 1:RSUB * i, :] - bcu[0:n])
            if n < C:
                kt = jnp.concatenate([kt, jnp.zeros((C - n, LANES), F32)], axis=0)
            ktb = kt.astype(BF16)
            qi = qt[RSUB * i:RSUB * (i + 1)]
            for hi_, (_, msk, _) in enumerate(heads):
                qim = qi if msk is None else jnp.where(msk, qi, 0.0)
                a_rows[hi_].append(_dot_nt(qim.astype(BF16), ktb))
        sb16 = s_old.astype(BF16)
        for hi_, (hd, msk, vh) in enumerate(heads):
            att = jnp.where(causal, jnp.concatenate(a_rows[hi_], axis=0), 0.0)
            qd = qdec if msk is None else jnp.where(msk, qdec, 0.0)
            o_heads[hd] = _dot(qd.astype(BF16), sb16) + _dot(att.astype(BF16), vh.astype(BF16))
        kht = (ku * jnp.exp2(bend - bcu)).T
        dcol = jnp.broadcast_to(jnp.exp2(bend), (LANES, LANES)).T
        if u < 4:
            upd = _dot(kht.astype(BF16), heads[0][2].astype(BF16))
        else:
            lhs = jnp.concatenate([jnp.where(srow, kht, 0.0), jnp.where(srow, 0.0, kht)], axis=1)
            rhs = jnp.concatenate([heads[0][2], heads[1][2]], axis=0)
            upd = _dot(lhs.astype(BF16), rhs.astype(BF16))
        s_sc[u] = dcol * s_old + upd

    outs = []
    for hd in range(8):
        if hd < 4:
            gn, gate = agn_ref[...], a_g[:, hd * LANES:(hd + 1) * LANES]
        else:
            gn, gate = bgn_ref[...], b_g[:, (hd - 4) * LANES:(hd - 3) * LANES]
        outs.append(_rms(o_heads[hd], gn) * _silu(gate))
    o_sc[pl.ds(r0, C), :] = jnp.concatenate(outs, axis=1).astype(BF16)


def _ab_layer(x, s_a, s_b, g, w_in, w_gk, b_gk, lb, a_gn, b_gn, w_out, t_valid):
    bsz, t_pad, _ = x.shape
    tm = 512 if t_pad % 512 == 0 else REC_CHUNK
    kern = functools.partial(_ab_kernel, t_valid=t_valid, t_pad=t_pad)
    return pl.pallas_call(
        kern,
        grid=(bsz, t_pad // tm),
        in_specs=[
            pl.BlockSpec((1, tm, D_MODEL), lambda b, t: (b, t, 0)),
            pl.BlockSpec((1, 4, 128, 128), lambda b, t: (b, 0, 0, 0)),
            pl.BlockSpec((1, 2, 128, 128), lambda b, t: (b, 0, 0, 0)),
            _const_spec((1, D_MODEL)),
            _const_spec((D_MODEL, AB_COLS)),
            _const_spec((LANES, 256)),
            _const_spec((1, 256)),
            _const_spec((1, 512)),
            _const_spec((1, 128)),
            _const_spec((1, 128)),
            _const_spec((D_MODEL, D_MODEL)),
        ],
        out_specs=[
            pl.BlockSpec((1, tm, D_MODEL), lambda b, t: (b, t, 0)),
            pl.BlockSpec((1, 4, 128, 128), lambda b, t: (b, 0, 0, 0)),
            pl.BlockSpec((1, 2, 128, 128), lambda b, t: (b, 0, 0, 0)),
        ],
        out_shape=[
            jax.ShapeDtypeStruct((bsz, t_pad, D_MODEL), F32),
            jax.ShapeDtypeStruct((bsz, 4, 128, 128), F32),
            jax.ShapeDtypeStruct((bsz, 2, 128, 128), F32),
        ],
        scratch_shapes=[pltpu.VMEM((6, 128, 128), F32), pltpu.VMEM((tm, AB_COLS), F32),
                        pltpu.VMEM((tm, D_MODEL), BF16)],
        compiler_params=_params(("parallel", "arbitrary")),
        name="ab_layer",
    )(x, s_a, s_b, g, w_in, w_gk, b_gk, lb, a_gn, b_gn, w_out)


FFN_TILE = FFN_HIDDEN // 2


def _ffn_kernel(*refs, has_attn, has_final):
    refs = list(refs)
    x_ref = refs.pop(0)
    x = x_ref[...]
    if has_attn:
        oc_ref, od_ref, wo_ref = refs.pop(0), refs.pop(0), refs.pop(0)
        x = x + _dot(jnp.concatenate([oc_ref[...], od_ref[...]], axis=1), wo_ref[...])
    g_ref, win_ref, wout_ref = refs.pop(0), refs.pop(0), refs.pop(0)
    gf_ref = refs.pop(0) if has_final else None
    out_ref = refs.pop(0)
    xn = _rms(x, g_ref[...]).astype(BF16)
    acc = x
    for j in range(FFN_HIDDEN // FFN_TILE):
        gate = _dot(xn, win_ref[:, j * FFN_TILE:(j + 1) * FFN_TILE])
        up = _dot(xn, win_ref[:, FFN_HIDDEN + j * FFN_TILE:FFN_HIDDEN + (j + 1) * FFN_TILE])
        act = (_silu(gate) * up).astype(BF16)
        acc = acc + _dot(act, wout_ref[j * FFN_TILE:(j + 1) * FFN_TILE, :])
    if has_final:
        acc = _rms(acc, gf_ref[...])
    out_ref[...] = acc


def _ffn(x, g, w_in, w_out, attn=None, g_final=None, tm=512):
    n = x.shape[0]
    tm = min(tm, n)
    row = lambda i: (i, 0)
    args = [x]
    specs = [pl.BlockSpec((tm, D_MODEL), row)]
    if attn is not None:
        oc, od, wo = attn
        args += [oc, od, wo]
        specs += [pl.BlockSpec((tm, 512), row), pl.BlockSpec((tm, 512), row), _const_spec((D_MODEL, D_MODEL))]
    args += [g, w_in, w_out]
    specs += [_const_spec((1, D_MODEL)), _const_spec((D_MODEL, 2 * FFN_HIDDEN)), _const_spec((FFN_HIDDEN, D_MODEL))]
    if g_final is not None:
        args.append(g_final)
        specs.append(_const_spec((1, D_MODEL)))
    kern = functools.partial(_ffn_kernel, has_attn=attn is not None, has_final=g_final is not None)
    return pl.pallas_call(
        kern,
        grid=(n // tm,),
        in_specs=specs,
        out_specs=pl.BlockSpec((tm, D_MODEL), row),
        out_shape=jax.ShapeDtypeStruct((n, D_MODEL), F32),
        compiler_params=_params(("parallel",)),
        name="ffn",
    )(*args)


def _cd_proj_kernel(x_ref, g_ref, w_ref, fb_ref, fq_ref, ck_ref, cv_ref, dq_ref, dk_ref, dv_ref, iq_ref,
                    misc_ref, iw_ref):
    xn = _rms(x_ref[...], g_ref[...]).astype(BF16)
    h = _dot(xn, w_ref[...])
    fq_ref[...] = (h[:, 0:512] * (C_HD ** -0.5 * LOG2E)).astype(BF16)
    ck_ref[...] = h[:, 512:1024]
    cv_ref[...] = h[:, 1024:1536]
    dq_ref[...] = (h[:, 1536:2560] * (D_HD ** -0.5 * LOG2E)).astype(BF16)
    dk_ref[...] = h[:, 2560:2688]
    dv_ref[...] = h[:, 2688:2816]
    iq_ref[...] = h[:, 2816:3072].astype(BF16)
    misc = h[:, 3072:3200]
    lane = _iota((1, LANES), 1)
    is_f = (lane >= IDX_DIM) & (lane < IDX_DIM + C_HEADS)
    misc_ref[...] = jnp.where(is_f, _log_sigmoid(misc + fb_ref[...]), misc)
    iw_ref[...] = h[:, 3200:3328]


def _cd_proj(x, g, w, fb, tm=512):
    n = x.shape[0]
    tm = min(tm, n)
    row = lambda i: (i, 0)
    widths = [(512, BF16), (512, F32), (512, F32), (1024, BF16), (128, F32), (128, F32), (256, BF16),
              (128, F32), (128, F32)]
    return pl.pallas_call(
        _cd_proj_kernel,
        grid=(n // tm,),
        in_specs=[pl.BlockSpec((tm, D_MODEL), row), _const_spec((1, D_MODEL)), _const_spec((D_MODEL, CD_COLS)),
                  _const_spec((1, LANES))],
        out_specs=[pl.BlockSpec((tm, w_), row) for w_, _ in widths],
        out_shape=[jax.ShapeDtypeStruct((n, w_), dt) for w_, dt in widths],
        compiler_params=_params(("parallel",)),
        name="cd_proj",
    )(x, g, w, fb)


def _aug_consts(is_query):
    p = np.zeros((3, LANES, 8 * LANES), np.float32)
    ones = np.zeros((1, 8 * LANES), np.float32)
    for h in range(8):
        off = h * LANES + (64 if h % 2 == 0 else 0)
        for c in range(3):
            if is_query:
                p[c, h, off + c] = 1.0
                ones[0, off + 3 + c] = 1.0
            else:
                p[c, h, off + 3 + c] = -1.0
                ones[0, off + c] = 1.0
    return jnp.asarray(p, BF16), jnp.asarray(ones, F32)


def _aug_lanes(cum, p_ref, ones_ref):
    hi, mid, lo = _split3(cum * LOG2E)
    return _dot(hi, p_ref[0]) + _dot(mid, p_ref[1]) + _dot(lo, p_ref[2]) + ones_ref[...]


def _running_sum(lf, carry):
    tm = lf.shape[0]

    @pl.when(pl.program_id(1) == 0)
    def _():
        carry[...] = jnp.zeros_like(carry)

    tri = jnp.where(_iota((tm, tm), 1) <= _iota((tm, tm), 0), 1.0, 0.0).astype(BF16)
    cum = _tri_dot(tri, lf) + carry[...]
    carry[...] = cum[tm - 1:tm, :]
    return cum


def _fox_rows(kk, vv, aug, ka_ref, vt_ref):
    lane = _iota((1, LANES), 1)
    for h in range(8):
        pr = slice((h // 2) * LANES, (h // 2 + 1) * LANES)
        own = (lane < 64) if h % 2 == 0 else (lane >= 64)
        ka_ref[0, h] = jnp.where(own, kk[:, pr], aug[:, h * LANES:(h + 1) * LANES]).astype(BF16)
    for pr in range(4):
        vt_ref[0, pr, 0] = vv[:, pr * LANES:(pr + 1) * LANES].T.astype(BF16)


def _fox_pack_kernel(k_ref, v_ref, lf_ref, p_ref, ones_ref, ka_ref, vt_ref, cum_ref, carry):
    cum = _running_sum(lf_ref[0], carry)
    cum_ref[0] = cum
    _fox_rows(k_ref[0], v_ref[0], _aug_lanes(cum, p_ref, ones_ref), ka_ref, vt_ref)


def _fox_pack(k, v, lf, tm):
    bsz, s, _ = k.shape
    p, ones = _aug_consts(False)
    return pl.pallas_call(
        _fox_pack_kernel,
        grid=(bsz, s // tm),
        in_specs=[pl.BlockSpec((1, tm, 512), lambda b, t: (b, t, 0)),
                  pl.BlockSpec((1, tm, 512), lambda b, t: (b, t, 0)),
                  pl.BlockSpec((1, tm, LANES), lambda b, t: (b, t, 0)),
                  _const_spec((3, LANES, 8 * LANES)), _const_spec((1, 8 * LANES))],
        out_specs=[pl.BlockSpec((1, 8, tm, LANES), lambda b, t: (b, 0, t, 0)),
                   pl.BlockSpec((1, 4, 1, LANES, tm), lambda b, t: (b, 0, t, 0, 0)),
                   pl.BlockSpec((1, tm, LANES), lambda b, t: (b, t, 0))],
        out_shape=[jax.ShapeDtypeStruct((bsz, 8, s, LANES), BF16),
                   jax.ShapeDtypeStruct((bsz, 4, s // tm, LANES, tm), BF16),
                   jax.ShapeDtypeStruct((bsz, s, LANES), F32)],
        scratch_shapes=[pltpu.VMEM((1, LANES), F32)],
        compiler_params=_params(("parallel", "arbitrary")),
        name="fox_pack",
    )(k, v, lf, p, ones)


def _q_rows(qq, aug, qa_ref):
    lane = _iota((1, LANES), 1)
    for h in range(8):
        pr = slice((h // 2) * LANES, (h // 2 + 1) * LANES)
        own = (lane < 64) if h % 2 == 0 else (lane >= 64)
        qa_ref[0, h] = jnp.where(own, qq[:, pr], aug[:, h * LANES:(h + 1) * LANES]).astype(BF16)


def _q_pack_kernel(q_ref, cum_ref, p_ref, ones_ref, qa_ref):
    _q_rows(q_ref[0].astype(F32), _aug_lanes(cum_ref[0], p_ref, ones_ref), qa_ref)


def _q_pack(q, cum_q, tm):
    bsz, tq, _ = q.shape
    p, ones = _aug_consts(True)
    return pl.pallas_call(
        _q_pack_kernel,
        grid=(bsz, tq // tm),
        in_specs=[pl.BlockSpec((1, tm, 512), lambda b, t: (b, t, 0)),
                  pl.BlockSpec((1, tm, LANES), lambda b, t: (b, t, 0)),
                  _const_spec((3, LANES, 8 * LANES)), _const_spec((1, 8 * LANES))],
        out_specs=pl.BlockSpec((1, 8, tm, LANES), lambda b, t: (b, 0, t, 0)),
        out_shape=jax.ShapeDtypeStruct((bsz, 8, tq, LANES), BF16),
        compiler_params=_params(("parallel", "parallel")),
        name="q_pack",
    )(q, cum_q, p, ones)


def _fox_attn_kernel(q_ref, k_ref, vt_ref, o_ref, s_sc, mt_sc, *, tq, tk, q_start):
    i = pl.program_id(2)
    q_lo = q_start + i * tq
    n_tiles = (q_lo + tq + tk - 1) // tk
    q_pos = q_lo + _iota((1, tq), 1)
    qs = (q_ref[0, 0], q_ref[0, 1])

    def logits(j):
        off = pl.multiple_of(j * tk, tk)
        ok = (off + _iota((tk, 1), 0)) <= q_pos
        ss = [jnp.where(ok, _dot_nt(k_ref[0, hh, pl.ds(off, tk), :], qs[hh]), MASKED) for hh in range(2)]
        return ss, [jnp.max(s, axis=0, keepdims=True) for s in ss]

    def stage(ss, mts):
        for hh in range(2):
            s_sc[hh] = ss[hh]
            mt_sc[hh] = mts[hh]

    stage(*logits(0))

    def body(j, carry):
        nxt = logits(jnp.minimum(j + 1, n_tiles - 1))
        out = []
        for hh in range(2):
            m, acc = carry[hh]
            m_new = jnp.maximum(m, mt_sc[hh])
            p = jnp.exp2(s_sc[hh] - m_new).astype(BF16)
            vt = jnp.concatenate([vt_ref[0, 0, j, hh * 64:(hh + 1) * 64, :], ones], axis=0)
            acc = jnp.exp2(m - m_new) * acc + _dot(vt, p)
            out.append((m_new, acc))
        stage(*nxt)
        return tuple(out)

    ones = jnp.ones((ONES_ROWS, tk), BF16)
    init = (jnp.full((1, tq), NEG, F32), jnp.zeros((64 + ONES_ROWS, tq), F32))
    carry = lax.fori_loop(0, n_tiles, body, (init, init))
    o_t = jnp.concatenate([acc[0:64] / acc[64:65] for _, acc in carry], axis=0)
    o_ref[0] = o_t.T.astype(BF16)


def _fox_attn(q_aug, k_aug, vt, tq, tk, q_start):
    bsz, _, t_q, _ = q_aug.shape
    s = k_aug.shape[2]
    kern = functools.partial(_fox_attn_kernel, tq=tq, tk=tk, q_start=q_start)
    return pl.pallas_call(
        kern,
        grid=(bsz, 4, t_q // tq),
        in_specs=[pl.BlockSpec((1, 2, tq, LANES), lambda b, p, i: (b, p, i, 0)),
                  pl.BlockSpec((1, 2, s, LANES), lambda b, p, i: (b, p, 0, 0)),
                  pl.BlockSpec((1, 1, s // tk, LANES, tk), lambda b, p, i: (b, p, 0, 0, 0))],
        out_specs=pl.BlockSpec((1, tq, LANES), lambda b, p, i: (b, i, p)),
        out_shape=jax.ShapeDtypeStruct((bsz, t_q, 512), BF16),
        scratch_shapes=[pltpu.VMEM((2, tk, tq), F32), pltpu.VMEM((2, 1, tq), F32)],
        compiler_params=_params(("parallel", "parallel", "arbitrary")),
        name="fox_attn",
    )(q_aug, k_aug, vt)


def _dsa_rows(dk, dv, ik, k2_ref, vt_ref, ik2_ref):
    lo = _iota((1, LANES), 1) < 64
    k2_ref[0] = dk.astype(BF16)
    vt_ref[0, 0] = dv.T.astype(BF16)
    ik = jnp.where(lo, ik, 0.0)
    ik2_ref[0, 0] = ik.astype(BF16)
    ik2_ref[0, 1] = pltpu.roll(ik, 64, 1).astype(BF16)


def _dsa_pack_kernel(dk_ref, dv_ref, ik_ref, k2_ref, vt_ref, ik2_ref):
    _dsa_rows(dk_ref[0], dv_ref[0], ik_ref[0], k2_ref, vt_ref, ik2_ref)


def _dsa_pack(dk, dv, ik, tm):
    bsz, s, _ = dk.shape
    tok = pl.BlockSpec((1, tm, LANES), lambda b, t: (b, t, 0))
    return pl.pallas_call(
        _dsa_pack_kernel,
        grid=(bsz, s // tm),
        in_specs=[tok, tok, tok],
        out_specs=[tok,
                   pl.BlockSpec((1, 1, LANES, tm), lambda b, t: (b, t, 0, 0)),
                   pl.BlockSpec((1, 2, tm, LANES), lambda b, t: (b, 0, t, 0))],
        out_shape=[jax.ShapeDtypeStruct((bsz, s, LANES), BF16),
                   jax.ShapeDtypeStruct((bsz, s // tm, LANES, tm), BF16),
                   jax.ShapeDtypeStruct((bsz, 2, s, LANES), BF16)],
        compiler_params=_params(("parallel", "parallel")),
        name="dsa_pack",
    )(dk, dv, ik)


def _cd_fused_kernel(x_ref, g_ref, w_ref, fb_ref, pk_ref, onesk_ref, pq_ref, onesq_ref,
                     ck_ref, cv_ref, dk_ref, dv_ref, misc_ref, iw_ref, iq_ref, dq_ref,
                     qa_ref, ka_ref, vt_ref, k2_ref, dvt_ref, ik2_ref, carry):
    xn = _rms(x_ref[0], g_ref[...]).astype(BF16)
    h = _dot(xn, w_ref[...])
    ck, cv = h[:, 512:1024], h[:, 1024:1536]
    dk, dv = h[:, 2560:2688], h[:, 2688:2816]
    misc = h[:, 3072:3200]
    lane = _iota((1, LANES), 1)
    is_f = (lane >= IDX_DIM) & (lane < IDX_DIM + C_HEADS)
    logf = _log_sigmoid(misc + fb_ref[...])
    misc = jnp.where(is_f, logf, misc)
    ck_ref[0], cv_ref[0], dk_ref[0], dv_ref[0], misc_ref[0] = ck, cv, dk, dv, misc
    iw_ref[0] = h[:, 3200:3328]
    iq_ref[0] = h[:, 2816:3072].astype(BF16)
    dq_ref[0] = (h[:, 1536:2560] * (D_HD ** -0.5 * LOG2E)).astype(BF16)
    cum = _running_sum(pltpu.roll(jnp.where(is_f, logf, 0.0), LANES - IDX_DIM, 1), carry)
    _fox_rows(ck, cv, _aug_lanes(cum, pk_ref, onesk_ref), ka_ref, vt_ref)
    _q_rows(h[:, 0:512] * (C_HD ** -0.5 * LOG2E), _aug_lanes(cum, pq_ref, onesq_ref), qa_ref)
    _dsa_rows(dk, dv, misc, k2_ref, dvt_ref, ik2_ref)


def _cd_fused(x, g, w, fb, tm):
    bsz, t_len, _ = x.shape
    pk, onesk = _aug_consts(False)
    pq, onesq = _aug_consts(True)
    tok = lambda w_: pl.BlockSpec((1, tm, w_), lambda b, t: (b, t, 0))
    hm = pl.BlockSpec((1, 8, tm, LANES), lambda b, t: (b, 0, t, 0))
    sds = jax.ShapeDtypeStruct
    n_t = t_len // tm
    return pl.pallas_call(
        _cd_fused_kernel,
        grid=(bsz, n_t),
        in_specs=[tok(D_MODEL), _const_spec((1, D_MODEL)), _const_spec((D_MODEL, CD_COLS)), _const_spec((1, LANES)),
                  _const_spec((3, LANES, 8 * LANES)), _const_spec((1, 8 * LANES)),
                  _const_spec((3, LANES, 8 * LANES)), _const_spec((1, 8 * LANES))],
        out_specs=[tok(512), tok(512), tok(LANES), tok(LANES), tok(LANES), tok(LANES), tok(256), tok(8 * LANES),
                   hm, hm, pl.BlockSpec((1, 4, 1, LANES, tm), lambda b, t: (b, 0, t, 0, 0)),
                   tok(LANES), pl.BlockSpec((1, 1, LANES, tm), lambda b, t: (b, t, 0, 0)),
                   pl.BlockSpec((1, 2, tm, LANES), lambda b, t: (b, 0, t, 0))],
        out_shape=[sds((bsz, t_len, 512), F32), sds((bsz, t_len, 512), F32), sds((bsz, t_len, LANES), F32),
                   sds((bsz, t_len, LANES), F32), sds((bsz, t_len, LANES), F32), sds((bsz, t_len, LANES), F32),
                   sds((bsz, t_len, 256), BF16), sds((bsz, t_len, 8 * LANES), BF16),
                   sds((bsz, 8, t_len, LANES), BF16), sds((bsz, 8, t_len, LANES), BF16),
                   sds((bsz, 4, n_t, LANES, tm), BF16), sds((bsz, t_len, LANES), BF16),
                   sds((bsz, n_t, LANES, tm), BF16), sds((bsz, 2, t_len, LANES), BF16)],
        scratch_shapes=[pltpu.VMEM((1, LANES), F32)],
        compiler_params=_params(("parallel", "arbitrary")),
        name="cd_fused",
    )(x, g, w, fb, pk, onesk, pq, onesq)


def _dsa_attn_kernel(dq_ref, iq_ref, iw_ref, ik2_ref, k2_ref, vt_ref, o_ref, keys_sc, hi_sc, lo_sc, eq_sc, m_sc, mt_sc, acc_sc,
                     s_sc,
                     *, tq, tk, q_start, s_valid, topk):
    i = pl.program_id(1)
    q_lo = q_start + i * tq
    adm_row = jnp.minimum(((q_lo + _iota((1, tq), 1)) // CHUNK + 1) * CHUNK, s_valid)
    adm_end = jnp.minimum(((q_lo + tq - 1) // CHUNK + 1) * CHUNK, s_valid)
    n_tiles = (adm_end + tk - 1) // tk
    iw_t = iw_ref[0].T
    iq = iq_ref[0]

    def score_body(j, c):
        off = pl.multiple_of(j * tk, tk)
        sc = jnp.zeros((tk, tq), F32)
        for hd in range(IDX_HEADS):
            qp = iq[:, (hd // 2) * LANES:(hd // 2 + 1) * LANES]
            r = jnp.maximum(_dot_nt(ik2_ref[0, hd % 2, pl.ds(off, tk), :], qp), 0.0)
            sc = sc + r * iw_t[hd:hd + 1, :]
        sc = sc * IDX_SCALE
        adm = (off + _iota((tk, tq), 0)) < adm_row
        bits = lax.bitcast_convert_type(sc, I32)
        bits = jnp.where(bits == INT_MIN, 0, bits)
        key = bits ^ ((bits >> 31) & 0x7FFFFFFF)
        keys_sc[j] = jnp.where(adm, key, INT_MIN)
        hi = lax.bitcast_convert_type(bits & -65536, F32)
        hi_sc[j] = jnp.where(adm, hi, -jnp.inf).astype(BF16)
        return c

    lax.fori_loop(0, n_tiles, score_body, 0)

    def pairs(one, init):
        part = lax.fori_loop(0, n_tiles // 2, lambda jj, c: one(2 * jj + 1, one(2 * jj, c)), init)
        return lax.fori_loop(2 * (n_tiles // 2), n_tiles, one, part)

    one16 = jnp.ones((tk, tq), BF16)
    zero16 = jnp.zeros((tk, tq), BF16)

    def hi_body(b, pre):
        cand = pre + lax.shift_left(jnp.int32(1), 15 - b)
        cbits = (cand ^ ((cand >> 15) & 0x7FFF)) & 0xFFFF
        subnormal = ((cbits & 0x7F80) == 0) & ((cbits & 0x007F) != 0)
        cbits = jnp.where(subnormal, jnp.where((cbits & 0x8000) != 0, 0x0000, 0x0080), cbits)
        cval = jnp.broadcast_to(lax.bitcast_convert_type(cbits << 16, F32).astype(BF16), (tk, tq))

        def one(j, c):
            hit = jnp.where(hi_sc[j] >= cval, one16, zero16).reshape(tk // 64, 4, 16, tq)
            for r in range(tk // 64):
                c = c + hit[r]
            return c
        part = pairs(one, jnp.zeros((4, 16, tq), BF16)).astype(F32)
        cnt = jnp.sum(jnp.sum(part, axis=0), axis=0, keepdims=True)
        return jnp.where(cnt >= topk, cand, pre)

    pre = lax.fori_loop(0, 16, hi_body, jnp.full((1, tq), -2 ** 15, I32))

    def count(pred):
        def one(j, c):
            hit = jnp.where(pred(j, keys_sc[j]), 1.0, 0.0)
            return c + jnp.sum(hit.reshape(tk // 32, 4, 8, tq), axis=0)

        part = pairs(one, jnp.zeros((4, 8, tq), F32))
        return jnp.sum(jnp.sum(part, axis=0), axis=0, keepdims=True)

    def lo_prep(j, c):
        kk = keys_sc[j]
        lo_sc[j] = jnp.where((kk >> 16) == pre, (kk & 0xFFFF) - 2 ** 15, -2 ** 15).astype(jnp.int16)
        hit = jnp.where((kk >> 16) > pre, 1.0, 0.0)
        return c + jnp.sum(hit.reshape(tk // 32, 4, 8, tq), axis=0)

    above = lax.fori_loop(0, n_tiles, lo_prep, jnp.zeros((4, 8, tq), F32))
    above = jnp.sum(jnp.sum(above, axis=0), axis=0, keepdims=True)
    one_i16 = jnp.ones((tk, tq), jnp.int16)
    zero_i16 = jnp.zeros((tk, tq), jnp.int16)

    def lo_body(b, low):
        cand = low + lax.shift_left(jnp.int32(1), 15 - b)
        cval = jnp.broadcast_to(cand.astype(jnp.int16), (tk, tq))

        def one(j, c):
            hit = jnp.where(lo_sc[j] >= cval, one_i16, zero_i16).reshape(tk // 64, 4, 16, tq)
            for r in range(tk // 64):
                c = c + hit[r]
            return c
        part = pairs(one, jnp.zeros((4, 16, tq), jnp.int16)).astype(I32).astype(F32)
        cnt = above + jnp.sum(jnp.sum(part, axis=0), axis=0, keepdims=True)
        return jnp.where(cnt >= topk, cand, low)

    low = lax.fori_loop(0, 16, lo_body, jnp.full((1, tq), -2 ** 15, I32))
    tau = (pre << 16) + (low + 2 ** 15)
    tau = jnp.where(adm_row < int(topk), INT_MIN, tau)
    need = jnp.where(tau == INT_MIN, -1.0, topk - count(lambda j, kk: kk > tau))
    eq_sc[...] = jnp.zeros(eq_sc.shape, F32)
    tri = jnp.where(_iota((tk, tk), 1) <= _iota((tk, tk), 0), 1.0, 0.0).astype(BF16)

    m_sc[...] = jnp.full(m_sc.shape, NEG, F32)
    acc_sc[...] = jnp.zeros(acc_sc.shape, F32)
    ones = jnp.ones((ONES_ROWS, tk), BF16)
    n_rep = D_HEADS // D_KV_HEADS
    q_stack = [jnp.concatenate([dq_ref[0, :, hd * LANES:(hd + 1) * LANES]
                                for hd in range(g * n_rep, (g + 1) * n_rep)], axis=0) for g in range(D_KV_HEADS)]

    def logits(j):
        kk = keys_sc[j]
        eq = kk == tau
        run = _dot(tri, jnp.where(eq, 1.0, 0.0).astype(BF16)) + eq_sc[...]
        eq_sc[...] = run[tk - 1:tk, :]
        sel = (kk > tau) | (eq & (run <= need))
        bias = jnp.where(sel, 0.0, MASKED)
        bias = jnp.concatenate([bias] * n_rep, axis=1)
        kt = k2_ref[0, pl.ds(pl.multiple_of(j * tk, tk), tk), :]
        ss = [_dot_nt(kt, q_stack[g]) + bias for g in range(D_KV_HEADS)]
        return ss, [jnp.max(s, axis=0, keepdims=True) for s in ss]

    def stage(slot, ss, mts):
        for g in range(D_KV_HEADS):
            s_sc[slot, g] = ss[g]
            mt_sc[slot, g] = mts[g]

    stage(0, *logits(0))

    def step(j, cur):
        stage(1 - cur, *logits(jnp.minimum(j + 1, n_tiles - 1)))
        for g in range(D_KV_HEADS):
            m_old = m_sc[g]
            m_new = jnp.maximum(m_old, mt_sc[cur, g])
            p = jnp.exp2(s_sc[cur, g] - m_new).astype(BF16)
            vt = jnp.concatenate([vt_ref[0, j, g * 64:(g + 1) * 64, :], ones], axis=0)
            acc_sc[g] = jnp.exp2(m_old - m_new) * acc_sc[g] + _dot(vt, p)
            m_sc[g] = m_new

    def attn_body(jj, c):
        step(2 * jj, 0)

        @pl.when(2 * jj + 1 < n_tiles)
        def _():
            step(2 * jj + 1, 1)
        return c

    lax.fori_loop(0, (n_tiles + 1) // 2, attn_body, 0)
    for g in range(D_KV_HEADS):
        o_g = acc_sc[g, 0:64] / acc_sc[g, 64:65]
        for pr in range(n_rep // 2):
            o_t = jnp.concatenate([o_g[:, (2 * pr) * tq:(2 * pr + 1) * tq],
                                   o_g[:, (2 * pr + 1) * tq:(2 * pr + 2) * tq]], axis=0)
            col = (g * n_rep // 2 + pr) * LANES
            o_ref[0, :, col:col + LANES] = o_t.T.astype(BF16)


def _dsa_attn(dq, iq, iw, ik2, k2, vt, tq, tk, q_start, s_valid):
    bsz, t_q, _ = dq.shape
    s = k2.shape[1]
    topk = min(IDX_TOPK_MAX, s_valid // 4)
    assert (s // tk) * (tk // 64) <= 256, "packed bf16 partial counts must stay exactly representable"
    kern = functools.partial(_dsa_attn_kernel, tq=tq, tk=tk, q_start=q_start, s_valid=s_valid, topk=float(topk))
    return pl.pallas_call(
        kern,
        grid=(bsz, t_q // tq),
        in_specs=[pl.BlockSpec((1, tq, 8 * LANES), lambda b, i: (b, i, 0)),
                  pl.BlockSpec((1, tq, 256), lambda b, i: (b, i, 0)),
                  pl.BlockSpec((1, tq, LANES), lambda b, i: (b, i, 0)),
                  pl.BlockSpec((1, 2, s, LANES), lambda b, i: (b, 0, 0, 0)),
                  pl.BlockSpec((1, s, LANES), lambda b, i: (b, 0, 0)),
                  pl.BlockSpec((1, s // tk, LANES, tk), lambda b, i: (b, 0, 0, 0))],
        out_specs=pl.BlockSpec((1, tq, 512), lambda b, i: (b, i, 0)),
        out_shape=jax.ShapeDtypeStruct((bsz, t_q, 512), BF16),
        scratch_shapes=[pltpu.VMEM((s // tk, tk, tq), I32),
                        pltpu.VMEM((s // tk, tk, tq), BF16),
                        pltpu.VMEM((s // tk, tk, tq), jnp.int16),
                        pltpu.VMEM((1, tq), F32),
                        pltpu.VMEM((D_KV_HEADS, 1, D_HEADS // D_KV_HEADS * tq), F32),
                        pltpu.VMEM((2, D_KV_HEADS, 1, D_HEADS // D_KV_HEADS * tq), F32),
                        pltpu.VMEM((D_KV_HEADS, 64 + ONES_ROWS, D_HEADS // D_KV_HEADS * tq), F32),
                        pltpu.VMEM((2, D_KV_HEADS, tk, D_HEADS // D_KV_HEADS * tq), F32)],
        compiler_params=_params(("parallel", "arbitrary")),
        name="dsa_attn",
    )(dq, iq, iw, ik2, k2, vt)


def _prep_ab(w_in, w_gk):
    w = jnp.pad(w_in, ((0, 0), (0, AB_COLS - w_in.shape[1]))).astype(BF16)
    wgk = jnp.pad(w_gk, ((0, LANES - B_GATE_RANK), (0, 0))).astype(BF16)
    return w, wgk


def _prep_cd(w_in, f_bias):
    o = np.cumsum([0, 512, 512, 512, C_HEADS, 512, 128, 128, 256, IDX_DIM, IDX_HEADS])
    c_q, c_k, c_v, c_f, d_q, d_k, d_v, d_iq, d_ik, d_iw = (w_in[:, o[i]:o[i + 1]] for i in range(10))
    zeros = lambda n: jnp.zeros((D_MODEL, n), w_in.dtype)
    dq_cols = []
    for h in range(D_HEADS):
        g = h // (D_HEADS // D_KV_HEADS)
        wh = d_q[:, h * 64:(h + 1) * 64]
        dq_cols += [wh, zeros(64)] if g == 0 else [zeros(64), wh]
    w = jnp.concatenate([c_q, c_k, c_v] + dq_cols + [d_k, d_v, d_iq, d_ik, c_f, zeros(LANES - IDX_DIM - C_HEADS),
                                                     d_iw, zeros(LANES - IDX_HEADS)], axis=1).astype(BF16)
    fb = jnp.pad(f_bias.astype(F32), (IDX_DIM, LANES - IDX_DIM - C_HEADS)).reshape(1, LANES)
    return w, fb


def _pad_rows(z, s):
    return jnp.pad(z, ((0, 0), (0, s - z.shape[1]), (0, 0)))


def _pad_lanes(z, n=LANES):
    return jnp.pad(z, ((0, 0), (0, 0), (0, n - z.shape[2])))


def _trunk(x, s_a, s_b, cache, wts):
    bsz, t_len, _ = x.shape
    row = lambda z: z.reshape(1, -1).astype(F32)

    t_pad = -(-t_len // REC_CHUNK) * REC_CHUNK
    x1, sa_new, sb_new = _ab_layer(
        _pad_rows(x, t_pad), s_a, s_b.reshape(bsz, 2, 128, 128), row(wts['norm_mix'][0]), wts['ab_w_in'],
        wts['gla_w_gk'], row(wts['gla_b_gk']), row(wts['lb']), row(wts['hgrn_gnorm']), row(wts['gla_gnorm']),
        wts['ab_w_out'], t_len)
    n = bsz * t_len
    x1 = x1[:, :t_len].reshape(n, D_MODEL)
    x2 = _ffn(x1, row(wts['norm_ffn'][0]), wts['ffn_w_in'][0], wts['ffn_w_out'][0])

    tk = 512
    q_start = 0 if cache is None else cache[0].shape[1]
    s_valid = q_start + t_len
    tq_pad = -(-t_len // LANES) * LANES
    if cache is None and t_len % tk == 0:
        (ck, cv, dk, dv, misc, iw, iq, dq, q_aug, k_aug, v_t, k2, dv_t, ik2) = _cd_fused(
            x2.reshape(bsz, t_len, D_MODEL), row(wts['norm_mix'][1]), wts['cd_w_in'], wts['fox_f_bias'], tm=tk)
        logf = misc[:, :, IDX_DIM:IDX_DIM + C_HEADS]
    else:
        fq, ck, cv, dq, dk, dv, iq, misc, iw = _cd_proj(x2, row(wts['norm_mix'][1]), wts['cd_w_in'],
                                                        wts['fox_f_bias'])
        per_b = lambda z: z.reshape(bsz, t_len, z.shape[-1])
        fq, ck, cv, dq, dk, dv, iq, misc, iw = map(per_b, (fq, ck, cv, dq, dk, dv, iq, misc, iw))
        logf = misc[:, :, IDX_DIM:IDX_DIM + C_HEADS]
        if cache is None:
            k_all, v_all, lf_all, dk_all, dv_all, ik_all = ck, cv, _pad_lanes(logf), dk, dv, misc
        else:
            c_k, c_v, c_lf, c_dk, c_dv, c_ik = cache
            cat = lambda c, r: jnp.concatenate([c.reshape(bsz, q_start, -1).astype(F32), r], axis=1)
            k_all, v_all, dk_all, dv_all = cat(c_k, ck), cat(c_v, cv), cat(c_dk, dk), cat(c_dv, dv)
            lf_all = _pad_lanes(cat(c_lf, logf))
            ik_all = cat(_pad_lanes(c_ik), misc)
        s_pad = -(-s_valid // tk) * tk
        k_all, v_all, lf_all, dk_all, dv_all, ik_all = (_pad_rows(z, s_pad) for z in
                                                        (k_all, v_all, lf_all, dk_all, dv_all, ik_all))
        k_aug, v_t, cum = _fox_pack(k_all, v_all, lf_all, tm=tk)
        q_aug = _q_pack(_pad_rows(fq, tq_pad), _pad_rows(cum[:, q_start:q_start + t_len], tq_pad),
                        tm=512 if tq_pad % 512 == 0 else LANES)
        k2, dv_t, ik2 = _dsa_pack(dk_all, dv_all, ik_all, tm=tk)
    o_c = _fox_attn(q_aug, k_aug, v_t, tq=512 if tq_pad % 512 == 0 else LANES, tk=tk, q_start=q_start)[:, :t_len]
    o_d = _dsa_attn(_pad_rows(dq, tq_pad), _pad_rows(iq, tq_pad), _pad_rows(iw, tq_pad), ik2, k2, dv_t,
                    tq=256 if tq_pad % 256 == 0 else LANES, tk=tk, q_start=q_start, s_valid=s_valid)[:, :t_len]

    y = _ffn(x2, row(wts['norm_ffn'][1]), wts['ffn_w_in'][1], wts['ffn_w_out'][1],
             attn=(o_c.reshape(n, 512), o_d.reshape(n, 512), wts['cd_w_out']), g_final=row(wts['norm_final']))
    rows = (ck.reshape(1, bsz, t_len, C_HEADS, C_HD), cv.reshape(1, bsz, t_len, C_HEADS, C_HD),
            logf[None], dk.reshape(1, bsz, t_len, D_KV_HEADS, D_HD), dv.reshape(1, bsz, t_len, D_KV_HEADS, D_HD),
            misc[None, :, :, :IDX_DIM])
    return (y.reshape(bsz, t_len, D_MODEL), sa_new[None], sb_new.reshape(1, bsz, B_HEADS, B_DK, B_DV)) + rows


def kernel(x_prompt, x_sample, state_hgrn, state_gla, cache_fox_k, cache_fox_v, cache_fox_logf, cache_dsa_k, cache_dsa_v, cache_dsa_ik, norm_mix, norm_ffn, norm_final, ab_w_in, ab_w_out, hgrn_lb_logits, hgrn_gnorm, gla_w_gk, gla_b_gk, gla_gnorm, cd_w_in, cd_w_out, fox_f_bias, ffn_w_in, ffn_w_out):
    lbs = jnp.cumsum(jax.nn.softmax(hgrn_lb_logits.astype(F32), axis=0), axis=0)
    w_ab, w_gk = _prep_ab(ab_w_in[0], gla_w_gk[0])
    w_cd, fb = _prep_cd(cd_w_in[0], fox_f_bias[0])
    wts = dict(norm_mix=norm_mix, norm_ffn=norm_ffn, norm_final=norm_final, ab_w_in=w_ab,
               ab_w_out=ab_w_out[0].astype(BF16), lb=lbs[0], hgrn_gnorm=hgrn_gnorm[0], gla_w_gk=w_gk,
               gla_b_gk=gla_b_gk[0], gla_gnorm=gla_gnorm[0], cd_w_in=w_cd, cd_w_out=cd_w_out[0].astype(BF16),
               fox_f_bias=fb, ffn_w_in=ffn_w_in.astype(BF16), ffn_w_out=ffn_w_out.astype(BF16))
    bp = x_prompt.shape[0]
    p_out = _trunk(x_prompt, jnp.zeros((bp, A_HEADS, A_DK, A_DV), F32), jnp.zeros((bp, B_HEADS, B_DK, B_DV), F32),
                   None, wts)
    cache = (cache_fox_k[0], cache_fox_v[0], cache_fox_logf[0], cache_dsa_k[0], cache_dsa_v[0], cache_dsa_ik[0])
    s_out = _trunk(x_sample, state_hgrn[0], state_gla[0], cache, wts)
    return (p_out[0], s_out[0]) + tuple(p_out[1:]) + tuple(s_out[1:])
```

```python
import functools

import numpy as np
import jax
import jax.numpy as jnp
from jax import lax
from jax.experimental import pallas as pl
from jax.experimental.pallas import tpu as pltpu

F32 = jnp.float32
BF16 = jnp.bfloat16
I32 = jnp.int32

D_MODEL = 1024
CHUNK = 64
A_HEADS, A_DK, A_DV = 4, 128, 128
B_HEADS, B_DK, B_DV = 4, 64, 128
B_GATE_RANK = 16
B_GATE_NORM = 16.0
C_HEADS, C_HD = 8, 64
D_HEADS, D_KV_HEADS, D_HD = 8, 2, 64
IDX_HEADS, IDX_DIM = 4, 64
IDX_TOPK_MAX = 256
IDX_SCALE = (IDX_DIM ** -0.5) * (IDX_HEADS ** -0.5)
FFN_HIDDEN = ((8 * D_MODEL // 3 + 255) // 256) * 256

LANES = 128
RSUB = 16
REC_CHUNK = 128
VMEM_LIMIT = 56 * 1024 * 1024
NEG = -1e30
MASKED = -2e30
LOG2E = 1.4426950408889634
ONES_ROWS = 16
INT_MIN = -2 ** 31

AB_COLS = 4 * 512 + 256 + 256 + 512 + 512 + LANES
CD_COLS = 3 * 512 + 8 * LANES + 2 * LANES + 256 + LANES + LANES


def _dot(a, b):
    return jnp.dot(a, b, preferred_element_type=F32)


def _dot_nt(a, b):
    return lax.dot_general(a, b, (((1,), (1,)), ((), ())), preferred_element_type=F32)


def _rms(x, g, eps=1e-6):
    return x * lax.rsqrt(jnp.mean(x * x, axis=-1, keepdims=True) + eps) * g


def _silu(x):
    return x * jax.nn.sigmoid(x)


def _log_sigmoid(x):
    return jnp.minimum(x, 0.0) - jnp.log1p(jnp.exp(-jnp.abs(x)))


def _split3(x):
    hi = x.astype(BF16)
    r = x - hi.astype(F32)
    mid = r.astype(BF16)
    lo = (r - mid.astype(F32)).astype(BF16)
    return hi, mid, lo


def _tri_dot(tri, x):
    hi, mid, lo = _split3(x)
    return _dot(tri, hi) + _dot(tri, mid) + _dot(tri, lo)


def _iota(shape, dim):
    return lax.broadcasted_iota(I32, shape, dim)


def _const_spec(shape):
    zeros = (0,) * len(shape)
    return pl.BlockSpec(shape, lambda *_: zeros, pipeline_mode=pl.Buffered(1))


def _params(sem):
    return pltpu.CompilerParams(dimension_semantics=sem, vmem_limit_bytes=VMEM_LIMIT)


def _ab_kernel(x_ref, sa_ref, sb_ref, g_ref, win_ref, wgk_ref, bgk_ref, lb_ref, agn_ref, bgn_ref, wout_ref,
               xo_ref, sao_ref, sbo_ref, s_sc, h_sc, o_sc, *, t_valid, t_pad):
    C = REC_CHUNK
    tm = x_ref.shape[1]
    t = pl.program_id(1)

    @pl.when(t == 0)
    def _():
        s_sc[0:4] = sa_ref[0]
        s_sc[4:6] = sb_ref[0]

    h_sc[...] = _dot(_rms(x_ref[0], g_ref[...]).astype(BF16), win_ref[...])

    def chunk_body(c, carry):
        r0 = pl.multiple_of(c * C, C)
        _ab_chunk(r0, t * tm + r0, h_sc, o_sc, s_sc, wgk_ref, bgk_ref, lb_ref, agn_ref, bgn_ref,
                  t_valid=t_valid, t_pad=t_pad)
        return carry

    lax.fori_loop(0, tm // C, chunk_body, 0)
    xo_ref[0] = x_ref[0] + _dot(o_sc[...], wout_ref[...])

    @pl.when(t == pl.num_programs(1) - 1)
    def _():
        sao_ref[0] = s_sc[0:4]
        sbo_ref[0] = s_sc[4:6]


def _ab_chunk(r0, row0, h_sc, o_sc, s_sc, wgk_ref, bgk_ref, lb_ref, agn_ref, bgn_ref, *, t_valid, t_pad):
    C = REC_CHUNK
    cols = lambda a, b: h_sc[pl.ds(r0, C), a:b]
    a_q, a_f, a_i, a_g = cols(0, 512), cols(512, 1024), cols(1024, 1536), cols(1536, 2048)
    b_q, b_k, b_v, b_g = cols(2048, 2304), cols(2304, 2560), cols(2560, 3072), cols(3072, 3584)
    b_lr = cols(3584, 3712)

    lb = lb_ref[...]
    f = lb + (1.0 - lb) * jax.nn.sigmoid(a_f)
    gk = _dot(b_lr.astype(BF16), wgk_ref[...]) + bgk_ref[...]
    la = jnp.concatenate([jnp.log2(f), _log_sigmoid(gk) * (LOG2E / B_GATE_NORM)], axis=1)
    q = jnp.concatenate([_silu(a_q), b_q * (B_DK ** -0.5)], axis=1)
    k = jnp.concatenate([1.0 - f, b_k], axis=1)
    v_a, v_b = a_i, b_v
    if t_valid < t_pad:
        ok = (row0 + _iota((C, 1), 0)) < t_valid
        la = jnp.where(ok, la, 0.0)
        k = jnp.where(ok, k, 0.0)
        v_a = jnp.where(ok, v_a, 0.0)
        v_b = jnp.where(ok, v_b, 0.0)

    row = _iota((C, C), 0)
    col = _iota((C, C), 1)
    causal = col <= row
    tri = jnp.where(causal, 1.0, 0.0).astype(BF16)
    tri_in = jnp.where(causal & ((row >> 4) == (col >> 4)), 1.0, 0.0).astype(BF16)
    bc = _tri_dot(tri, la)
    b_in = _tri_dot(tri_in, la)
    lane = _iota((1, LANES), 1)
    lo_half = lane < 64
    srow = _iota((LANES, 1), 0) < 64

    o_heads = [None] * 8
    for u in range(6):
        sl = slice(u * LANES, (u + 1) * LANES)
        qu, ku, bcu = q[:, sl], k[:, sl], bc[:, sl]
        s_old = s_sc[u]
        bend = bcu[C - 1:C, :]
        qt = qu * jnp.exp2(b_in[:, sl])
        qdec = qu * jnp.exp2(bcu)
        if u < 4:
            heads = [(u, None, v_a[:, sl])]
        else:
            ha = 4 + 2 * (u - 4)
            heads = [(ha, lo_half, v_b[:, (ha - 4) * LANES:(ha - 3) * LANES]),
                     (ha + 1, jnp.logical_not(lo_half), v_b[:, (ha - 3) * LANES:(ha - 2) * LANES])]
        a_rows = [[] for _ in heads]
        for i in range(C // RSUB):
            n = RSUB * (i + 1)
            if i == 0:
                kt = ku[0:n] * jnp.exp2(-bcu[0:n])
            else:
                kt = ku[0:n] * jnp.exp2(bcu[RSUB * i - 1:RSUB * i, :] - bcu[0:n])
            if n < C:
                kt = jnp.concatenate([kt, jnp.zeros((C - n, LANES), F32)], axis=0)
            ktb = kt.astype(BF16)
            qi = qt[RSUB * i:RSUB * (i + 1)]
            lhs = jnp.concatenate([qi if msk is None else jnp.where(msk, qi, 0.0) for _, msk, _ in heads], axis=0)
            res = _dot_nt(lhs.astype(BF16), ktb)
            for hi_ in range(len(heads)):
                a_rows[hi_].append(res[RSUB * hi_:RSUB * (hi_ + 1)])
        qd = jnp.concatenate([qdec if msk is None else jnp.where(msk, qdec, 0.0) for _, msk, _ in heads], axis=0)
        o_inter = _dot(qd.astype(BF16), s_old.astype(BF16))
        for hi_, (hd, msk, vh) in enumerate(heads):
            att = jnp.where(causal, jnp.concatenate(a_rows[hi_], axis=0), 0.0)
            o_heads[hd] = o_inter[C * hi_:C * (hi_ + 1)] + _dot(att.astype(BF16), vh.astype(BF16))
        kht = (ku * jnp.exp2(bend - bcu)).T
        dcol = jnp.broadcast_to(jnp.exp2(bend), (LANES, LANES)).T
        if u < 4:
            upd = _dot(kht.astype(BF16), heads[0][2].astype(BF16))
        else:
            lhs = jnp.concatenate([jnp.where(srow, kht, 0.0), jnp.where(srow, 0.0, kht)], axis=1)
            rhs = jnp.concatenate([heads[0][2], heads[1][2]], axis=0)
            upd = _dot(lhs.astype(BF16), rhs.astype(BF16))
        s_sc[u] = dcol * s_old + upd

    outs = []
    for hd in range(8):
        if hd < 4:
            gn, gate = agn_ref[...], a_g[:, hd * LANES:(hd + 1) * LANES]
        else:
            gn, gate = bgn_ref[...], b_g[:, (hd - 4) * LANES:(hd - 3) * LANES]
        outs.append(_rms(o_heads[hd], gn) * _silu(gate))
    o_sc[pl.ds(r0, C), :] = jnp.concatenate(outs, axis=1).astype(BF16)


def _ab_layer(x, s_a, s_b, g, w_in, w_gk, b_gk, lb, a_gn, b_gn, w_out, t_valid):
    bsz, t_pad, _ = x.shape
    tm = 512 if t_pad % 512 == 0 else REC_CHUNK
    kern = functools.partial(_ab_kernel, t_valid=t_valid, t_pad=t_pad)
    return pl.pallas_call(
        kern,
        grid=(bsz, t_pad // tm),
        in_specs=[
            pl.BlockSpec((1, tm, D_MODEL), lambda b, t: (b, t, 0)),
            pl.BlockSpec((1, 4, 128, 128), lambda b, t: (b, 0, 0, 0)),
            pl.BlockSpec((1, 2, 128, 128), lambda b, t: (b, 0, 0, 0)),
            _const_spec((1, D_MODEL)),
            _const_spec((D_MODEL, AB_COLS)),
            _const_spec((LANES, 256)),
            _const_spec((1, 256)),
            _const_spec((1, 512)),
            _const_spec((1, 128)),
            _const_spec((1, 128)),
            _const_spec((D_MODEL, D_MODEL)),
        ],
        out_specs=[
            pl.BlockSpec((1, tm, D_MODEL), lambda b, t: (b, t, 0)),
            pl.BlockSpec((1, 4, 128, 128), lambda b, t: (b, 0, 0, 0)),
            pl.BlockSpec((1, 2, 128, 128), lambda b, t: (b, 0, 0, 0)),
        ],
        out_shape=[
            jax.ShapeDtypeStruct((bsz, t_pad, D_MODEL), F32),
            jax.ShapeDtypeStruct((bsz, 4, 128, 128), F32),
            jax.ShapeDtypeStruct((bsz, 2, 128, 128), F32),
        ],
        scratch_shapes=[pltpu.VMEM((6, 128, 128), F32), pltpu.VMEM((tm, AB_COLS), F32),
                        pltpu.VMEM((tm, D_MODEL), BF16)],
        compiler_params=_params(("parallel", "arbitrary")),
        name="ab_layer",
    )(x, s_a, s_b, g, w_in, w_gk, b_gk, lb, a_gn, b_gn, w_out)


FFN_TILE = FFN_HIDDEN // 2


def _ffn_kernel(*refs, has_attn, has_final):
    refs = list(refs)
    x_ref = refs.pop(0)
    x = x_ref[...]
    if has_attn:
        oc_ref, od_ref, wo_ref = refs.pop(0), refs.pop(0), refs.pop(0)
        x = x + _dot(jnp.concatenate([oc_ref[...], od_ref[...]], axis=1), wo_ref[...])
    g_ref, win_ref, wout_ref = refs.pop(0), refs.pop(0), refs.pop(0)
    gf_ref = refs.pop(0) if has_final else None
    out_ref = refs.pop(0)
    xn = _rms(x, g_ref[...]).astype(BF16)
    acc = x
    for j in range(FFN_HIDDEN // FFN_TILE):
        gate = _dot(xn, win_ref[:, j * FFN_TILE:(j + 1) * FFN_TILE])
        up = _dot(xn, win_ref[:, FFN_HIDDEN + j * FFN_TILE:FFN_HIDDEN + (j + 1) * FFN_TILE])
        act = (_silu(gate) * up).astype(BF16)
        acc = acc + _dot(act, wout_ref[j * FFN_TILE:(j + 1) * FFN_TILE, :])
    if has_final:
        acc = _rms(acc, gf_ref[...])
    out_ref[...] = acc


def _ffn(x, g, w_in, w_out, attn=None, g_final=None, tm=512):
    n = x.shape[0]
    tm = min(tm, n)
    row = lambda i: (i, 0)
    args = [x]
    specs = [pl.BlockSpec((tm, D_MODEL), row)]
    if attn is not None:
        oc, od, wo = attn
        args += [oc, od, wo]
        specs += [pl.BlockSpec((tm, 512), row), pl.BlockSpec((tm, 512), row), _const_spec((D_MODEL, D_MODEL))]
    args += [g, w_in, w_out]
    specs += [_const_spec((1, D_MODEL)), _const_spec((D_MODEL, 2 * FFN_HIDDEN)), _const_spec((FFN_HIDDEN, D_MODEL))]
    if g_final is not None:
        args.append(g_final)
        specs.append(_const_spec((1, D_MODEL)))
    kern = functools.partial(_ffn_kernel, has_attn=attn is not None, has_final=g_final is not None)
    return pl.pallas_call(
        kern,
        grid=(n // tm,),
        in_specs=specs,
        out_specs=pl.BlockSpec((tm, D_MODEL), row),
        out_shape=jax.ShapeDtypeStruct((n, D_MODEL), F32),
        compiler_params=_params(("parallel",)),
        name="ffn",
    )(*args)


def _cd_proj_kernel(x_ref, g_ref, w_ref, fb_ref, fq_ref, ck_ref, cv_ref, dq_ref, dk_ref, dv_ref, iq_ref,
                    misc_ref, iw_ref):
    xn = _rms(x_ref[...], g_ref[...]).astype(BF16)
    h = _dot(xn, w_ref[...])
    fq_ref[...] = (h[:, 0:512] * (C_HD ** -0.5 * LOG2E)).astype(BF16)
    ck_ref[...] = h[:, 512:1024]
    cv_ref[...] = h[:, 1024:1536]
    dq_ref[...] = (h[:, 1536:2560] * (D_HD ** -0.5 * LOG2E)).astype(BF16)
    dk_ref[...] = h[:, 2560:2688]
    dv_ref[...] = h[:, 2688:2816]
    iq_ref[...] = h[:, 2816:3072].astype(BF16)
    misc = h[:, 3072:3200]
    lane = _iota((1, LANES), 1)
    is_f = (lane >= IDX_DIM) & (lane < IDX_DIM + C_HEADS)
    misc_ref[...] = jnp.where(is_f, _log_sigmoid(misc + fb_ref[...]), misc)
    iw_ref[...] = h[:, 3200:3328]


def _cd_proj(x, g, w, fb, tm=512):
    n = x.shape[0]
    tm = min(tm, n)
    row = lambda i: (i, 0)
    widths = [(512, BF16), (512, F32), (512, F32), (1024, BF16), (128, F32), (128, F32), (256, BF16),
              (128, F32), (128, F32)]
    return pl.pallas_call(
        _cd_proj_kernel,
        grid=(n // tm,),
        in_specs=[pl.BlockSpec((tm, D_MODEL), row), _const_spec((1, D_MODEL)), _const_spec((D_MODEL, CD_COLS)),
                  _const_spec((1, LANES))],
        out_specs=[pl.BlockSpec((tm, w_), row) for w_, _ in widths],
        out_shape=[jax.ShapeDtypeStruct((n, w_), dt) for w_, dt in widths],
        compiler_params=_params(("parallel",)),
        name="cd_proj",
    )(x, g, w, fb)


def _aug_consts(is_query):
    p = np.zeros((3, LANES, 8 * LANES), np.float32)
    ones = np.zeros((1, 8 * LANES), np.float32)
    for h in range(8):
        off = h * LANES + (64 if h % 2 == 0 else 0)
        for c in range(3):
            if is_query:
                p[c, h, off + c] = 1.0
                ones[0, off + 3 + c] = 1.0
            else:
                p[c, h, off + 3 + c] = -1.0
                ones[0, off + c] = 1.0
    return jnp.asarray(p, BF16), jnp.asarray(ones, F32)


def _aug_lanes(cum, p_ref, ones_ref):
    hi, mid, lo = _split3(cum * LOG2E)
    return _dot(hi, p_ref[0]) + _dot(mid, p_ref[1]) + _dot(lo, p_ref[2]) + ones_ref[...]


def _running_sum(lf, carry):
    tm = lf.shape[0]

    @pl.when(pl.program_id(1) == 0)
    def _():
        carry[...] = jnp.zeros_like(carry)

    tri = jnp.where(_iota((tm, tm), 1) <= _iota((tm, tm), 0), 1.0, 0.0).astype(BF16)
    cum = _tri_dot(tri, lf) + carry[...]
    carry[...] = cum[tm - 1:tm, :]
    return cum


def _fox_rows(kk, vv, aug, ka_ref, vt_ref):
    lane = _iota((1, LANES), 1)
    for h in range(8):
        pr = slice((h // 2) * LANES, (h // 2 + 1) * LANES)
        own = (lane < 64) if h % 2 == 0 else (lane >= 64)
        ka_ref[0, h] = jnp.where(own, kk[:, pr], aug[:, h * LANES:(h + 1) * LANES]).astype(BF16)
    for pr in range(4):
        vt_ref[0, pr, 0] = vv[:, pr * LANES:(pr + 1) * LANES].T.astype(BF16)


def _fox_pack_kernel(k_ref, v_ref, lf_ref, p_ref, ones_ref, ka_ref, vt_ref, cum_ref, carry):
    cum = _running_sum(lf_ref[0], carry)
    cum_ref[0] = cum
    _fox_rows(k_ref[0], v_ref[0], _aug_lanes(cum, p_ref, ones_ref), ka_ref, vt_ref)


def _fox_pack(k, v, lf, tm):
    bsz, s, _ = k.shape
    p, ones = _aug_consts(False)
    return pl.pallas_call(
        _fox_pack_kernel,
        grid=(bsz, s // tm),
        in_specs=[pl.BlockSpec((1, tm, 512), lambda b, t: (b, t, 0)),
                  pl.BlockSpec((1, tm, 512), lambda b, t: (b, t, 0)),
                  pl.BlockSpec((1, tm, LANES), lambda b, t: (b, t, 0)),
                  _const_spec((3, LANES, 8 * LANES)), _const_spec((1, 8 * LANES))],
        out_specs=[pl.BlockSpec((1, 8, tm, LANES), lambda b, t: (b, 0, t, 0)),
                   pl.BlockSpec((1, 4, 1, LANES, tm), lambda b, t: (b, 0, t, 0, 0)),
                   pl.BlockSpec((1, tm, LANES), lambda b, t: (b, t, 0))],
        out_shape=[jax.ShapeDtypeStruct((bsz, 8, s, LANES), BF16),
                   jax.ShapeDtypeStruct((bsz, 4, s // tm, LANES, tm), BF16),
                   jax.ShapeDtypeStruct((bsz, s, LANES), F32)],
        scratch_shapes=[pltpu.VMEM((1, LANES), F32)],
        compiler_params=_params(("parallel", "arbitrary")),
        name="fox_pack",
    )(k, v, lf, p, ones)


def _q_rows(qq, aug, qa_ref):
    lane = _iota((1, LANES), 1)
    for h in range(8):
        pr = slice((h // 2) * LANES, (h // 2 + 1) * LANES)
        own = (lane < 64) if h % 2 == 0 else (lane >= 64)
        qa_ref[0, h] = jnp.where(own, qq[:, pr], aug[:, h * LANES:(h + 1) * LANES]).astype(BF16)


def _q_pack_kernel(q_ref, cum_ref, p_ref, ones_ref, qa_ref):
    _q_rows(q_ref[0].astype(F32), _aug_lanes(cum_ref[0], p_ref, ones_ref), qa_ref)


def _q_pack(q, cum_q, tm):
    bsz, tq, _ = q.shape
    p, ones = _aug_consts(True)
    return pl.pallas_call(
        _q_pack_kernel,
        grid=(bsz, tq // tm),
        in_specs=[pl.BlockSpec((1, tm, 512), lambda b, t: (b, t, 0)),
                  pl.BlockSpec((1, tm, LANES), lambda b, t: (b, t, 0)),
                  _const_spec((3, LANES, 8 * LANES)), _const_spec((1, 8 * LANES))],
        out_specs=pl.BlockSpec((1, 8, tm, LANES), lambda b, t: (b, 0, t, 0)),
        out_shape=jax.ShapeDtypeStruct((bsz, 8, tq, LANES), BF16),
        compiler_params=_params(("parallel", "parallel")),
        name="q_pack",
    )(q, cum_q, p, ones)


def _fox_attn_kernel(q_ref, k_ref, vt_ref, o_ref, s_sc, mt_sc, *, tq, tk, q_start):
    i = pl.program_id(2)
    q_lo = q_start + i * tq
    n_tiles = (q_lo + tq + tk - 1) // tk
    q_pos = q_lo + _iota((1, tq), 1)
    qs = (q_ref[0, 0], q_ref[0, 1])

    def logits(j):
        off = pl.multiple_of(j * tk, tk)
        ok = (off + _iota((tk, 1), 0)) <= q_pos
        ss = [jnp.where(ok, _dot_nt(k_ref[0, hh, pl.ds(off, tk), :], qs[hh]), MASKED) for hh in range(2)]
        return ss, [jnp.max(s, axis=0, keepdims=True) for s in ss]

    def stage(ss, mts):
        for hh in range(2):
            s_sc[hh] = ss[hh]
            mt_sc[hh] = mts[hh]

    stage(*logits(0))

    def body(j, carry):
        nxt = logits(jnp.minimum(j + 1, n_tiles - 1))
        out = []
        for hh in range(2):
            m, acc = carry[hh]
            m_new = jnp.maximum(m, mt_sc[hh])
            p = jnp.exp2(s_sc[hh] - m_new).astype(BF16)
            vt = jnp.concatenate([vt_ref[0, 0, j, hh * 64:(hh + 1) * 64, :], ones], axis=0)
            acc = jnp.exp2(m - m_new) * acc + _dot(vt, p)
            out.append((m_new, acc))
        stage(*nxt)
        return tuple(out)

    ones = jnp.ones((ONES_ROWS, tk), BF16)
    init = (jnp.full((1, tq), NEG, F32), jnp.zeros((64 + ONES_ROWS, tq), F32))
    carry = lax.fori_loop(0, n_tiles, body, (init, init))
    o_t = jnp.concatenate([acc[0:64] / acc[64:65] for _, acc in carry], axis=0)
    o_ref[0] = o_t.T.astype(BF16)


def _fox_attn(q_aug, k_aug, vt, tq, tk, q_start):
    bsz, _, t_q, _ = q_aug.shape
    s = k_aug.shape[2]
    kern = functools.partial(_fox_attn_kernel, tq=tq, tk=tk, q_start=q_start)
    return pl.pallas_call(
        kern,
        grid=(bsz, 4, t_q // tq),
        in_specs=[pl.BlockSpec((1, 2, tq, LANES), lambda b, p, i: (b, p, i, 0)),
                  pl.BlockSpec((1, 2, s, LANES), lambda b, p, i: (b, p, 0, 0)),
                  pl.BlockSpec((1, 1, s // tk, LANES, tk), lambda b, p, i: (b, p, 0, 0, 0))],
        out_specs=pl.BlockSpec((1, tq, LANES), lambda b, p, i: (b, i, p)),
        out_shape=jax.ShapeDtypeStruct((bsz, t_q, 512), BF16),
        scratch_shapes=[pltpu.VMEM((2, tk, tq), F32), pltpu.VMEM((2, 1, tq), F32)],
        compiler_params=_params(("parallel", "parallel", "arbitrary")),
        name="fox_attn",
    )(q_aug, k_aug, vt)


def _dsa_rows(dk, dv, ik, k2_ref, vt_ref, ik2_ref):
    lo = _iota((1, LANES), 1) < 64
    k2_ref[0] = dk.astype(BF16)
    vt_ref[0, 0] = dv.T.astype(BF16)
    ik = jnp.where(lo, ik, 0.0)
    ik2_ref[0, 0] = ik.astype(BF16)
    ik2_ref[0, 1] = pltpu.roll(ik, 64, 1).astype(BF16)


def _dsa_pack_kernel(dk_ref, dv_ref, ik_ref, k2_ref, vt_ref, ik2_ref):
    _dsa_rows(dk_ref[0], dv_ref[0], ik_ref[0], k2_ref, vt_ref, ik2_ref)


def _dsa_pack(dk, dv, ik, tm):
    bsz, s, _ = dk.shape
    tok = pl.BlockSpec((1, tm, LANES), lambda b, t: (b, t, 0))
    return pl.pallas_call(
        _dsa_pack_kernel,
        grid=(bsz, s // tm),
        in_specs=[tok, tok, tok],
        out_specs=[tok,
                   pl.BlockSpec((1, 1, LANES, tm), lambda b, t: (b, t, 0, 0)),
                   pl.BlockSpec((1, 2, tm, LANES), lambda b, t: (b, 0, t, 0))],
        out_shape=[jax.ShapeDtypeStruct((bsz, s, LANES), BF16),
                   jax.ShapeDtypeStruct((bsz, s // tm, LANES, tm), BF16),
                   jax.ShapeDtypeStruct((bsz, 2, s, LANES), BF16)],
        compiler_params=_params(("parallel", "parallel")),
        name="dsa_pack",
    )(dk, dv, ik)


def _cd_fused_kernel(x_ref, g_ref, w_ref, fb_ref, pk_ref, onesk_ref, pq_ref, onesq_ref,
                     ck_ref, cv_ref, dk_ref, dv_ref, misc_ref, iw_ref, iq_ref, dq_ref,
                     qa_ref, ka_ref, vt_ref, k2_ref, dvt_ref, ik2_ref, carry):
    xn = _rms(x_ref[0], g_ref[...]).astype(BF16)
    h = _dot(xn, w_ref[...])
    ck, cv = h[:, 512:1024], h[:, 1024:1536]
    dk, dv = h[:, 2560:2688], h[:, 2688:2816]
    misc = h[:, 3072:3200]
    lane = _iota((1, LANES), 1)
    is_f = (lane >= IDX_DIM) & (lane < IDX_DIM + C_HEADS)
    logf = _log_sigmoid(misc + fb_ref[...])
    misc = jnp.where(is_f, logf, misc)
    ck_ref[0], cv_ref[0], dk_ref[0], dv_ref[0], misc_ref[0] = ck, cv, dk, dv, misc
    iw_ref[0] = h[:, 3200:3328]
    iq_ref[0] = h[:, 2816:3072].astype(BF16)
    dq_ref[0] = (h[:, 1536:2560] * (D_HD ** -0.5 * LOG2E)).astype(BF16)
    cum = _running_sum(pltpu.roll(jnp.where(is_f, logf, 0.0), LANES - IDX_DIM, 1), carry)
    _fox_rows(ck, cv, _aug_lanes(cum, pk_ref, onesk_ref), ka_ref, vt_ref)
    _q_rows(h[:, 0:512] * (C_HD ** -0.5 * LOG2E), _aug_lanes(cum, pq_ref, onesq_ref), qa_ref)
    _dsa_rows(dk, dv, misc, k2_ref, dvt_ref, ik2_ref)


def _cd_fused(x, g, w, fb, tm):
    bsz, t_len, _ = x.shape
    pk, onesk = _aug_consts(False)
    pq, onesq = _aug_consts(True)
    tok = lambda w_: pl.BlockSpec((1, tm, w_), lambda b, t: (b, t, 0))
    hm = pl.BlockSpec((1, 8, tm, LANES), lambda b, t: (b, 0, t, 0))
    sds = jax.ShapeDtypeStruct
    n_t = t_len // tm
    return pl.pallas_call(
        _cd_fused_kernel,
        grid=(bsz, n_t),
        in_specs=[tok(D_MODEL), _const_spec((1, D_MODEL)), _const_spec((D_MODEL, CD_COLS)), _const_spec((1, LANES)),
                  _const_spec((3, LANES, 8 * LANES)), _const_spec((1, 8 * LANES)),
                  _const_spec((3, LANES, 8 * LANES)), _const_spec((1, 8 * LANES))],
        out_specs=[tok(512), tok(512), tok(LANES), tok(LANES), tok(LANES), tok(LANES), tok(256), tok(8 * LANES),
                   hm, hm, pl.BlockSpec((1, 4, 1, LANES, tm), lambda b, t: (b, 0, t, 0, 0)),
                   tok(LANES), pl.BlockSpec((1, 1, LANES, tm), lambda b, t: (b, t, 0, 0)),
                   pl.BlockSpec((1, 2, tm, LANES), lambda b, t: (b, 0, t, 0))],
        out_shape=[sds((bsz, t_len, 512), F32), sds((bsz, t_len, 512), F32), sds((bsz, t_len, LANES), F32),
                   sds((bsz, t_len, LANES), F32), sds((bsz, t_len, LANES), F32), sds((bsz, t_len, LANES), F32),
                   sds((bsz, t_len, 256), BF16), sds((bsz, t_len, 8 * LANES), BF16),
                   sds((bsz, 8, t_len, LANES), BF16), sds((bsz, 8, t_len, LANES), BF16),
                   sds((bsz, 4, n_t, LANES, tm), BF16), sds((bsz, t_len, LANES), BF16),
                   sds((bsz, n_t, LANES, tm), BF16), sds((bsz, 2, t_len, LANES), BF16)],
        scratch_shapes=[pltpu.VMEM((1, LANES), F32)],
        compiler_params=_params(("parallel", "arbitrary")),
        name="cd_fused",
    )(x, g, w, fb, pk, onesk, pq, onesq)


def _dsa_attn_kernel(dq_ref, iq_ref, iw_ref, ik2_ref, k2_ref, vt_ref, o_ref, keys_sc, hi_sc, lo_sc, eq_sc, m_sc, mt_sc, acc_sc,
                     s_sc,
                     *, tq, tk, q_start, s_valid, topk):
    i = pl.program_id(1)
    q_lo = q_start + i * tq
    adm_row = jnp.minimum(((q_lo + _iota((1, tq), 1)) // CHUNK + 1) * CHUNK, s_valid)
    adm_end = jnp.minimum(((q_lo + tq - 1) // CHUNK + 1) * CHUNK, s_valid)
    n_tiles = (adm_end + tk - 1) // tk
    iw_t = iw_ref[0].T
    iq = iq_ref[0]

    def score_body(j, c):
        off = pl.multiple_of(j * tk, tk)
        sc = jnp.zeros((tk, tq), F32)
        for hd in range(IDX_HEADS):
            qp = iq[:, (hd // 2) * LANES:(hd // 2 + 1) * LANES]
            r = jnp.maximum(_dot_nt(ik2_ref[0, hd % 2, pl.ds(off, tk), :], qp), 0.0)
            sc = sc + r * iw_t[hd:hd + 1, :]
        sc = sc * IDX_SCALE
        adm = (off + _iota((tk, tq), 0)) < adm_row
        bits = lax.bitcast_convert_type(sc, I32)
        bits = jnp.where(bits == INT_MIN, 0, bits)
        key = bits ^ ((bits >> 31) & 0x7FFFFFFF)
        keys_sc[j] = jnp.where(adm, key, INT_MIN)
        hi = lax.bitcast_convert_type(bits & -65536, F32)
        hi_sc[j] = jnp.where(adm, hi, -jnp.inf).astype(BF16)
        return c

    lax.fori_loop(0, n_tiles, score_body, 0)

    def pairs(one, init):
        part = lax.fori_loop(0, n_tiles // 2, lambda jj, c: one(2 * jj + 1, one(2 * jj, c)), init)
        return lax.fori_loop(2 * (n_tiles // 2), n_tiles, one, part)

    one16 = jnp.ones((tk, tq), BF16)
    zero16 = jnp.zeros((tk, tq), BF16)

    def hi_body(b, pre):
        cand = pre + lax.shift_left(jnp.int32(1), 15 - b)
        cbits = (cand ^ ((cand >> 15) & 0x7FFF)) & 0xFFFF
        subnormal = ((cbits & 0x7F80) == 0) & ((cbits & 0x007F) != 0)
        cbits = jnp.where(subnormal, jnp.where((cbits & 0x8000) != 0, 0x0000, 0x0080), cbits)
        cval = jnp.broadcast_to(lax.bitcast_convert_type(cbits << 16, F32).astype(BF16), (tk, tq))

        def one(j, c):
            hit = jnp.where(hi_sc[j] >= cval, one16, zero16).reshape(tk // 64, 4, 16, tq)
            for r in range(tk // 64):
                c = c + hit[r]
            return c
        part = pairs(one, jnp.zeros((4, 16, tq), BF16)).astype(F32)
        cnt = jnp.sum(jnp.sum(part, axis=0), axis=0, keepdims=True)
        return jnp.where(cnt >= topk, cand, pre)

    pre = lax.fori_loop(0, 16, hi_body, jnp.full((1, tq), -2 ** 15, I32))

    def count(pred):
        def one(j, c):
            hit = jnp.where(pred(j, keys_sc[j]), 1.0, 0.0)
            return c + jnp.sum(hit.reshape(tk // 32, 4, 8, tq), axis=0)

        part = pairs(one, jnp.zeros((4, 8, tq), F32))
        return jnp.sum(jnp.sum(part, axis=0), axis=0, keepdims=True)

    def lo_prep(j, c):
        kk = keys_sc[j]
        lo_sc[j] = jnp.where((kk >> 16) == pre, (kk & 0xFFFF) - 2 ** 15, -2 ** 15).astype(jnp.int16)
        hit = jnp.where((kk >> 16) > pre, 1.0, 0.0)
        return c + jnp.sum(hit.reshape(tk // 32, 4, 8, tq), axis=0)

    above = lax.fori_loop(0, n_tiles, lo_prep, jnp.zeros((4, 8, tq), F32))
    above = jnp.sum(jnp.sum(above, axis=0), axis=0, keepdims=True)
    one_i16 = jnp.ones((tk, tq), jnp.int16)
    zero_i16 = jnp.zeros((tk, tq), jnp.int16)

    def lo_body(b, low):
        cand = low + lax.shift_left(jnp.int32(1), 15 - b)
        cval = jnp.broadcast_to(cand.astype(jnp.int16), (tk, tq))

        def one(j, c):
            hit = jnp.where(lo_sc[j] >= cval, one_i16, zero_i16).reshape(tk // 64, 4, 16, tq)
            for r in range(tk // 64):
                c = c + hit[r]
            return c
        part = pairs(one, jnp.zeros((4, 16, tq), jnp.int16)).astype(I32).astype(F32)
        cnt = above + jnp.sum(jnp.sum(part, axis=0), axis=0, keepdims=True)
        return jnp.where(cnt >= topk, cand, low)

    low = lax.fori_loop(0, 16, lo_body, jnp.full((1, tq), -2 ** 15, I32))
    tau = (pre << 16) + (low + 2 ** 15)
    tau = jnp.where(adm_row < int(topk), INT_MIN, tau)
    need = jnp.where(tau == INT_MIN, -1.0, topk - count(lambda j, kk: kk > tau))
    eq_sc[...] = jnp.zeros(eq_sc.shape, F32)
    tri = jnp.where(_iota((tk, tk), 1) <= _iota((tk, tk), 0), 1.0, 0.0).astype(BF16)

    m_sc[...] = jnp.full(m_sc.shape, NEG, F32)
    acc_sc[...] = jnp.zeros(acc_sc.shape, F32)
    ones = jnp.ones((ONES_ROWS, tk), BF16)
    n_rep = D_HEADS // D_KV_HEADS
    q_stack = [jnp.concatenate([dq_ref[0, :, hd * LANES:(hd + 1) * LANES]
                                for hd in range(g * n_rep, (g + 1) * n_rep)], axis=0) for g in range(D_KV_HEADS)]

    def logits(j):
        kk = keys_sc[j]
        eq = kk == tau
        run = _dot(tri, jnp.where(eq, 1.0, 0.0).astype(BF16)) + eq_sc[...]
        eq_sc[...] = run[tk - 1:tk, :]
        sel = (kk > tau) | (eq & (run <= need))
        bias = jnp.where(sel, 0.0, MASKED)
        bias = jnp.concatenate([bias] * n_rep, axis=1)
        kt = k2_ref[0, pl.ds(pl.multiple_of(j * tk, tk), tk), :]
        ss = [_dot_nt(kt, q_stack[g]) + bias for g in range(D_KV_HEADS)]
        return ss, [jnp.max(s, axis=0, keepdims=True) for s in ss]

    def stage(slot, ss, mts):
        for g in range(D_KV_HEADS):
            s_sc[slot, g] = ss[g]
            mt_sc[slot, g] = mts[g]

    stage(0, *logits(0))

    def step(j, cur):
        stage(1 - cur, *logits(jnp.minimum(j + 1, n_tiles - 1)))
        for g in range(D_KV_HEADS):
            m_old = m_sc[g]
            m_new = jnp.maximum(m_old, mt_sc[cur, g])
            p = jnp.exp2(s_sc[cur, g] - m_new).astype(BF16)
            vt = jnp.concatenate([vt_ref[0, j, g * 64:(g + 1) * 64, :], ones], axis=0)
            acc_sc[g] = jnp.exp2(m_old - m_new) * acc_sc[g] + _dot(vt, p)
            m_sc[g] = m_new

    def attn_body(jj, c):
        step(2 * jj, 0)

        @pl.when(2 * jj + 1 < n_tiles)
        def _():
            step(2 * jj + 1, 1)
        return c

    lax.fori_loop(0, (n_tiles + 1) // 2, attn_body, 0)
    for g in range(D_KV_HEADS):
        o_g = acc_sc[g, 0:64] / acc_sc[g, 64:65]
        for pr in range(n_rep // 2):
            o_t = jnp.concatenate([o_g[:, (2 * pr) * tq:(2 * pr + 1) * tq],
                                   o_g[:, (2 * pr + 1) * tq:(2 * pr + 2) * tq]], axis=0)
            col = (g * n_rep // 2 + pr) * LANES
            o_ref[0, :, col:col + LANES] = o_t.T.astype(BF16)


def _dsa_attn(dq, iq, iw, ik2, k2, vt, tq, tk, q_start, s_valid):
    bsz, t_q, _ = dq.shape
    s = k2.shape[1]
    topk = min(IDX_TOPK_MAX, s_valid // 4)
    assert (s // tk) * (tk // 64) <= 256, "packed bf16 partial counts must stay exactly representable"
    kern = functools.partial(_dsa_attn_kernel, tq=tq, tk=tk, q_start=q_start, s_valid=s_valid, topk=float(topk))
    return pl.pallas_call(
        kern,
        grid=(bsz, t_q // tq),
        in_specs=[pl.BlockSpec((1, tq, 8 * LANES), lambda b, i: (b, i, 0)),
                  pl.BlockSpec((1, tq, 256), lambda b, i: (b, i, 0)),
                  pl.BlockSpec((1, tq, LANES), lambda b, i: (b, i, 0)),
                  pl.BlockSpec((1, 2, s, LANES), lambda b, i: (b, 0, 0, 0)),
                  pl.BlockSpec((1, s, LANES), lambda b, i: (b, 0, 0)),
                  pl.BlockSpec((1, s // tk, LANES, tk), lambda b, i: (b, 0, 0, 0))],
        out_specs=pl.BlockSpec((1, tq, 512), lambda b, i: (b, i, 0)),
        out_shape=jax.ShapeDtypeStruct((bsz, t_q, 512), BF16),
        scratch_shapes=[pltpu.VMEM((s // tk, tk, tq), I32),
                        pltpu.VMEM((s // tk, tk, tq), BF16),
                        pltpu.VMEM((s // tk, tk, tq), jnp.int16),
                        pltpu.VMEM((1, tq), F32),
                        pltpu.VMEM((D_KV_HEADS, 1, D_HEADS // D_KV_HEADS * tq), F32),
                        pltpu.VMEM((2, D_KV_HEADS, 1, D_HEADS // D_KV_HEADS * tq), F32),
                        pltpu.VMEM((D_KV_HEADS, 64 + ONES_ROWS, D_HEADS // D_KV_HEADS * tq), F32),
                        pltpu.VMEM((2, D_KV_HEADS, tk, D_HEADS // D_KV_HEADS * tq), F32)],
        compiler_params=_params(("parallel", "arbitrary")),
        name="dsa_attn",
    )(dq, iq, iw, ik2, k2, vt)


def _prep_ab(w_in, w_gk):
    w = jnp.pad(w_in, ((0, 0), (0, AB_COLS - w_in.shape[1]))).astype(BF16)
    wgk = jnp.pad(w_gk, ((0, LANES - B_GATE_RANK), (0, 0))).astype(BF16)
    return w, wgk


def _prep_cd(w_in, f_bias):
    o = np.cumsum([0, 512, 512, 512, C_HEADS, 512, 128, 128, 256, IDX_DIM, IDX_HEADS])
    c_q, c_k, c_v, c_f, d_q, d_k, d_v, d_iq, d_ik, d_iw = (w_in[:, o[i]:o[i + 1]] for i in range(10))
    zeros = lambda n: jnp.zeros((D_MODEL, n), w_in.dtype)
    dq_cols = []
    for h in range(D_HEADS):
        g = h // (D_HEADS // D_KV_HEADS)
        wh = d_q[:, h * 64:(h + 1) * 64]
        dq_cols += [wh, zeros(64)] if g == 0 else [zeros(64), wh]
    w = jnp.concatenate([c_q, c_k, c_v] + dq_cols + [d_k, d_v, d_iq, d_ik, c_f, zeros(LANES - IDX_DIM - C_HEADS),
                                                     d_iw, zeros(LANES - IDX_HEADS)], axis=1).astype(BF16)
    fb = jnp.pad(f_bias.astype(F32), (IDX_DIM, LANES - IDX_DIM - C_HEADS)).reshape(1, LANES)
    return w, fb


def _pad_rows(z, s):
    return jnp.pad(z, ((0, 0), (0, s - z.shape[1]), (0, 0)))


def _pad_lanes(z, n=LANES):
    return jnp.pad(z, ((0, 0), (0, 0), (0, n - z.shape[2])))


def _trunk(x, s_a, s_b, cache, wts):
    bsz, t_len, _ = x.shape
    row = lambda z: z.reshape(1, -1).astype(F32)

    t_pad = -(-t_len // REC_CHUNK) * REC_CHUNK
    x1, sa_new, sb_new = _ab_layer(
        _pad_rows(x, t_pad), s_a, s_b.reshape(bsz, 2, 128, 128), row(wts['norm_mix'][0]), wts['ab_w_in'],
        wts['gla_w_gk'], row(wts['gla_b_gk']), row(wts['lb']), row(wts['hgrn_gnorm']), row(wts['gla_gnorm']),
        wts['ab_w_out'], t_len)
    n = bsz * t_len
    x1 = x1[:, :t_len].reshape(n, D_MODEL)
    x2 = _ffn(x1, row(wts['norm_ffn'][0]), wts['ffn_w_in'][0], wts['ffn_w_out'][0])

    tk = 512
    q_start = 0 if cache is None else cache[0].shape[1]
    s_valid = q_start + t_len
    tq_pad = -(-t_len // LANES) * LANES
    if cache is None and t_len % tk == 0:
        (ck, cv, dk, dv, misc, iw, iq, dq, q_aug, k_aug, v_t, k2, dv_t, ik2) = _cd_fused(
            x2.reshape(bsz, t_len, D_MODEL), row(wts['norm_mix'][1]), wts['cd_w_in'], wts['fox_f_bias'], tm=tk)
        logf = misc[:, :, IDX_DIM:IDX_DIM + C_HEADS]
    else:
        fq, ck, cv, dq, dk, dv, iq, misc, iw = _cd_proj(x2, row(wts['norm_mix'][1]), wts['cd_w_in'],
                                                        wts['fox_f_bias'])
        per_b = lambda z: z.reshape(bsz, t_len, z.shape[-1])
        fq, ck, cv, dq, dk, dv, iq, misc, iw = map(per_b, (fq, ck, cv, dq, dk, dv, iq, misc, iw))
        logf = misc[:, :, IDX_DIM:IDX_DIM + C_HEADS]
        if cache is None:
            k_all, v_all, lf_all, dk_all, dv_all, ik_all = ck, cv, _pad_lanes(logf), dk, dv, misc
        else:
            c_k, c_v, c_lf, c_dk, c_dv, c_ik = cache
            cat = lambda c, r: jnp.concatenate([c.reshape(bsz, q_start, -1).astype(F32), r], axis=1)
            k_all, v_all, dk_all, dv_all = cat(c_k, ck), cat(c_v, cv), cat(c_dk, dk), cat(c_dv, dv)
            lf_all = _pad_lanes(cat(c_lf, logf))
            ik_all = cat(_pad_lanes(c_ik), misc)
        s_pad = -(-s_valid // tk) * tk
        k_all, v_all, lf_all, dk_all, dv_all, ik_all = (_pad_rows(z, s_pad) for z in
                                                        (k_all, v_all, lf_all, dk_all, dv_all, ik_all))
        k_aug, v_t, cum = _fox_pack(k_all, v_all, lf_all, tm=tk)
        q_aug = _q_pack(_pad_rows(fq, tq_pad), _pad_rows(cum[:, q_start:q_start + t_len], tq_pad),
                        tm=512 if tq_pad % 512 == 0 else LANES)
        k2, dv_t, ik2 = _dsa_pack(dk_all, dv_all, ik_all, tm=tk)
    o_c = _fox_attn(q_aug, k_aug, v_t, tq=512 if tq_pad % 512 == 0 else LANES, tk=tk, q_start=q_start)[:, :t_len]
    o_d = _dsa_attn(_pad_rows(dq, tq_pad), _pad_rows(iq, tq_pad), _pad_rows(iw, tq_pad), ik2, k2, dv_t,
                    tq=256 if tq_pad % 256 == 0 else LANES, tk=tk, q_start=q_start, s_valid=s_valid)[:, :t_len]

    y = _ffn(x2, row(wts['norm_ffn'][1]), wts['ffn_w_in'][1], wts['ffn_w_out'][1],
             attn=(o_c.reshape(n, 512), o_d.reshape(n, 512), wts['cd_w_out']), g_final=row(wts['norm_final']))
    rows = (ck.reshape(1, bsz, t_len, C_HEADS, C_HD), cv.reshape(1, bsz, t_len, C_HEADS, C_HD),
            logf[None], dk.reshape(1, bsz, t_len, D_KV_HEADS, D_HD), dv.reshape(1, bsz, t_len, D_KV_HEADS, D_HD),
            misc[None, :, :, :IDX_DIM])
    return (y.reshape(bsz, t_len, D_MODEL), sa_new[None], sb_new.reshape(1, bsz, B_HEADS, B_DK, B_DV)) + rows


def kernel(x_prompt, x_sample, state_hgrn, state_gla, cache_fox_k, cache_fox_v, cache_fox_logf, cache_dsa_k, cache_dsa_v, cache_dsa_ik, norm_mix, norm_ffn, norm_final, ab_w_in, ab_w_out, hgrn_lb_logits, hgrn_gnorm, gla_w_gk, gla_b_gk, gla_gnorm, cd_w_in, cd_w_out, fox_f_bias, ffn_w_in, ffn_w_out):
    lbs = jnp.cumsum(jax.nn.softmax(hgrn_lb_logits.astype(F32), axis=0), axis=0)
    w_ab, w_gk = _prep_ab(ab_w_in[0], gla_w_gk[0])
    w_cd, fb = _prep_cd(cd_w_in[0], fox_f_bias[0])
    wts = dict(norm_mix=norm_mix, norm_ffn=norm_ffn, norm_final=norm_final, ab_w_in=w_ab,
               ab_w_out=ab_w_out[0].astype(BF16), lb=lbs[0], hgrn_gnorm=hgrn_gnorm[0], gla_w_gk=w_gk,
               gla_b_gk=gla_b_gk[0], gla_gnorm=gla_gnorm[0], cd_w_in=w_cd, cd_w_out=cd_w_out[0].astype(BF16),
               fox_f_bias=fb, ffn_w_in=ffn_w_in.astype(BF16), ffn_w_out=ffn_w_out.astype(BF16))
    bp = x_prompt.shape[0]
    p_out = _trunk(x_prompt, jnp.zeros((bp, A_HEADS, A_DK, A_DV), F32), jnp.zeros((bp, B_HEADS, B_DK, B_DV), F32),
                   None, wts)
    cache = (cache_fox_k[0], cache_fox_v[0], cache_fox_logf[0], cache_dsa_k[0], cache_dsa_v[0], cache_dsa_ik[0])
    s_out = _trunk(x_sample, state_hgrn[0], state_gla[0], cache, wts)
    return (p_out[0], s_out[0]) + tuple(p_out[1:]) + tuple(s_out[1:])
```

```python
import functools

import numpy as np
import jax
import jax.numpy as jnp
from jax import lax
from jax.experimental import pallas as pl
from jax.experimental.pallas import tpu as pltpu

F32 = jnp.float32
BF16 = jnp.bfloat16
I32 = jnp.int32

D_MODEL = 1024
CHUNK = 64
A_HEADS, A_DK, A_DV = 4, 128, 128
B_HEADS, B_DK, B_DV = 4, 64, 128
B_GATE_RANK = 16
B_GATE_NORM = 16.0
C_HEADS, C_HD = 8, 64
D_HEADS, D_KV_HEADS, D_HD = 8, 2, 64
IDX_HEADS, IDX_DIM = 4, 64
IDX_TOPK_MAX = 256
IDX_SCALE = (IDX_DIM ** -0.5) * (IDX_HEADS ** -0.5)
FFN_HIDDEN = ((8 * D_MODEL // 3 + 255) // 256) * 256

LANES = 128
RSUB = 16
REC_CHUNK = 128
VMEM_LIMIT = 56 * 1024 * 1024
NEG = -1e30
MASKED = -2e30
LOG2E = 1.4426950408889634
ONES_ROWS = 16
INT_MIN = -2 ** 31

AB_COLS = 4 * 512 + 256 + 256 + 512 + 512 + LANES
CD_COLS = 3 * 512 + 8 * LANES + 2 * LANES + 256 + LANES + LANES


def _dot(a, b):
    return jnp.dot(a, b, preferred_element_type=F32)


def _dot_nt(a, b):
    return lax.dot_general(a, b, (((1,), (1,)), ((), ())), preferred_element_type=F32)


def _rms(x, g, eps=1e-6):
    return x * lax.rsqrt(jnp.mean(x * x, axis=-1, keepdims=True) + eps) * g


def _silu(x):
    return x * jax.nn.sigmoid(x)


def _log_sigmoid(x):
    return jnp.minimum(x, 0.0) - jnp.log1p(jnp.exp(-jnp.abs(x)))


def _split3(x):
    hi = x.astype(BF16)
    r = x - hi.astype(F32)
    mid = r.astype(BF16)
    lo = (r - mid.astype(F32)).astype(BF16)
    return hi, mid, lo


def _tri_dot(tri, x):
    hi, mid, lo = _split3(x)
    return _dot(tri, hi) + _dot(tri, mid) + _dot(tri, lo)


def _iota(shape, dim):
    return lax.broadcasted_iota(I32, shape, dim)


def _const_spec(shape):
    zeros = (0,) * len(shape)
    return pl.BlockSpec(shape, lambda *_: zeros, pipeline_mode=pl.Buffered(1))


def _params(sem):
    return pltpu.CompilerParams(dimension_semantics=sem, vmem_limit_bytes=VMEM_LIMIT)


def _ab_kernel(x_ref, sa_ref, sb_ref, g_ref, win_ref, wgk_ref, bgk_ref, lb_ref, agn_ref, bgn_ref, wout_ref,
               xo_ref, sao_ref, sbo_ref, s_sc, h_sc, o_sc, *, t_valid, t_pad):
    C = REC_CHUNK
    tm = x_ref.shape[1]
    t = pl.program_id(1)

    @pl.when(t == 0)
    def _():
        s_sc[0:4] = sa_ref[0]
        s_sc[4:6] = sb_ref[0]

    h_sc[...] = _dot(_rms(x_ref[0], g_ref[...]).astype(BF16), win_ref[...])

    def chunk_body(c, carry):
        r0 = pl.multiple_of(c * C, C)
        _ab_chunk(r0, t * tm + r0, h_sc, o_sc, s_sc, wgk_ref, bgk_ref, lb_ref, agn_ref, bgn_ref,
                  t_valid=t_valid, t_pad=t_pad)
        return carry

    lax.fori_loop(0, tm // C, chunk_body, 0)
    xo_ref[0] = x_ref[0] + _dot(o_sc[...], wout_ref[...])

    @pl.when(t == pl.num_programs(1) - 1)
    def _():
        sao_ref[0] = s_sc[0:4]
        sbo_ref[0] = s_sc[4:6]


def _ab_chunk(r0, row0, h_sc, o_sc, s_sc, wgk_ref, bgk_ref, lb_ref, agn_ref, bgn_ref, *, t_valid, t_pad):
    C = REC_CHUNK
    cols = lambda a, b: h_sc[pl.ds(r0, C), a:b]
    a_q, a_f, a_i, a_g = cols(0, 512), cols(512, 1024), cols(1024, 1536), cols(1536, 2048)
    b_q, b_k, b_v, b_g = cols(2048, 2304), cols(2304, 2560), cols(2560, 3072), cols(3072, 3584)
    b_lr = cols(3584, 3712)

    lb = lb_ref[...]
    f = lb + (1.0 - lb) * jax.nn.sigmoid(a_f)
    gk = _dot(b_lr.astype(BF16), wgk_ref[...]) + bgk_ref[...]
    la = jnp.concatenate([jnp.log2(f), _log_sigmoid(gk) * (LOG2E / B_GATE_NORM)], axis=1)
    q = jnp.concatenate([_silu(a_q), b_q * (B_DK ** -0.5)], axis=1)
    k = jnp.concatenate([1.0 - f, b_k], axis=1)
    v_a, v_b = a_i, b_v
    if t_valid < t_pad:
        ok = (row0 + _iota((C, 1), 0)) < t_valid
        la = jnp.where(ok, la, 0.0)
        k = jnp.where(ok, k, 0.0)
        v_a = jnp.where(ok, v_a, 0.0)
        v_b = jnp.where(ok, v_b, 0.0)

    row = _iota((C, C), 0)
    col = _iota((C, C), 1)
    causal = col <= row
    tri = jnp.where(causal, 1.0, 0.0).astype(BF16)
    tri_in = jnp.where(causal & ((row >> 4) == (col >> 4)), 1.0, 0.0).astype(BF16)
    bc = _tri_dot(tri, la)
    b_in = _tri_dot(tri_in, la)
    lane = _iota((1, LANES), 1)
    lo_half = lane < 64
    srow = _iota((LANES, 1), 0) < 64

    o_heads = [None] * 8
    for u in range(6):
        sl = slice(u * LANES, (u + 1) * LANES)
        qu, ku, bcu = q[:, sl], k[:, sl], bc[:, sl]
        s_old = s_sc[u]
        bend = bcu[C - 1:C, :]
        qt = qu * jnp.exp2(b_in[:, sl])
        qdec = qu * jnp.exp2(bcu)
        if u < 4:
            heads = [(u, None, v_a[:, sl])]
        else:
            ha = 4 + 2 * (u - 4)
            heads = [(ha, lo_half, v_b[:, (ha - 4) * LANES:(ha - 3) * LANES]),
                     (ha + 1, jnp.logical_not(lo_half), v_b[:, (ha - 3) * LANES:(ha - 2) * LANES])]
        a_rows = [[] for _ in heads]
        for i in range(C // RSUB):
            n = RSUB * (i + 1)
            if i == 0:
                kt = ku[0:n] * jnp.exp2(-bcu[0:n])
            else:
                kt = ku[0:n] * jnp.exp2(bcu[RSUB * i - 1:RSUB * i, :] - bcu[0:n])
            if n < C:
                kt = jnp.concatenate([kt, jnp.zeros((C - n, LANES), F32)], axis=0)
            ktb = kt.astype(BF16)
            qi = qt[RSUB * i:RSUB * (i + 1)]
            lhs = jnp.concatenate([qi if msk is None else jnp.where(msk, qi, 0.0) for _, msk, _ in heads], axis=0)
            res = _dot_nt(lhs.astype(BF16), ktb)
            for hi_ in range(len(heads)):
                a_rows[hi_].append(res[RSUB * hi_:RSUB * (hi_ + 1)])
        qd = jnp.concatenate([qdec if msk is None else jnp.where(msk, qdec, 0.0) for _, msk, _ in heads], axis=0)
        o_inter = _dot(qd.astype(BF16), s_old.astype(BF16))
        for hi_, (hd, msk, vh) in enumerate(heads):
            att = jnp.where(causal, jnp.concatenate(a_rows[hi_], axis=0), 0.0)
            o_heads[hd] = o_inter[C * hi_:C * (hi_ + 1)] + _dot(att.astype(BF16), vh.astype(BF16))
        kht = (ku * jnp.exp2(bend - bcu)).T
        dcol = jnp.broadcast_to(jnp.exp2(bend), (LANES, LANES)).T
        if u < 4:
            upd = _dot(kht.astype(BF16), heads[0][2].astype(BF16))
        else:
            lhs = jnp.concatenate([jnp.where(srow, kht, 0.0), jnp.where(srow, 0.0, kht)], axis=1)
            rhs = jnp.concatenate([heads[0][2], heads[1][2]], axis=0)
            upd = _dot(lhs.astype(BF16), rhs.astype(BF16))
        s_sc[u] = dcol * s_old + upd

    outs = []
    for hd in range(8):
        if hd < 4:
            gn, gate = agn_ref[...], a_g[:, hd * LANES:(hd + 1) * LANES]
        else:
            gn, gate = bgn_ref[...], b_g[:, (hd - 4) * LANES:(hd - 3) * LANES]
        outs.append(_rms(o_heads[hd], gn) * _silu(gate))
    o_sc[pl.ds(r0, C), :] = jnp.concatenate(outs, axis=1).astype(BF16)


def _ab_layer(x, s_a, s_b, g, w_in, w_gk, b_gk, lb, a_gn, b_gn, w_out, t_valid):
    bsz, t_pad, _ = x.shape
    tm = 512 if t_pad % 512 == 0 else REC_CHUNK
    kern = functools.partial(_ab_kernel, t_valid=t_valid, t_pad=t_pad)
    return pl.pallas_call(
        kern,
        grid=(bsz, t_pad // tm),
        in_specs=[
            pl.BlockSpec((1, tm, D_MODEL), lambda b, t: (b, t, 0)),
            pl.BlockSpec((1, 4, 128, 128), lambda b, t: (b, 0, 0, 0)),
            pl.BlockSpec((1, 2, 128, 128), lambda b, t: (b, 0, 0, 0)),
            _const_spec((1, D_MODEL)),
            _const_spec((D_MODEL, AB_COLS)),
            _const_spec((LANES, 256)),
            _const_spec((1, 256)),
            _const_spec((1, 512)),
            _const_spec((1, 128)),
            _const_spec((1, 128)),
            _const_spec((D_MODEL, D_MODEL)),
        ],
        out_specs=[
            pl.BlockSpec((1, tm, D_MODEL), lambda b, t: (b, t, 0)),
            pl.BlockSpec((1, 4, 128, 128), lambda b, t: (b, 0, 0, 0)),
            pl.BlockSpec((1, 2, 128, 128), lambda b, t: (b, 0, 0, 0)),
        ],
        out_shape=[
            jax.ShapeDtypeStruct((bsz, t_pad, D_MODEL), F32),
            jax.ShapeDtypeStruct((bsz, 4, 128, 128), F32),
            jax.ShapeDtypeStruct((bsz, 2, 128, 128), F32),
        ],
        scratch_shapes=[pltpu.VMEM((6, 128, 128), F32), pltpu.VMEM((tm, AB_COLS), F32),
                        pltpu.VMEM((tm, D_MODEL), BF16)],
        compiler_params=_params(("parallel", "arbitrary")),
        name="ab_layer",
    )(x, s_a, s_b, g, w_in, w_gk, b_gk, lb, a_gn, b_gn, w_out)


FFN_TILE = FFN_HIDDEN // 2


def _ffn_kernel(*refs, has_attn, has_final):
    refs = list(refs)
    x_ref = refs.pop(0)
    x = x_ref[...]
    if has_attn:
        oc_ref, od_ref, wo_ref = refs.pop(0), refs.pop(0), refs.pop(0)
        x = x + _dot(jnp.concatenate([oc_ref[...], od_ref[...]], axis=1), wo_ref[...])
    g_ref, win_ref, wout_ref = refs.pop(0), refs.pop(0), refs.pop(0)
    gf_ref = refs.pop(0) if has_final else None
    out_ref = refs.pop(0)
    xn = _rms(x, g_ref[...]).astype(BF16)
    acc = x
    for j in range(FFN_HIDDEN // FFN_TILE):
        gate = _dot(xn, win_ref[:, j * FFN_TILE:(j + 1) * FFN_TILE])
        up = _dot(xn, win_ref[:, FFN_HIDDEN + j * FFN_TILE:FFN_HIDDEN + (j + 1) * FFN_TILE])
        act = (_silu(gate) * up).astype(BF16)
        acc = acc + _dot(act, wout_ref[j * FFN_TILE:(j + 1) * FFN_TILE, :])
    if has_final:
        acc = _rms(acc, gf_ref[...])
    out_ref[...] = acc


def _ffn(x, g, w_in, w_out, attn=None, g_final=None, tm=512):
    n = x.shape[0]
    tm = min(tm, n)
    row = lambda i: (i, 0)
    args = [x]
    specs = [pl.BlockSpec((tm, D_MODEL), row)]
    if attn is not None:
        oc, od, wo = attn
        args += [oc, od, wo]
        specs += [pl.BlockSpec((tm, 512), row), pl.BlockSpec((tm, 512), row), _const_spec((D_MODEL, D_MODEL))]
    args += [g, w_in, w_out]
    specs += [_const_spec((1, D_MODEL)), _const_spec((D_MODEL, 2 * FFN_HIDDEN)), _const_spec((FFN_HIDDEN, D_MODEL))]
    if g_final is not None:
        args.append(g_final)
        specs.append(_const_spec((1, D_MODEL)))
    kern = functools.partial(_ffn_kernel, has_attn=attn is not None, has_final=g_final is not None)
    return pl.pallas_call(
        kern,
        grid=(n // tm,),
        in_specs=specs,
        out_specs=pl.BlockSpec((tm, D_MODEL), row),
        out_shape=jax.ShapeDtypeStruct((n, D_MODEL), F32),
        compiler_params=_params(("parallel",)),
        name="ffn",
    )(*args)


def _cd_proj_kernel(x_ref, g_ref, w_ref, fb_ref, fq_ref, ck_ref, cv_ref, dq_ref, dk_ref, dv_ref, iq_ref,
                    misc_ref, iw_ref):
    xn = _rms(x_ref[...], g_ref[...]).astype(BF16)
    h = _dot(xn, w_ref[...])
    fq_ref[...] = (h[:, 0:512] * (C_HD ** -0.5 * LOG2E)).astype(BF16)
    ck_ref[...] = h[:, 512:1024]
    cv_ref[...] = h[:, 1024:1536]
    dq_ref[...] = (h[:, 1536:2560] * (D_HD ** -0.5 * LOG2E)).astype(BF16)
    dk_ref[...] = h[:, 2560:2688]
    dv_ref[...] = h[:, 2688:2816]
    iq_ref[...] = h[:, 2816:3072].astype(BF16)
    misc = h[:, 3072:3200]
    lane = _iota((1, LANES), 1)
    is_f = (lane >= IDX_DIM) & (lane < IDX_DIM + C_HEADS)
    misc_ref[...] = jnp.where(is_f, _log_sigmoid(misc + fb_ref[...]), misc)
    iw_ref[...] = h[:, 3200:3328]


def _cd_proj(x, g, w, fb, tm=512):
    n = x.shape[0]
    tm = min(tm, n)
    row = lambda i: (i, 0)
    widths = [(512, BF16), (512, F32), (512, F32), (1024, BF16), (128, F32), (128, F32), (256, BF16),
              (128, F32), (128, F32)]
    return pl.pallas_call(
        _cd_proj_kernel,
        grid=(n // tm,),
        in_specs=[pl.BlockSpec((tm, D_MODEL), row), _const_spec((1, D_MODEL)), _const_spec((D_MODEL, CD_COLS)),
                  _const_spec((1, LANES))],
        out_specs=[pl.BlockSpec((tm, w_), row) for w_, _ in widths],
        out_shape=[jax.ShapeDtypeStruct((n, w_), dt) for w_, dt in widths],
        compiler_params=_params(("parallel",)),
        name="cd_proj",
    )(x, g, w, fb)


def _aug_consts(is_query):
    p = np.zeros((3, LANES, 8 * LANES), np.float32)
    ones = np.zeros((1, 8 * LANES), np.float32)
    for h in range(8):
        off = h * LANES + (64 if h % 2 == 0 else 0)
        for c in range(3):
            if is_query:
                p[c, h, off + c] = 1.0
                ones[0, off + 3 + c] = 1.0
            else:
                p[c, h, off + 3 + c] = -1.0
                ones[0, off + c] = 1.0
    return jnp.asarray(p, BF16), jnp.asarray(ones, F32)


def _aug_lanes(cum, p_ref, ones_ref):
    hi, mid, lo = _split3(cum * LOG2E)
    return _dot(hi, p_ref[0]) + _dot(mid, p_ref[1]) + _dot(lo, p_ref[2]) + ones_ref[...]


def _running_sum(lf, carry):
    tm = lf.shape[0]

    @pl.when(pl.program_id(1) == 0)
    def _():
        carry[...] = jnp.zeros_like(carry)

    tri = jnp.where(_iota((tm, tm), 1) <= _iota((tm, tm), 0), 1.0, 0.0).astype(BF16)
    cum = _tri_dot(tri, lf) + carry[...]
    carry[...] = cum[tm - 1:tm, :]
    return cum


def _fox_rows(kk, vv, aug, ka_ref, vt_ref):
    lane = _iota((1, LANES), 1)
    for h in range(8):
        pr = slice((h // 2) * LANES, (h // 2 + 1) * LANES)
        own = (lane < 64) if h % 2 == 0 else (lane >= 64)
        ka_ref[0, h] = jnp.where(own, kk[:, pr], aug[:, h * LANES:(h + 1) * LANES]).astype(BF16)
    for pr in range(4):
        vt_ref[0, pr, 0] = vv[:, pr * LANES:(pr + 1) * LANES].T.astype(BF16)


def _fox_pack_kernel(k_ref, v_ref, lf_ref, p_ref, ones_ref, ka_ref, vt_ref, cum_ref, carry):
    cum = _running_sum(lf_ref[0], carry)
    cum_ref[0] = cum
    _fox_rows(k_ref[0], v_ref[0], _aug_lanes(cum, p_ref, ones_ref), ka_ref, vt_ref)


def _fox_pack(k, v, lf, tm):
    bsz, s, _ = k.shape
    p, ones = _aug_consts(False)
    return pl.pallas_call(
        _fox_pack_kernel,
        grid=(bsz, s // tm),
        in_specs=[pl.BlockSpec((1, tm, 512), lambda b, t: (b, t, 0)),
                  pl.BlockSpec((1, tm, 512), lambda b, t: (b, t, 0)),
                  pl.BlockSpec((1, tm, LANES), lambda b, t: (b, t, 0)),
                  _const_spec((3, LANES, 8 * LANES)), _const_spec((1, 8 * LANES))],
        out_specs=[pl.BlockSpec((1, 8, tm, LANES), lambda b, t: (b, 0, t, 0)),
                   pl.BlockSpec((1, 4, 1, LANES, tm), lambda b, t: (b, 0, t, 0, 0)),
                   pl.BlockSpec((1, tm, LANES), lambda b, t: (b, t, 0))],
        out_shape=[jax.ShapeDtypeStruct((bsz, 8, s, LANES), BF16),
                   jax.ShapeDtypeStruct((bsz, 4, s // tm, LANES, tm), BF16),
                   jax.ShapeDtypeStruct((bsz, s, LANES), F32)],
        scratch_shapes=[pltpu.VMEM((1, LANES), F32)],
        compiler_params=_params(("parallel", "arbitrary")),
        name="fox_pack",
    )(k, v, lf, p, ones)


def _q_rows(qq, aug, qa_ref):
    lane = _iota((1, LANES), 1)
    for h in range(8):
        pr = slice((h // 2) * LANES, (h // 2 + 1) * LANES)
        own = (lane < 64) if h % 2 == 0 else (lane >= 64)
        qa_ref[0, h] = jnp.where(own, qq[:, pr], aug[:, h * LANES:(h + 1) * LANES]).astype(BF16)


def _q_pack_kernel(q_ref, cum_ref, p_ref, ones_ref, qa_ref):
    _q_rows(q_ref[0].astype(F32), _aug_lanes(cum_ref[0], p_ref, ones_ref), qa_ref)


def _q_pack(q, cum_q, tm):
    bsz, tq, _ = q.shape
    p, ones = _aug_consts(True)
    return pl.pallas_call(
        _q_pack_kernel,
        grid=(bsz, tq // tm),
        in_specs=[pl.BlockSpec((1, tm, 512), lambda b, t: (b, t, 0)),
                  pl.BlockSpec((1, tm, LANES), lambda b, t: (b, t, 0)),
                  _const_spec((3, LANES, 8 * LANES)), _const_spec((1, 8 * LANES))],
        out_specs=pl.BlockSpec((1, 8, tm, LANES), lambda b, t: (b, 0, t, 0)),
        out_shape=jax.ShapeDtypeStruct((bsz, 8, tq, LANES), BF16),
        compiler_params=_params(("parallel", "parallel")),
        name="q_pack",
    )(q, cum_q, p, ones)


def _fox_attn_kernel(q_ref, k_ref, vt_ref, o_ref, s_sc, mt_sc, *, tq, tk, q_start):
    i = pl.program_id(2)
    q_lo = q_start + i * tq
    n_full = (q_lo + 1) // tk
    n_tiles = (q_lo + tq + tk - 1) // tk
    q_pos = q_lo + _iota((1, tq), 1)
    qs = (q_ref[0, 0], q_ref[0, 1])

    def logits(j, masked):
        off = pl.multiple_of(j * tk, tk)
        ss = [_dot_nt(k_ref[0, hh, pl.ds(off, tk), :], qs[hh]) for hh in range(2)]
        if masked:
            ok = (off + _iota((tk, 1), 0)) <= q_pos
            ss = [jnp.where(ok, s, MASKED) for s in ss]
        return ss, [jnp.max(s, axis=0, keepdims=True) for s in ss]

    def stage(ss, mts):
        for hh in range(2):
            s_sc[hh] = ss[hh]
            mt_sc[hh] = mts[hh]

    stage(*logits(0, True))

    def consume(j, carry):
        out = []
        for hh in range(2):
            m, acc = carry[hh]
            m_new = jnp.maximum(m, mt_sc[hh])
            p = jnp.exp2(s_sc[hh] - m_new).astype(BF16)
            vt = jnp.concatenate([vt_ref[0, 0, j, hh * 64:(hh + 1) * 64, :], ones], axis=0)
            acc = jnp.exp2(m - m_new) * acc + _dot(vt, p)
            out.append((m_new, acc))
        return tuple(out)

    def body(j, carry, masked):
        nxt = logits(j + 1, masked)
        out = consume(j, carry)
        stage(*nxt)
        return out

    ones = jnp.ones((ONES_ROWS, tk), BF16)
    init = (jnp.full((1, tq), NEG, F32), jnp.zeros((64 + ONES_ROWS, tq), F32))
    n_plain = jnp.maximum(n_full - 1, 0)
    carry = lax.fori_loop(0, n_plain, functools.partial(body, masked=False), (init, init))
    carry = lax.fori_loop(n_plain, n_tiles - 1, functools.partial(body, masked=True), carry)
    carry = consume(n_tiles - 1, carry)
    o_t = jnp.concatenate([acc[0:64] / acc[64:65] for _, acc in carry], axis=0)
    o_ref[0] = o_t.T.astype(BF16)


def _fox_attn(q_aug, k_aug, vt, tq, tk, q_start):
    bsz, _, t_q, _ = q_aug.shape
    s = k_aug.shape[2]
    kern = functools.partial(_fox_attn_kernel, tq=tq, tk=tk, q_start=q_start)
    return pl.pallas_call(
        kern,
        grid=(bsz, 4, t_q // tq),
        in_specs=[pl.BlockSpec((1, 2, tq, LANES), lambda b, p, i: (b, p, i, 0)),
                  pl.BlockSpec((1, 2, s, LANES), lambda b, p, i: (b, p, 0, 0)),
                  pl.BlockSpec((1, 1, s // tk, LANES, tk), lambda b, p, i: (b, p, 0, 0, 0))],
        out_specs=pl.BlockSpec((1, tq, LANES), lambda b, p, i: (b, i, p)),
        out_shape=jax.ShapeDtypeStruct((bsz, t_q, 512), BF16),
        scratch_shapes=[pltpu.VMEM((2, tk, tq), F32), pltpu.VMEM((2, 1, tq), F32)],
        compiler_params=_params(("parallel", "parallel", "arbitrary")),
        name="fox_attn",
    )(q_aug, k_aug, vt)


def _dsa_rows(dk, dv, ik, k2_ref, vt_ref, ik2_ref):
    lo = _iota((1, LANES), 1) < 64
    k2_ref[0] = dk.astype(BF16)
    vt_ref[0, 0] = dv.T.astype(BF16)
    ik = jnp.where(lo, ik, 0.0)
    ik2_ref[0, 0] = ik.astype(BF16)
    ik2_ref[0, 1] = pltpu.roll(ik, 64, 1).astype(BF16)


def _dsa_pack_kernel(dk_ref, dv_ref, ik_ref, k2_ref, vt_ref, ik2_ref):
    _dsa_rows(dk_ref[0], dv_ref[0], ik_ref[0], k2_ref, vt_ref, ik2_ref)


def _dsa_pack(dk, dv, ik, tm):
    bsz, s, _ = dk.shape
    tok = pl.BlockSpec((1, tm, LANES), lambda b, t: (b, t, 0))
    return pl.pallas_call(
        _dsa_pack_kernel,
        grid=(bsz, s // tm),
        in_specs=[tok, tok, tok],
        out_specs=[tok,
                   pl.BlockSpec((1, 1, LANES, tm), lambda b, t: (b, t, 0, 0)),
                   pl.BlockSpec((1, 2, tm, LANES), lambda b, t: (b, 0, t, 0))],
        out_shape=[jax.ShapeDtypeStruct((bsz, s, LANES), BF16),
                   jax.ShapeDtypeStruct((bsz, s // tm, LANES, tm), BF16),
                   jax.ShapeDtypeStruct((bsz, 2, s, LANES), BF16)],
        compiler_params=_params(("parallel", "parallel")),
        name="dsa_pack",
    )(dk, dv, ik)


def _cd_fused_kernel(x_ref, g_ref, w_ref, fb_ref, pk_ref, onesk_ref, pq_ref, onesq_ref,
                     ck_ref, cv_ref, dk_ref, dv_ref, misc_ref, iw_ref, iq_ref, dq_ref,
                     qa_ref, ka_ref, vt_ref, k2_ref, dvt_ref, ik2_ref, carry):
    xn = _rms(x_ref[0], g_ref[...]).astype(BF16)
    h = _dot(xn, w_ref[...])
    ck, cv = h[:, 512:1024], h[:, 1024:1536]
    dk, dv = h[:, 2560:2688], h[:, 2688:2816]
    misc = h[:, 3072:3200]
    lane = _iota((1, LANES), 1)
    is_f = (lane >= IDX_DIM) & (lane < IDX_DIM + C_HEADS)
    logf = _log_sigmoid(misc + fb_ref[...])
    misc = jnp.where(is_f, logf, misc)
    ck_ref[0], cv_ref[0], dk_ref[0], dv_ref[0], misc_ref[0] = ck, cv, dk, dv, misc
    iw_ref[0] = h[:, 3200:3328]
    iq_ref[0] = h[:, 2816:3072].astype(BF16)
    dq_ref[0] = (h[:, 1536:2560] * (D_HD ** -0.5 * LOG2E)).astype(BF16)
    cum = _running_sum(pltpu.roll(jnp.where(is_f, logf, 0.0), LANES - IDX_DIM, 1), carry)
    _fox_rows(ck, cv, _aug_lanes(cum, pk_ref, onesk_ref), ka_ref, vt_ref)
    _q_rows(h[:, 0:512] * (C_HD ** -0.5 * LOG2E), _aug_lanes(cum, pq_ref, onesq_ref), qa_ref)
    _dsa_rows(dk, dv, misc, k2_ref, dvt_ref, ik2_ref)


def _cd_fused(x, g, w, fb, tm):
    bsz, t_len, _ = x.shape
    pk, onesk = _aug_consts(False)
    pq, onesq = _aug_consts(True)
    tok = lambda w_: pl.BlockSpec((1, tm, w_), lambda b, t: (b, t, 0))
    hm = pl.BlockSpec((1, 8, tm, LANES), lambda b, t: (b, 0, t, 0))
    sds = jax.ShapeDtypeStruct
    n_t = t_len // tm
    return pl.pallas_call(
        _cd_fused_kernel,
        grid=(bsz, n_t),
        in_specs=[tok(D_MODEL), _const_spec((1, D_MODEL)), _const_spec((D_MODEL, CD_COLS)), _const_spec((1, LANES)),
                  _const_spec((3, LANES, 8 * LANES)), _const_spec((1, 8 * LANES)),
                  _const_spec((3, LANES, 8 * LANES)), _const_spec((1, 8 * LANES))],
        out_specs=[tok(512), tok(512), tok(LANES), tok(LANES), tok(LANES), tok(LANES), tok(256), tok(8 * LANES),
                   hm, hm, pl.BlockSpec((1, 4, 1, LANES, tm), lambda b, t: (b, 0, t, 0, 0)),
                   tok(LANES), pl.BlockSpec((1, 1, LANES, tm), lambda b, t: (b, t, 0, 0)),
                   pl.BlockSpec((1, 2, tm, LANES), lambda b, t: (b, 0, t, 0))],
        out_shape=[sds((bsz, t_len, 512), F32), sds((bsz, t_len, 512), F32), sds((bsz, t_len, LANES), F32),
                   sds((bsz, t_len, LANES), F32), sds((bsz, t_len, LANES), F32), sds((bsz, t_len, LANES), F32),
                   sds((bsz, t_len, 256), BF16), sds((bsz, t_len, 8 * LANES), BF16),
                   sds((bsz, 8, t_len, LANES), BF16), sds((bsz, 8, t_len, LANES), BF16),
                   sds((bsz, 4, n_t, LANES, tm), BF16), sds((bsz, t_len, LANES), BF16),
                   sds((bsz, n_t, LANES, tm), BF16), sds((bsz, 2, t_len, LANES), BF16)],
        scratch_shapes=[pltpu.VMEM((1, LANES), F32)],
        compiler_params=_params(("parallel", "arbitrary")),
        name="cd_fused",
    )(x, g, w, fb, pk, onesk, pq, onesq)


def _dsa_attn_kernel(dq_ref, iq_ref, iw_ref, ik2_ref, k2_ref, vt_ref, o_ref, keys_sc, hi_sc, lo_sc, eq_sc, m_sc, mt_sc, acc_sc,
                     s_sc,
                     *, tq, tk, q_start, s_valid, topk):
    i = pl.program_id(1)
    q_lo = q_start + i * tq
    adm_row = jnp.minimum(((q_lo + _iota((1, tq), 1)) // CHUNK + 1) * CHUNK, s_valid)
    adm_end = jnp.minimum(((q_lo + tq - 1) // CHUNK + 1) * CHUNK, s_valid)
    n_tiles = (adm_end + tk - 1) // tk
    iw_t = iw_ref[0].T
    iq = iq_ref[0]

    def score_body(j, c):
        off = pl.multiple_of(j * tk, tk)
        sc = jnp.zeros((tk, tq), F32)
        for hd in range(IDX_HEADS):
            qp = iq[:, (hd // 2) * LANES:(hd // 2 + 1) * LANES]
            r = jnp.maximum(_dot_nt(ik2_ref[0, hd % 2, pl.ds(off, tk), :], qp), 0.0)
            sc = sc + r * iw_t[hd:hd + 1, :]
        sc = sc * IDX_SCALE
        adm = (off + _iota((tk, tq), 0)) < adm_row
        bits = lax.bitcast_convert_type(sc, I32)
        bits = jnp.where(bits == INT_MIN, 0, bits)
        key = bits ^ ((bits >> 31) & 0x7FFFFFFF)
        keys_sc[j] = jnp.where(adm, key, INT_MIN)
        hi = lax.bitcast_convert_type(bits & -65536, F32)
        hi_sc[j] = jnp.where(adm, hi, -jnp.inf).astype(BF16)
        return c

    lax.fori_loop(0, n_tiles, score_body, 0)

    def pairs(one, init):
        part = lax.fori_loop(0, n_tiles // 2, lambda jj, c: one(2 * jj + 1, one(2 * jj, c)), init)
        return lax.fori_loop(2 * (n_tiles // 2), n_tiles, one, part)

    one16 = jnp.ones((tk, tq), BF16)
    zero16 = jnp.zeros((tk, tq), BF16)

    def hi_body(b, pre):
        cand = pre + lax.shift_left(jnp.int32(1), 15 - b)
        cbits = (cand ^ ((cand >> 15) & 0x7FFF)) & 0xFFFF
        subnormal = ((cbits & 0x7F80) == 0) & ((cbits & 0x007F) != 0)
        cbits = jnp.where(subnormal, jnp.where((cbits & 0x8000) != 0, 0x0000, 0x0080), cbits)
        cval = jnp.broadcast_to(lax.bitcast_convert_type(cbits << 16, F32).astype(BF16), (tk, tq))

        def one(j, c):
            hit = jnp.where(hi_sc[j] >= cval, one16, zero16).reshape(tk // 64, 4, 16, tq)
            for r in range(tk // 64):
                c = c + hit[r]
            return c
        part = pairs(one, jnp.zeros((4, 16, tq), BF16)).astype(F32)
        cnt = jnp.sum(jnp.sum(part, axis=0), axis=0, keepdims=True)
        return jnp.where(cnt >= topk, cand, pre)

    pre = lax.fori_loop(0, 16, hi_body, jnp.full((1, tq), -2 ** 15, I32))

    def count(pred):
        def one(j, c):
            hit = jnp.where(pred(j, keys_sc[j]), 1.0, 0.0)
            return c + jnp.sum(hit.reshape(tk // 32, 4, 8, tq), axis=0)

        part = pairs(one, jnp.zeros((4, 8, tq), F32))
        return jnp.sum(jnp.sum(part, axis=0), axis=0, keepdims=True)

    def lo_prep(j, c):
        kk = keys_sc[j]
        lo_sc[j] = jnp.where((kk >> 16) == pre, (kk & 0xFFFF) - 2 ** 15, -2 ** 15).astype(jnp.int16)
        hit = jnp.where((kk >> 16) > pre, 1.0, 0.0)
        return c + jnp.sum(hit.reshape(tk // 32, 4, 8, tq), axis=0)

    above = lax.fori_loop(0, n_tiles, lo_prep, jnp.zeros((4, 8, tq), F32))
    above = jnp.sum(jnp.sum(above, axis=0), axis=0, keepdims=True)
    one_i16 = jnp.ones((tk, tq), jnp.int16)
    zero_i16 = jnp.zeros((tk, tq), jnp.int16)

    def lo_body(b, low):
        cand = low + lax.shift_left(jnp.int32(1), 15 - b)
        cval = jnp.broadcast_to(cand.astype(jnp.int16), (tk, tq))

        def one(j, c):
            hit = jnp.where(lo_sc[j] >= cval, one_i16, zero_i16).reshape(tk // 64, 4, 16, tq)
            for r in range(tk // 64):
                c = c + hit[r]
            return c
        part = pairs(one, jnp.zeros((4, 16, tq), jnp.int16)).astype(I32).astype(F32)
        cnt = above + jnp.sum(jnp.sum(part, axis=0), axis=0, keepdims=True)
        return jnp.where(cnt >= topk, cand, low)

    low = lax.fori_loop(0, 16, lo_body, jnp.full((1, tq), -2 ** 15, I32))
    tau = (pre << 16) + (low + 2 ** 15)
    tau = jnp.where(adm_row < int(topk), INT_MIN, tau)
    need = jnp.where(tau == INT_MIN, -1.0, topk - count(lambda j, kk: kk > tau))
    eq_sc[...] = jnp.zeros(eq_sc.shape, F32)
    tri = jnp.where(_iota((tk, tk), 1) <= _iota((tk, tk), 0), 1.0, 0.0).astype(BF16)

    m_sc[...] = jnp.full(m_sc.shape, NEG, F32)
    acc_sc[...] = jnp.zeros(acc_sc.shape, F32)
    ones = jnp.ones((ONES_ROWS, tk), BF16)
    n_rep = D_HEADS // D_KV_HEADS
    q_stack = [jnp.concatenate([dq_ref[0, :, hd * LANES:(hd + 1) * LANES]
                                for hd in range(g * n_rep, (g + 1) * n_rep)], axis=0) for g in range(D_KV_HEADS)]

    def logits(j):
        kk = keys_sc[j]
        eq = kk == tau
        run = _dot(tri, jnp.where(eq, 1.0, 0.0).astype(BF16)) + eq_sc[...]
        eq_sc[...] = run[tk - 1:tk, :]
        sel = (kk > tau) | (eq & (run <= need))
        bias = jnp.where(sel, 0.0, MASKED)
        bias = jnp.concatenate([bias] * n_rep, axis=1)
        kt = k2_ref[0, pl.ds(pl.multiple_of(j * tk, tk), tk), :]
        ss = [_dot_nt(kt, q_stack[g]) + bias for g in range(D_KV_HEADS)]
        return ss, [jnp.max(s, axis=0, keepdims=True) for s in ss]

    def stage(slot, ss, mts):
        for g in range(D_KV_HEADS):
            s_sc[slot, g] = ss[g]
            mt_sc[slot, g] = mts[g]

    stage(0, *logits(0))

    def consume(j, cur):
        for g in range(D_KV_HEADS):
            m_old = m_sc[g]
            m_new = jnp.maximum(m_old, mt_sc[cur, g])
            p = jnp.exp2(s_sc[cur, g] - m_new).astype(BF16)
            vt = jnp.concatenate([vt_ref[0, j, g * 64:(g + 1) * 64, :], ones], axis=0)
            acc_sc[g] = jnp.exp2(m_old - m_new) * acc_sc[g] + _dot(vt, p)
            m_sc[g] = m_new

    def step(j, cur):
        stage(1 - cur, *logits(j + 1))
        consume(j, cur)

    def attn_body(jj, c):
        step(2 * jj, 0)
        step(2 * jj + 1, 1)
        return c

    lax.fori_loop(0, (n_tiles - 1) // 2, attn_body, 0)

    @pl.when(n_tiles % 2 == 0)
    def _():
        step(n_tiles - 2, 0)
        consume(n_tiles - 1, 1)

    @pl.when(n_tiles % 2 == 1)
    def _():
        consume(n_tiles - 1, 0)
    for g in range(D_KV_HEADS):
        o_g = acc_sc[g, 0:64] / acc_sc[g, 64:65]
        for pr in range(n_rep // 2):
            o_t = jnp.concatenate([o_g[:, (2 * pr) * tq:(2 * pr + 1) * tq],
                                   o_g[:, (2 * pr + 1) * tq:(2 * pr + 2) * tq]], axis=0)
            col = (g * n_rep // 2 + pr) * LANES
            o_ref[0, :, col:col + LANES] = o_t.T.astype(BF16)


def _dsa_attn(dq, iq, iw, ik2, k2, vt, tq, tk, q_start, s_valid):
    bsz, t_q, _ = dq.shape
    s = k2.shape[1]
    topk = min(IDX_TOPK_MAX, s_valid // 4)
    assert (s // tk) * (tk // 64) <= 256, "packed bf16 partial counts must stay exactly representable"
    kern = functools.partial(_dsa_attn_kernel, tq=tq, tk=tk, q_start=q_start, s_valid=s_valid, topk=float(topk))
    return pl.pallas_call(
        kern,
        grid=(bsz, t_q // tq),
        in_specs=[pl.BlockSpec((1, tq, 8 * LANES), lambda b, i: (b, i, 0)),
                  pl.BlockSpec((1, tq, 256), lambda b, i: (b, i, 0)),
                  pl.BlockSpec((1, tq, LANES), lambda b, i: (b, i, 0)),
                  pl.BlockSpec((1, 2, s, LANES), lambda b, i: (b, 0, 0, 0)),
                  pl.BlockSpec((1, s, LANES), lambda b, i: (b, 0, 0)),
                  pl.BlockSpec((1, s // tk, LANES, tk), lambda b, i: (b, 0, 0, 0))],
        out_specs=pl.BlockSpec((1, tq, 512), lambda b, i: (b, i, 0)),
        out_shape=jax.ShapeDtypeStruct((bsz, t_q, 512), BF16),
        scratch_shapes=[pltpu.VMEM((s // tk, tk, tq), I32),
                        pltpu.VMEM((s // tk, tk, tq), BF16),
                        pltpu.VMEM((s // tk, tk, tq), jnp.int16),
                        pltpu.VMEM((1, tq), F32),
                        pltpu.VMEM((D_KV_HEADS, 1, D_HEADS // D_KV_HEADS * tq), F32),
                        pltpu.VMEM((2, D_KV_HEADS, 1, D_HEADS // D_KV_HEADS * tq), F32),
                        pltpu.VMEM((D_KV_HEADS, 64 + ONES_ROWS, D_HEADS // D_KV_HEADS * tq), F32),
                        pltpu.VMEM((2, D_KV_HEADS, tk, D_HEADS // D_KV_HEADS * tq), F32)],
        compiler_params=_params(("parallel", "arbitrary")),
        name="dsa_attn",
    )(dq, iq, iw, ik2, k2, vt)


def _prep_ab(w_in, w_gk):
    w = jnp.pad(w_in, ((0, 0), (0, AB_COLS - w_in.shape[1]))).astype(BF16)
    wgk = jnp.pad(w_gk, ((0, LANES - B_GATE_RANK), (0, 0))).astype(BF16)
    return w, wgk


def _prep_cd(w_in, f_bias):
    o = np.cumsum([0, 512, 512, 512, C_HEADS, 512, 128, 128, 256, IDX_DIM, IDX_HEADS])
    c_q, c_k, c_v, c_f, d_q, d_k, d_v, d_iq, d_ik, d_iw = (w_in[:, o[i]:o[i + 1]] for i in range(10))
    zeros = lambda n: jnp.zeros((D_MODEL, n), w_in.dtype)
    dq_cols = []
    for h in range(D_HEADS):
        g = h // (D_HEADS // D_KV_HEADS)
        wh = d_q[:, h * 64:(h + 1) * 64]
        dq_cols += [wh, zeros(64)] if g == 0 else [zeros(64), wh]
    w = jnp.concatenate([c_q, c_k, c_v] + dq_cols + [d_k, d_v, d_iq, d_ik, c_f, zeros(LANES - IDX_DIM - C_HEADS),
                                                     d_iw, zeros(LANES - IDX_HEADS)], axis=1).astype(BF16)
    fb = jnp.pad(f_bias.astype(F32), (IDX_DIM, LANES - IDX_DIM - C_HEADS)).reshape(1, LANES)
    return w, fb


def _pad_rows(z, s):
    return jnp.pad(z, ((0, 0), (0, s - z.shape[1]), (0, 0)))


def _pad_lanes(z, n=LANES):
    return jnp.pad(z, ((0, 0), (0, 0), (0, n - z.shape[2])))


def _trunk(x, s_a, s_b, cache, wts):
    bsz, t_len, _ = x.shape
    row = lambda z: z.reshape(1, -1).astype(F32)

    t_pad = -(-t_len // REC_CHUNK) * REC_CHUNK
    x1, sa_new, sb_new = _ab_layer(
        _pad_rows(x, t_pad), s_a, s_b.reshape(bsz, 2, 128, 128), row(wts['norm_mix'][0]), wts['ab_w_in'],
        wts['gla_w_gk'], row(wts['gla_b_gk']), row(wts['lb']), row(wts['hgrn_gnorm']), row(wts['gla_gnorm']),
        wts['ab_w_out'], t_len)
    n = bsz * t_len
    x1 = x1[:, :t_len].reshape(n, D_MODEL)
    x2 = _ffn(x1, row(wts['norm_ffn'][0]), wts['ffn_w_in'][0], wts['ffn_w_out'][0])

    tk = 512
    q_start = 0 if cache is None else cache[0].shape[1]
    s_valid = q_start + t_len
    tq_pad = -(-t_len // LANES) * LANES
    if cache is None and t_len % tk == 0:
        (ck, cv, dk, dv, misc, iw, iq, dq, q_aug, k_aug, v_t, k2, dv_t, ik2) = _cd_fused(
            x2.reshape(bsz, t_len, D_MODEL), row(wts['norm_mix'][1]), wts['cd_w_in'], wts['fox_f_bias'], tm=tk)
        logf = misc[:, :, IDX_DIM:IDX_DIM + C_HEADS]
    else:
        fq, ck, cv, dq, dk, dv, iq, misc, iw = _cd_proj(x2, row(wts['norm_mix'][1]), wts['cd_w_in'],
                                                        wts['fox_f_bias'])
        per_b = lambda z: z.reshape(bsz, t_len, z.shape[-1])
        fq, ck, cv, dq, dk, dv, iq, misc, iw = map(per_b, (fq, ck, cv, dq, dk, dv, iq, misc, iw))
        logf = misc[:, :, IDX_DIM:IDX_DIM + C_HEADS]
        if cache is None:
            k_all, v_all, lf_all, dk_all, dv_all, ik_all = ck, cv, _pad_lanes(logf), dk, dv, misc
        else:
            c_k, c_v, c_lf, c_dk, c_dv, c_ik = cache
            cat = lambda c, r: jnp.concatenate([c.reshape(bsz, q_start, -1).astype(F32), r], axis=1)
            k_all, v_all, dk_all, dv_all = cat(c_k, ck), cat(c_v, cv), cat(c_dk, dk), cat(c_dv, dv)
            lf_all = _pad_lanes(cat(c_lf, logf))
            ik_all = cat(_pad_lanes(c_ik), misc)
        s_pad = -(-s_valid // tk) * tk
        k_all, v_all, lf_all, dk_all, dv_all, ik_all = (_pad_rows(z, s_pad) for z in
                                                        (k_all, v_all, lf_all, dk_all, dv_all, ik_all))
        k_aug, v_t, cum = _fox_pack(k_all, v_all, lf_all, tm=tk)
        q_aug = _q_pack(_pad_rows(fq, tq_pad), _pad_rows(cum[:, q_start:q_start + t_len], tq_pad),
                        tm=512 if tq_pad % 512 == 0 else LANES)
        k2, dv_t, ik2 = _dsa_pack(dk_all, dv_all, ik_all, tm=tk)
    o_c = _fox_attn(q_aug, k_aug, v_t, tq=512 if tq_pad % 512 == 0 else LANES, tk=tk, q_start=q_start)[:, :t_len]
    o_d = _dsa_attn(_pad_rows(dq, tq_pad), _pad_rows(iq, tq_pad), _pad_rows(iw, tq_pad), ik2, k2, dv_t,
                    tq=256 if tq_pad % 256 == 0 else LANES, tk=tk, q_start=q_start, s_valid=s_valid)[:, :t_len]

    y = _ffn(x2, row(wts['norm_ffn'][1]), wts['ffn_w_in'][1], wts['ffn_w_out'][1],
             attn=(o_c.reshape(n, 512), o_d.reshape(n, 512), wts['cd_w_out']), g_final=row(wts['norm_final']))
    rows = (ck.reshape(1, bsz, t_len, C_HEADS, C_HD), cv.reshape(1, bsz, t_len, C_HEADS, C_HD),
            logf[None], dk.reshape(1, bsz, t_len, D_KV_HEADS, D_HD), dv.reshape(1, bsz, t_len, D_KV_HEADS, D_HD),
            misc[None, :, :, :IDX_DIM])
    return (y.reshape(bsz, t_len, D_MODEL), sa_new[None], sb_new.reshape(1, bsz, B_HEADS, B_DK, B_DV)) + rows


def kernel(x_prompt, x_sample, state_hgrn, state_gla, cache_fox_k, cache_fox_v, cache_fox_logf, cache_dsa_k, cache_dsa_v, cache_dsa_ik, norm_mix, norm_ffn, norm_final, ab_w_in, ab_w_out, hgrn_lb_logits, hgrn_gnorm, gla_w_gk, gla_b_gk, gla_gnorm, cd_w_in, cd_w_out, fox_f_bias, ffn_w_in, ffn_w_out):
    lbs = jnp.cumsum(jax.nn.softmax(hgrn_lb_logits.astype(F32), axis=0), axis=0)
    w_ab, w_gk = _prep_ab(ab_w_in[0], gla_w_gk[0])
    w_cd, fb = _prep_cd(cd_w_in[0], fox_f_bias[0])
    wts = dict(norm_mix=norm_mix, norm_ffn=norm_ffn, norm_final=norm_final, ab_w_in=w_ab,
               ab_w_out=ab_w_out[0].astype(BF16), lb=lbs[0], hgrn_gnorm=hgrn_gnorm[0], gla_w_gk=w_gk,
               gla_b_gk=gla_b_gk[0], gla_gnorm=gla_gnorm[0], cd_w_in=w_cd, cd_w_out=cd_w_out[0].astype(BF16),
               fox_f_bias=fb, ffn_w_in=ffn_w_in.astype(BF16), ffn_w_out=ffn_w_out.astype(BF16))
    bp = x_prompt.shape[0]
    p_out = _trunk(x_prompt, jnp.zeros((bp, A_HEADS, A_DK, A_DV), F32), jnp.zeros((bp, B_HEADS, B_DK, B_DV), F32),
                   None, wts)
    cache = (cache_fox_k[0], cache_fox_v[0], cache_fox_logf[0], cache_dsa_k[0], cache_dsa_v[0], cache_dsa_ik[0])
    s_out = _trunk(x_sample, state_hgrn[0], state_gla[0], cache, wts)
    return (p_out[0], s_out[0]) + tuple(p_out[1:]) + tuple(s_out[1:])
```

```python
import functools

import numpy as np
import jax
import jax.numpy as jnp
from jax import lax
from jax.experimental import pallas as pl
from jax.experimental.pallas import tpu as pltpu

F32 = jnp.float32
BF16 = jnp.bfloat16
I32 = jnp.int32

D_MODEL = 1024
CHUNK = 64
A_HEADS, A_DK, A_DV = 4, 128, 128
B_HEADS, B_DK, B_DV = 4, 64, 128
B_GATE_RANK = 16
B_GATE_NORM = 16.0
C_HEADS, C_HD = 8, 64
D_HEADS, D_KV_HEADS, D_HD = 8, 2, 64
IDX_HEADS, IDX_DIM = 4, 64
IDX_TOPK_MAX = 256
IDX_SCALE = (IDX_DIM ** -0.5) * (IDX_HEADS ** -0.5)
FFN_HIDDEN = ((8 * D_MODEL // 3 + 255) // 256) * 256

LANES = 128
RSUB = 16
REC_CHUNK = 128
VMEM_LIMIT = 56 * 1024 * 1024
NEG = -1e30
MASKED = -2e30
LOG2E = 1.4426950408889634
ONES_ROWS = 16
INT_MIN = -2 ** 31

AB_COLS = 4 * 512 + 256 + 256 + 512 + 512 + LANES
CD_COLS = 3 * 512 + 8 * LANES + 2 * LANES + 256 + LANES + LANES


def _dot(a, b):
    return jnp.dot(a, b, preferred_element_type=F32)


def _dot_nt(a, b):
    return lax.dot_general(a, b, (((1,), (1,)), ((), ())), preferred_element_type=F32)


def _rms(x, g, eps=1e-6):
    return x * lax.rsqrt(jnp.mean(x * x, axis=-1, keepdims=True) + eps) * g


def _silu(x):
    return x * jax.nn.sigmoid(x)


def _log_sigmoid(x):
    return jnp.minimum(x, 0.0) - jnp.log1p(jnp.exp(-jnp.abs(x)))


def _split3(x):
    hi = x.astype(BF16)
    r = x - hi.astype(F32)
    mid = r.astype(BF16)
    lo = (r - mid.astype(F32)).astype(BF16)
    return hi, mid, lo


def _tri_dot(tri, x):
    hi, mid, lo = _split3(x)
    return _dot(tri, hi) + _dot(tri, mid) + _dot(tri, lo)


def _iota(shape, dim):
    return lax.broadcasted_iota(I32, shape, dim)


def _const_spec(shape):
    zeros = (0,) * len(shape)
    return pl.BlockSpec(shape, lambda *_: zeros, pipeline_mode=pl.Buffered(1))


def _params(sem):
    return pltpu.CompilerParams(dimension_semantics=sem, vmem_limit_bytes=VMEM_LIMIT)


def _ab_kernel(x_ref, sa_ref, sb_ref, g_ref, win_ref, wgk_ref, bgk_ref, lb_ref, agn_ref, bgn_ref, wout_ref,
               xo_ref, sao_ref, sbo_ref, s_sc, h_sc, o_sc, *, t_valid, t_pad):
    C = REC_CHUNK
    tm = x_ref.shape[1]
    t = pl.program_id(1)

    @pl.when(t == 0)
    def _():
        s_sc[0:4] = sa_ref[0]
        s_sc[4:6] = sb_ref[0]

    h_sc[...] = _dot(_rms(x_ref[0], g_ref[...]).astype(BF16), win_ref[...])

    def chunk_body(c, carry):
        r0 = pl.multiple_of(c * C, C)
        _ab_chunk(r0, t * tm + r0, h_sc, o_sc, s_sc, wgk_ref, bgk_ref, lb_ref, agn_ref, bgn_ref,
                  t_valid=t_valid, t_pad=t_pad)
        return carry

    lax.fori_loop(0, tm // C, chunk_body, 0)
    xo_ref[0] = x_ref[0] + _dot(o_sc[...], wout_ref[...])

    @pl.when(t == pl.num_programs(1) - 1)
    def _():
        sao_ref[0] = s_sc[0:4]
        sbo_ref[0] = s_sc[4:6]


def _ab_chunk(r0, row0, h_sc, o_sc, s_sc, wgk_ref, bgk_ref, lb_ref, agn_ref, bgn_ref, *, t_valid, t_pad):
    C = REC_CHUNK
    cols = lambda a, b: h_sc[pl.ds(r0, C), a:b]
    a_q, a_f, a_i, a_g = cols(0, 512), cols(512, 1024), cols(1024, 1536), cols(1536, 2048)
    b_q, b_k, b_v, b_g = cols(2048, 2304), cols(2304, 2560), cols(2560, 3072), cols(3072, 3584)
    b_lr = cols(3584, 3712)

    lb = lb_ref[...]
    f = lb + (1.0 - lb) * jax.nn.sigmoid(a_f)
    gk = _dot(b_lr.astype(BF16), wgk_ref[...]) + bgk_ref[...]
    la = jnp.concatenate([jnp.log2(f), _log_sigmoid(gk) * (LOG2E / B_GATE_NORM)], axis=1)
    q = jnp.concatenate([_silu(a_q), b_q * (B_DK ** -0.5)], axis=1)
    k = jnp.concatenate([1.0 - f, b_k], axis=1)
    v_a, v_b = a_i, b_v
    if t_valid < t_pad:
        ok = (row0 + _iota((C, 1), 0)) < t_valid
        la = jnp.where(ok, la, 0.0)
        k = jnp.where(ok, k, 0.0)
        v_a = jnp.where(ok, v_a, 0.0)
        v_b = jnp.where(ok, v_b, 0.0)

    row = _iota((C, C), 0)
    col = _iota((C, C), 1)
    causal = col <= row
    tri = jnp.where(causal, 1.0, 0.0).astype(BF16)
    tri_in = jnp.where(causal & ((row >> 4) == (col >> 4)), 1.0, 0.0).astype(BF16)
    bc = _tri_dot(tri, la)
    b_in = _tri_dot(tri_in, la)
    lane = _iota((1, LANES), 1)
    lo_half = lane < 64
    srow = _iota((LANES, 1), 0) < 64

    o_heads = [None] * 8
    for u in range(6):
        sl = slice(u * LANES, (u + 1) * LANES)
        qu, ku, bcu = q[:, sl], k[:, sl], bc[:, sl]
        s_old = s_sc[u]
        bend = bcu[C - 1:C, :]
        qt = qu * jnp.exp2(b_in[:, sl])
        qdec = qu * jnp.exp2(bcu)
        if u < 4:
            heads = [(u, None, v_a[:, sl])]
        else:
            ha = 4 + 2 * (u - 4)
            heads = [(ha, lo_half, v_b[:, (ha - 4) * LANES:(ha - 3) * LANES]),
                     (ha + 1, jnp.logical_not(lo_half), v_b[:, (ha - 3) * LANES:(ha - 2) * LANES])]
        a_rows = [[] for _ in heads]
        for i in range(C // RSUB):
            n = RSUB * (i + 1)
            if i == 0:
                kt = ku[0:n] * jnp.exp2(-bcu[0:n])
            else:
                kt = ku[0:n] * jnp.exp2(bcu[RSUB * i - 1:RSUB * i, :] - bcu[0:n])
            if n < C:
                kt = jnp.concatenate([kt, jnp.zeros((C - n, LANES), F32)], axis=0)
            ktb = kt.astype(BF16)
            qi = qt[RSUB * i:RSUB * (i + 1)]
            lhs = jnp.concatenate([qi if msk is None else jnp.where(msk, qi, 0.0) for _, msk, _ in heads], axis=0)
            res = _dot_nt(lhs.astype(BF16), ktb)
            for hi_ in range(len(heads)):
                a_rows[hi_].append(res[RSUB * hi_:RSUB * (hi_ + 1)])
        qd = jnp.concatenate([qdec if msk is None else jnp.where(msk, qdec, 0.0) for _, msk, _ in heads], axis=0)
        o_inter = _dot(qd.astype(BF16), s_old.astype(BF16))
        for hi_, (hd, msk, vh) in enumerate(heads):
            att = jnp.where(causal, jnp.concatenate(a_rows[hi_], axis=0), 0.0)
            o_heads[hd] = o_inter[C * hi_:C * (hi_ + 1)] + _dot(att.astype(BF16), vh.astype(BF16))
        kht = (ku * jnp.exp2(bend - bcu)).T
        dcol = jnp.broadcast_to(jnp.exp2(bend), (LANES, LANES)).T
        if u < 4:
            upd = _dot(kht.astype(BF16), heads[0][2].astype(BF16))
        else:
            lhs = jnp.concatenate([jnp.where(srow, kht, 0.0), jnp.where(srow, 0.0, kht)], axis=1)
            rhs = jnp.concatenate([heads[0][2], heads[1][2]], axis=0)
            upd = _dot(lhs.astype(BF16), rhs.astype(BF16))
        s_sc[u] = dcol * s_old + upd

    outs = []
    for hd in range(8):
        if hd < 4:
            gn, gate = agn_ref[...], a_g[:, hd * LANES:(hd + 1) * LANES]
        else:
            gn, gate = bgn_ref[...], b_g[:, (hd - 4) * LANES:(hd - 3) * LANES]
        outs.append(_rms(o_heads[hd], gn) * _silu(gate))
    o_sc[pl.ds(r0, C), :] = jnp.concatenate(outs, axis=1).astype(BF16)


def _ab_layer(x, s_a, s_b, g, w_in, w_gk, b_gk, lb, a_gn, b_gn, w_out, t_valid):
    bsz, t_pad, _ = x.shape
    tm = 512 if t_pad % 512 == 0 else REC_CHUNK
    kern = functools.partial(_ab_kernel, t_valid=t_valid, t_pad=t_pad)
    return pl.pallas_call(
        kern,
        grid=(bsz, t_pad // tm),
        in_specs=[
            pl.BlockSpec((1, tm, D_MODEL), lambda b, t: (b, t, 0)),
            pl.BlockSpec((1, 4, 128, 128), lambda b, t: (b, 0, 0, 0)),
            pl.BlockSpec((1, 2, 128, 128), lambda b, t: (b, 0, 0, 0)),
            _const_spec((1, D_MODEL)),
            _const_spec((D_MODEL, AB_COLS)),
            _const_spec((LANES, 256)),
            _const_spec((1, 256)),
            _const_spec((1, 512)),
            _const_spec((1, 128)),
            _const_spec((1, 128)),
            _const_spec((D_MODEL, D_MODEL)),
        ],
        out_specs=[
            pl.BlockSpec((1, tm, D_MODEL), lambda b, t: (b, t, 0)),
            pl.BlockSpec((1, 4, 128, 128), lambda b, t: (b, 0, 0, 0)),
            pl.BlockSpec((1, 2, 128, 128), lambda b, t: (b, 0, 0, 0)),
        ],
        out_shape=[
            jax.ShapeDtypeStruct((bsz, t_pad, D_MODEL), F32),
            jax.ShapeDtypeStruct((bsz, 4, 128, 128), F32),
            jax.ShapeDtypeStruct((bsz, 2, 128, 128), F32),
        ],
        scratch_shapes=[pltpu.VMEM((6, 128, 128), F32), pltpu.VMEM((tm, AB_COLS), F32),
                        pltpu.VMEM((tm, D_MODEL), BF16)],
        compiler_params=_params(("parallel", "arbitrary")),
        name="ab_layer",
    )(x, s_a, s_b, g, w_in, w_gk, b_gk, lb, a_gn, b_gn, w_out)


FFN_TILE = FFN_HIDDEN // 2


def _ffn_kernel(*refs, has_attn, has_final):
    refs = list(refs)
    x_ref = refs.pop(0)
    x = x_ref[...]
    if has_attn:
        oc_ref, od_ref, wo_ref = refs.pop(0), refs.pop(0), refs.pop(0)
        x = x + _dot(jnp.concatenate([oc_ref[...], od_ref[...]], axis=1), wo_ref[...])
    g_ref, win_ref, wout_ref = refs.pop(0), refs.pop(0), refs.pop(0)
    gf_ref = refs.pop(0) if has_final else None
    out_ref = refs.pop(0)
    xn = _rms(x, g_ref[...]).astype(BF16)
    acc = x
    for j in range(FFN_HIDDEN // FFN_TILE):
        gate = _dot(xn, win_ref[:, j * FFN_TILE:(j + 1) * FFN_TILE])
        up = _dot(xn, win_ref[:, FFN_HIDDEN + j * FFN_TILE:FFN_HIDDEN + (j + 1) * FFN_TILE])
        act = (_silu(gate) * up).astype(BF16)
        acc = acc + _dot(act, wout_ref[j * FFN_TILE:(j + 1) * FFN_TILE, :])
    if has_final:
        acc = _rms(acc, gf_ref[...])
    out_ref[...] = acc


def _ffn(x, g, w_in, w_out, attn=None, g_final=None, tm=512):
    n = x.shape[0]
    tm = min(tm, n)
    row = lambda i: (i, 0)
    args = [x]
    specs = [pl.BlockSpec((tm, D_MODEL), row)]
    if attn is not None:
        oc, od, wo = attn
        args += [oc, od, wo]
        specs += [pl.BlockSpec((tm, 512), row), pl.BlockSpec((tm, 512), row), _const_spec((D_MODEL, D_MODEL))]
    args += [g, w_in, w_out]
    specs += [_const_spec((1, D_MODEL)), _const_spec((D_MODEL, 2 * FFN_HIDDEN)), _const_spec((FFN_HIDDEN, D_MODEL))]
    if g_final is not None:
        args.append(g_final)
        specs.append(_const_spec((1, D_MODEL)))
    kern = functools.partial(_ffn_kernel, has_attn=attn is not None, has_final=g_final is not None)
    return pl.pallas_call(
        kern,
        grid=(n // tm,),
        in_specs=specs,
        out_specs=pl.BlockSpec((tm, D_MODEL), row),
        out_shape=jax.ShapeDtypeStruct((n, D_MODEL), F32),
        compiler_params=_params(("parallel",)),
        name="ffn",
    )(*args)


def _cd_proj_kernel(x_ref, g_ref, w_ref, fb_ref, fq_ref, ck_ref, cv_ref, dq_ref, dk_ref, dv_ref, iq_ref,
                    misc_ref, iw_ref):
    xn = _rms(x_ref[...], g_ref[...]).astype(BF16)
    h = _dot(xn, w_ref[...])
    fq_ref[...] = (h[:, 0:512] * (C_HD ** -0.5 * LOG2E)).astype(BF16)
    ck_ref[...] = h[:, 512:1024]
    cv_ref[...] = h[:, 1024:1536]
    dq_ref[...] = (h[:, 1536:2560] * (D_HD ** -0.5 * LOG2E)).astype(BF16)
    dk_ref[...] = h[:, 2560:2688]
    dv_ref[...] = h[:, 2688:2816]
    iq_ref[...] = h[:, 2816:3072].astype(BF16)
    misc = h[:, 3072:3200]
    lane = _iota((1, LANES), 1)
    is_f = (lane >= IDX_DIM) & (lane < IDX_DIM + C_HEADS)
    misc_ref[...] = jnp.where(is_f, _log_sigmoid(misc + fb_ref[...]), misc)
    iw_ref[...] = h[:, 3200:3328]


def _cd_proj(x, g, w, fb, tm=512):
    n = x.shape[0]
    tm = min(tm, n)
    row = lambda i: (i, 0)
    widths = [(512, BF16), (512, F32), (512, F32), (1024, BF16), (128, F32), (128, F32), (256, BF16),
              (128, F32), (128, F32)]
    return pl.pallas_call(
        _cd_proj_kernel,
        grid=(n // tm,),
        in_specs=[pl.BlockSpec((tm, D_MODEL), row), _const_spec((1, D_MODEL)), _const_spec((D_MODEL, CD_COLS)),
                  _const_spec((1, LANES))],
        out_specs=[pl.BlockSpec((tm, w_), row) for w_, _ in widths],
        out_shape=[jax.ShapeDtypeStruct((n, w_), dt) for w_, dt in widths],
        compiler_params=_params(("parallel",)),
        name="cd_proj",
    )(x, g, w, fb)


def _aug_consts(is_query):
    p = np.zeros((3, LANES, 8 * LANES), np.float32)
    ones = np.zeros((1, 8 * LANES), np.float32)
    for h in range(8):
        off = h * LANES + (64 if h % 2 == 0 else 0)
        for c in range(3):
            if is_query:
                p[c, h, off + c] = 1.0
                ones[0, off + 3 + c] = 1.0
            else:
                p[c, h, off + 3 + c] = -1.0
                ones[0, off + c] = 1.0
    return jnp.asarray(p, BF16), jnp.asarray(ones, F32)


def _aug_lanes(cum, p_ref, ones_ref):
    hi, mid, lo = _split3(cum * LOG2E)
    return _dot(hi, p_ref[0]) + _dot(mid, p_ref[1]) + _dot(lo, p_ref[2]) + ones_ref[...]


def _running_sum(lf, carry):
    tm = lf.shape[0]

    @pl.when(pl.program_id(1) == 0)
    def _():
        carry[...] = jnp.zeros_like(carry)

    tri = jnp.where(_iota((tm, tm), 1) <= _iota((tm, tm), 0), 1.0, 0.0).astype(BF16)
    cum = _tri_dot(tri, lf) + carry[...]
    carry[...] = cum[tm - 1:tm, :]
    return cum


def _fox_rows(kk, vv, aug, ka_ref, vt_ref):
    lane = _iota((1, LANES), 1)
    for h in range(8):
        pr = slice((h // 2) * LANES, (h // 2 + 1) * LANES)
        own = (lane < 64) if h % 2 == 0 else (lane >= 64)
        ka_ref[0, h] = jnp.where(own, kk[:, pr], aug[:, h * LANES:(h + 1) * LANES]).astype(BF16)
    for pr in range(4):
        vt_ref[0, pr, 0] = vv[:, pr * LANES:(pr + 1) * LANES].T.astype(BF16)


def _fox_pack_kernel(k_ref, v_ref, lf_ref, p_ref, ones_ref, ka_ref, vt_ref, cum_ref, carry):
    cum = _running_sum(lf_ref[0], carry)
    cum_ref[0] = cum
    _fox_rows(k_ref[0], v_ref[0], _aug_lanes(cum, p_ref, ones_ref), ka_ref, vt_ref)


def _fox_pack(k, v, lf, tm):
    bsz, s, _ = k.shape
    p, ones = _aug_consts(False)
    return pl.pallas_call(
        _fox_pack_kernel,
        grid=(bsz, s // tm),
        in_specs=[pl.BlockSpec((1, tm, 512), lambda b, t: (b, t, 0)),
                  pl.BlockSpec((1, tm, 512), lambda b, t: (b, t, 0)),
                  pl.BlockSpec((1, tm, LANES), lambda b, t: (b, t, 0)),
                  _const_spec((3, LANES, 8 * LANES)), _const_spec((1, 8 * LANES))],
        out_specs=[pl.BlockSpec((1, 8, tm, LANES), lambda b, t: (b, 0, t, 0)),
                   pl.BlockSpec((1, 4, 1, LANES, tm), lambda b, t: (b, 0, t, 0, 0)),
                   pl.BlockSpec((1, tm, LANES), lambda b, t: (b, t, 0))],
        out_shape=[jax.ShapeDtypeStruct((bsz, 8, s, LANES), BF16),
                   jax.ShapeDtypeStruct((bsz, 4, s // tm, LANES, tm), BF16),
                   jax.ShapeDtypeStruct((bsz, s, LANES), F32)],
        scratch_shapes=[pltpu.VMEM((1, LANES), F32)],
        compiler_params=_params(("parallel", "arbitrary")),
        name="fox_pack",
    )(k, v, lf, p, ones)


def _q_rows(qq, aug, qa_ref):
    lane = _iota((1, LANES), 1)
    for h in range(8):
        pr = slice((h // 2) * LANES, (h // 2 + 1) * LANES)
        own = (lane < 64) if h % 2 == 0 else (lane >= 64)
        qa_ref[0, h] = jnp.where(own, qq[:, pr], aug[:, h * LANES:(h + 1) * LANES]).astype(BF16)


def _q_pack_kernel(q_ref, cum_ref, p_ref, ones_ref, qa_ref):
    _q_rows(q_ref[0].astype(F32), _aug_lanes(cum_ref[0], p_ref, ones_ref), qa_ref)


def _q_pack(q, cum_q, tm):
    bsz, tq, _ = q.shape
    p, ones = _aug_consts(True)
    return pl.pallas_call(
        _q_pack_kernel,
        grid=(bsz, tq // tm),
        in_specs=[pl.BlockSpec((1, tm, 512), lambda b, t: (b, t, 0)),
                  pl.BlockSpec((1, tm, LANES), lambda b, t: (b, t, 0)),
                  _const_spec((3, LANES, 8 * LANES)), _const_spec((1, 8 * LANES))],
        out_specs=pl.BlockSpec((1, 8, tm, LANES), lambda b, t: (b, 0, t, 0)),
        out_shape=jax.ShapeDtypeStruct((bsz, 8, tq, LANES), BF16),
        compiler_params=_params(("parallel", "parallel")),
        name="q_pack",
    )(q, cum_q, p, ones)


def _fox_attn_kernel(q_ref, k_ref, vt_ref, o_ref, s_sc, mt_sc, *, tq, tk, q_start):
    i = pl.program_id(2)
    q_lo = q_start + i * tq
    n_full = (q_lo + 1) // tk
    n_tiles = (q_lo + tq + tk - 1) // tk
    q_pos = q_lo + _iota((1, tq), 1)
    qs = (q_ref[0, 0], q_ref[0, 1])

    def logits(j, masked):
        off = pl.multiple_of(j * tk, tk)
        ss = [_dot_nt(k_ref[0, hh, pl.ds(off, tk), :], qs[hh]) for hh in range(2)]
        if masked:
            ok = (off + _iota((tk, 1), 0)) <= q_pos
            ss = [jnp.where(ok, s, MASKED) for s in ss]
        return ss, [jnp.max(s, axis=0, keepdims=True) for s in ss]

    def stage(ss, mts):
        for hh in range(2):
            s_sc[hh] = ss[hh]
            mt_sc[hh] = mts[hh]

    stage(*logits(0, True))

    def consume(j, carry):
        out = []
        for hh in range(2):
            m, acc = carry[hh]
            m_new = jnp.maximum(m, mt_sc[hh])
            p = jnp.exp2(s_sc[hh] - m_new).astype(BF16)
            vt = jnp.concatenate([vt_ref[0, 0, j, hh * 64:(hh + 1) * 64, :], ones], axis=0)
            acc = jnp.exp2(m - m_new) * acc + _dot(vt, p)
            out.append((m_new, acc))
        return tuple(out)

    def body(j, carry, masked):
        nxt = logits(j + 1, masked)
        out = consume(j, carry)
        stage(*nxt)
        return out

    ones = jnp.ones((ONES_ROWS, tk), BF16)
    init = (jnp.full((1, tq), NEG, F32), jnp.zeros((64 + ONES_ROWS, tq), F32))
    n_plain = jnp.maximum(n_full - 1, 0)
    carry = lax.fori_loop(0, n_plain, functools.partial(body, masked=False), (init, init))
    carry = lax.fori_loop(n_plain, n_tiles - 1, functools.partial(body, masked=True), carry)
    carry = consume(n_tiles - 1, carry)
    o_t = jnp.concatenate([acc[0:64] / acc[64:65] for _, acc in carry], axis=0)
    o_ref[0] = o_t.T.astype(BF16)


def _fox_attn(q_aug, k_aug, vt, tq, tk, q_start):
    bsz, _, t_q, _ = q_aug.shape
    s = k_aug.shape[2]
    kern = functools.partial(_fox_attn_kernel, tq=tq, tk=tk, q_start=q_start)
    return pl.pallas_call(
        kern,
        grid=(bsz, 4, t_q // tq),
        in_specs=[pl.BlockSpec((1, 2, tq, LANES), lambda b, p, i: (b, p, i, 0)),
                  pl.BlockSpec((1, 2, s, LANES), lambda b, p, i: (b, p, 0, 0)),
                  pl.BlockSpec((1, 1, s // tk, LANES, tk), lambda b, p, i: (b, p, 0, 0, 0))],
        out_specs=pl.BlockSpec((1, tq, LANES), lambda b, p, i: (b, i, p)),
        out_shape=jax.ShapeDtypeStruct((bsz, t_q, 512), BF16),
        scratch_shapes=[pltpu.VMEM((2, tk, tq), F32), pltpu.VMEM((2, 1, tq), F32)],
        compiler_params=_params(("parallel", "parallel", "arbitrary")),
        name="fox_attn",
    )(q_aug, k_aug, vt)


def _dsa_rows(dk, dv, ik, k2_ref, vt_ref, ik2_ref):
    lo = _iota((1, LANES), 1) < 64
    k2_ref[0] = dk.astype(BF16)
    vt_ref[0, 0] = dv.T.astype(BF16)
    ik = jnp.where(lo, ik, 0.0)
    ik2_ref[0, 0] = ik.astype(BF16)
    ik2_ref[0, 1] = pltpu.roll(ik, 64, 1).astype(BF16)


def _dsa_pack_kernel(dk_ref, dv_ref, ik_ref, k2_ref, vt_ref, ik2_ref):
    _dsa_rows(dk_ref[0], dv_ref[0], ik_ref[0], k2_ref, vt_ref, ik2_ref)


def _dsa_pack(dk, dv, ik, tm):
    bsz, s, _ = dk.shape
    tok = pl.BlockSpec((1, tm, LANES), lambda b, t: (b, t, 0))
    return pl.pallas_call(
        _dsa_pack_kernel,
        grid=(bsz, s // tm),
        in_specs=[tok, tok, tok],
        out_specs=[tok,
                   pl.BlockSpec((1, 1, LANES, tm), lambda b, t: (b, t, 0, 0)),
                   pl.BlockSpec((1, 2, tm, LANES), lambda b, t: (b, 0, t, 0))],
        out_shape=[jax.ShapeDtypeStruct((bsz, s, LANES), BF16),
                   jax.ShapeDtypeStruct((bsz, s // tm, LANES, tm), BF16),
                   jax.ShapeDtypeStruct((bsz, 2, s, LANES), BF16)],
        compiler_params=_params(("parallel", "parallel")),
        name="dsa_pack",
    )(dk, dv, ik)


def _cd_fused_kernel(x_ref, g_ref, w_ref, fb_ref, pk_ref, onesk_ref, pq_ref, onesq_ref,
                     ck_ref, cv_ref, dk_ref, dv_ref, misc_ref, iw_ref, iq_ref, dq_ref,
                     qa_ref, ka_ref, vt_ref, k2_ref, dvt_ref, ik2_ref, carry):
    xn = _rms(x_ref[0], g_ref[...]).astype(BF16)
    h = _dot(xn, w_ref[...])
    ck, cv = h[:, 512:1024], h[:, 1024:1536]
    dk, dv = h[:, 2560:2688], h[:, 2688:2816]
    misc = h[:, 3072:3200]
    lane = _iota((1, LANES), 1)
    is_f = (lane >= IDX_DIM) & (lane < IDX_DIM + C_HEADS)
    logf = _log_sigmoid(misc + fb_ref[...])
    misc = jnp.where(is_f, logf, misc)
    ck_ref[0], cv_ref[0], dk_ref[0], dv_ref[0], misc_ref[0] = ck, cv, dk, dv, misc
    iw_ref[0] = h[:, 3200:3328]
    iq_ref[0] = h[:, 2816:3072].astype(BF16)
    dq_ref[0] = (h[:, 1536:2560] * (D_HD ** -0.5 * LOG2E)).astype(BF16)
    cum = _running_sum(pltpu.roll(jnp.where(is_f, logf, 0.0), LANES - IDX_DIM, 1), carry)
    _fox_rows(ck, cv, _aug_lanes(cum, pk_ref, onesk_ref), ka_ref, vt_ref)
    _q_rows(h[:, 0:512] * (C_HD ** -0.5 * LOG2E), _aug_lanes(cum, pq_ref, onesq_ref), qa_ref)
    _dsa_rows(dk, dv, misc, k2_ref, dvt_ref, ik2_ref)


def _cd_fused(x, g, w, fb, tm):
    bsz, t_len, _ = x.shape
    pk, onesk = _aug_consts(False)
    pq, onesq = _aug_consts(True)
    tok = lambda w_: pl.BlockSpec((1, tm, w_), lambda b, t: (b, t, 0))
    hm = pl.BlockSpec((1, 8, tm, LANES), lambda b, t: (b, 0, t, 0))
    sds = jax.ShapeDtypeStruct
    n_t = t_len // tm
    return pl.pallas_call(
        _cd_fused_kernel,
        grid=(bsz, n_t),
        in_specs=[tok(D_MODEL), _const_spec((1, D_MODEL)), _const_spec((D_MODEL, CD_COLS)), _const_spec((1, LANES)),
                  _const_spec((3, LANES, 8 * LANES)), _const_spec((1, 8 * LANES)),
                  _const_spec((3, LANES, 8 * LANES)), _const_spec((1, 8 * LANES))],
        out_specs=[tok(512), tok(512), tok(LANES), tok(LANES), tok(LANES), tok(LANES), tok(256), tok(8 * LANES),
                   hm, hm, pl.BlockSpec((1, 4, 1, LANES, tm), lambda b, t: (b, 0, t, 0, 0)),
                   tok(LANES), pl.BlockSpec((1, 1, LANES, tm), lambda b, t: (b, t, 0, 0)),
                   pl.BlockSpec((1, 2, tm, LANES), lambda b, t: (b, 0, t, 0))],
        out_shape=[sds((bsz, t_len, 512), F32), sds((bsz, t_len, 512), F32), sds((bsz, t_len, LANES), F32),
                   sds((bsz, t_len, LANES), F32), sds((bsz, t_len, LANES), F32), sds((bsz, t_len, LANES), F32),
                   sds((bsz, t_len, 256), BF16), sds((bsz, t_len, 8 * LANES), BF16),
                   sds((bsz, 8, t_len, LANES), BF16), sds((bsz, 8, t_len, LANES), BF16),
                   sds((bsz, 4, n_t, LANES, tm), BF16), sds((bsz, t_len, LANES), BF16),
                   sds((bsz, n_t, LANES, tm), BF16), sds((bsz, 2, t_len, LANES), BF16)],
        scratch_shapes=[pltpu.VMEM((1, LANES), F32)],
        compiler_params=_params(("parallel", "arbitrary")),
        name="cd_fused",
    )(x, g, w, fb, pk, onesk, pq, onesq)


def _dsa_attn_kernel(dq_ref, iq_ref, iw_ref, ik2_ref, k2_ref, vt_ref, o_ref, keys_sc, hi_sc, lo_sc, eq_sc, m_sc, mt_sc, acc_sc,
                     s_sc,
                     *, tq, tk, q_start, s_valid, topk):
    i = pl.program_id(1)
    q_lo = q_start + i * tq
    adm_row = jnp.minimum(((q_lo + _iota((1, tq), 1)) // CHUNK + 1) * CHUNK, s_valid)
    adm_end = jnp.minimum(((q_lo + tq - 1) // CHUNK + 1) * CHUNK, s_valid)
    n_tiles = (adm_end + tk - 1) // tk
    iw_t = iw_ref[0].T
    iq = iq_ref[0]

    def score_body(j, c):
        off = pl.multiple_of(j * tk, tk)
        sc = jnp.zeros((tk, tq), F32)
        for hd in range(IDX_HEADS):
            qp = iq[:, (hd // 2) * LANES:(hd // 2 + 1) * LANES]
            r = jnp.maximum(_dot_nt(ik2_ref[0, hd % 2, pl.ds(off, tk), :], qp), 0.0)
            sc = sc + r * iw_t[hd:hd + 1, :]
        sc = sc * IDX_SCALE
        adm = (off + _iota((tk, tq), 0)) < adm_row
        bits = lax.bitcast_convert_type(sc, I32)
        bits = jnp.where(bits == INT_MIN, 0, bits)
        key = bits ^ ((bits >> 31) & 0x7FFFFFFF)
        keys_sc[j] = jnp.where(adm, key, INT_MIN)
        hi = lax.bitcast_convert_type(bits & -65536, F32)
        hi_sc[j] = jnp.where(adm, hi, -jnp.inf).astype(BF16)
        return c

    lax.fori_loop(0, n_tiles, score_body, 0)

    def pairs(one, init):
        part = lax.fori_loop(0, n_tiles // 2, lambda jj, c: one(2 * jj + 1, one(2 * jj, c)), init)
        return lax.fori_loop(2 * (n_tiles // 2), n_tiles, one, part)

    one16 = jnp.ones((tk, tq), BF16)
    zero16 = jnp.zeros((tk, tq), BF16)

    def hi_body(b, pre):
        cand = pre + lax.shift_left(jnp.int32(1), 15 - b)
        cbits = (cand ^ ((cand >> 15) & 0x7FFF)) & 0xFFFF
        subnormal = ((cbits & 0x7F80) == 0) & ((cbits & 0x007F) != 0)
        cbits = jnp.where(subnormal, jnp.where((cbits & 0x8000) != 0, 0x0000, 0x0080), cbits)
        cval = jnp.broadcast_to(lax.bitcast_convert_type(cbits << 16, F32).astype(BF16), (tk, tq))

        def one(j, c):
            hit = jnp.where(hi_sc[j] >= cval, one16, zero16).reshape(tk // 64, 4, 16, tq)
            for r in range(tk // 64):
                c = c + hit[r]
            return c
        part = pairs(one, jnp.zeros((4, 16, tq), BF16)).astype(F32)
        cnt = jnp.sum(jnp.sum(part, axis=0), axis=0, keepdims=True)
        return jnp.where(cnt >= topk, cand, pre)

    pre = lax.fori_loop(0, 16, hi_body, jnp.full((1, tq), -2 ** 15, I32))

    def lo_prep(j, c):
        kk = keys_sc[j]
        lo_sc[j] = jnp.where((kk >> 16) == pre, (kk & 0xFFFF) - 2 ** 15, -2 ** 15).astype(jnp.int16)
        hit = jnp.where((kk >> 16) > pre, 1.0, 0.0)
        return c + jnp.sum(hit.reshape(tk // 32, 4, 8, tq), axis=0)

    above = lax.fori_loop(0, n_tiles, lo_prep, jnp.zeros((4, 8, tq), F32))
    above = jnp.sum(jnp.sum(above, axis=0), axis=0, keepdims=True)
    one_i16 = jnp.ones((tk, tq), jnp.int16)
    zero_i16 = jnp.zeros((tk, tq), jnp.int16)

    def count_from(cand, strict):
        cval = jnp.broadcast_to(cand.astype(jnp.int16), (tk, tq))

        def one(j, c):
            lo = lo_sc[j]
            hit = jnp.where((lo > cval) if strict else (lo >= cval), one_i16, zero_i16).reshape(tk // 64, 4, 16, tq)
            for r in range(tk // 64):
                c = c + hit[r]
            return c
        part = pairs(one, jnp.zeros((4, 16, tq), jnp.int16)).astype(I32).astype(F32)
        return above + jnp.sum(jnp.sum(part, axis=0), axis=0, keepdims=True)

    def lo_body(b, low):
        cand = low + lax.shift_left(jnp.int32(1), 15 - b)
        return jnp.where(count_from(cand, False) >= topk, cand, low)

    low = lax.fori_loop(0, 16, lo_body, jnp.full((1, tq), -2 ** 15, I32))
    tau = (pre << 16) + (low + 2 ** 15)
    tau = jnp.where(adm_row < int(topk), INT_MIN, tau)
    need = jnp.where(tau == INT_MIN, -1.0, topk - count_from(low, True))
    eq_sc[...] = jnp.zeros(eq_sc.shape, F32)
    tri = jnp.where(_iota((tk, tk), 1) <= _iota((tk, tk), 0), 1.0, 0.0).astype(BF16)

    m_sc[...] = jnp.full(m_sc.shape, NEG, F32)
    acc_sc[...] = jnp.zeros(acc_sc.shape, F32)
    ones = jnp.ones((ONES_ROWS, tk), BF16)
    n_rep = D_HEADS // D_KV_HEADS
    q_stack = [jnp.concatenate([dq_ref[0, :, hd * LANES:(hd + 1) * LANES]
                                for hd in range(g * n_rep, (g + 1) * n_rep)], axis=0) for g in range(D_KV_HEADS)]

    def logits(j):
        kk = keys_sc[j]
        eq = kk == tau
        run = _dot(tri, jnp.where(eq, 1.0, 0.0).astype(BF16)) + eq_sc[...]
        eq_sc[...] = run[tk - 1:tk, :]
        sel = (kk > tau) | (eq & (run <= need))
        bias = jnp.where(sel, 0.0, MASKED)
        bias = jnp.concatenate([bias] * n_rep, axis=1)
        kt = k2_ref[0, pl.ds(pl.multiple_of(j * tk, tk), tk), :]
        ss = [_dot_nt(kt, q_stack[g]) + bias for g in range(D_KV_HEADS)]
        return ss, [jnp.max(s, axis=0, keepdims=True) for s in ss]

    def stage(slot, ss, mts):
        for g in range(D_KV_HEADS):
            s_sc[slot, g] = ss[g]
            mt_sc[slot, g] = mts[g]

    stage(0, *logits(0))

    def consume(j, cur):
        for g in range(D_KV_HEADS):
            m_old = m_sc[g]
            m_new = jnp.maximum(m_old, mt_sc[cur, g])
            p = jnp.exp2(s_sc[cur, g] - m_new).astype(BF16)
            vt = jnp.concatenate([vt_ref[0, j, g * 64:(g + 1) * 64, :], ones], axis=0)
            acc_sc[g] = jnp.exp2(m_old - m_new) * acc_sc[g] + _dot(vt, p)
            m_sc[g] = m_new

    def step(j, cur):
        stage(1 - cur, *logits(j + 1))
        consume(j, cur)

    def attn_body(jj, c):
        step(2 * jj, 0)
        step(2 * jj + 1, 1)
        return c

    lax.fori_loop(0, (n_tiles - 1) // 2, attn_body, 0)

    @pl.when(n_tiles % 2 == 0)
    def _():
        step(n_tiles - 2, 0)
        consume(n_tiles - 1, 1)

    @pl.when(n_tiles % 2 == 1)
    def _():
        consume(n_tiles - 1, 0)
    for g in range(D_KV_HEADS):
        o_g = acc_sc[g, 0:64] / acc_sc[g, 64:65]
        for pr in range(n_rep // 2):
            o_t = jnp.concatenate([o_g[:, (2 * pr) * tq:(2 * pr + 1) * tq],
                                   o_g[:, (2 * pr + 1) * tq:(2 * pr + 2) * tq]], axis=0)
            col = (g * n_rep // 2 + pr) * LANES
            o_ref[0, :, col:col + LANES] = o_t.T.astype(BF16)


def _dsa_attn(dq, iq, iw, ik2, k2, vt, tq, tk, q_start, s_valid):
    bsz, t_q, _ = dq.shape
    s = k2.shape[1]
    topk = min(IDX_TOPK_MAX, s_valid // 4)
    assert (s // tk) * (tk // 64) <= 256, "packed bf16 partial counts must stay exactly representable"
    kern = functools.partial(_dsa_attn_kernel, tq=tq, tk=tk, q_start=q_start, s_valid=s_valid, topk=float(topk))
    return pl.pallas_call(
        kern,
        grid=(bsz, t_q // tq),
        in_specs=[pl.BlockSpec((1, tq, 8 * LANES), lambda b, i: (b, i, 0)),
                  pl.BlockSpec((1, tq, 256), lambda b, i: (b, i, 0)),
                  pl.BlockSpec((1, tq, LANES), lambda b, i: (b, i, 0)),
                  pl.BlockSpec((1, 2, s, LANES), lambda b, i: (b, 0, 0, 0)),
                  pl.BlockSpec((1, s, LANES), lambda b, i: (b, 0, 0)),
                  pl.BlockSpec((1, s // tk, LANES, tk), lambda b, i: (b, 0, 0, 0))],
        out_specs=pl.BlockSpec((1, tq, 512), lambda b, i: (b, i, 0)),
        out_shape=jax.ShapeDtypeStruct((bsz, t_q, 512), BF16),
        scratch_shapes=[pltpu.VMEM((s // tk, tk, tq), I32),
                        pltpu.VMEM((s // tk, tk, tq), BF16),
                        pltpu.VMEM((s // tk, tk, tq), jnp.int16),
                        pltpu.VMEM((1, tq), F32),
                        pltpu.VMEM((D_KV_HEADS, 1, D_HEADS // D_KV_HEADS * tq), F32),
                        pltpu.VMEM((2, D_KV_HEADS, 1, D_HEADS // D_KV_HEADS * tq), F32),
                        pltpu.VMEM((D_KV_HEADS, 64 + ONES_ROWS, D_HEADS // D_KV_HEADS * tq), F32),
                        pltpu.VMEM((2, D_KV_HEADS, tk, D_HEADS // D_KV_HEADS * tq), F32)],
        compiler_params=_params(("parallel", "arbitrary")),
        name="dsa_attn",
    )(dq, iq, iw, ik2, k2, vt)


def _prep_ab(w_in, w_gk):
    w = jnp.pad(w_in, ((0, 0), (0, AB_COLS - w_in.shape[1]))).astype(BF16)
    wgk = jnp.pad(w_gk, ((0, LANES - B_GATE_RANK), (0, 0))).astype(BF16)
    return w, wgk


def _prep_cd(w_in, f_bias):
    o = np.cumsum([0, 512, 512, 512, C_HEADS, 512, 128, 128, 256, IDX_DIM, IDX_HEADS])
    c_q, c_k, c_v, c_f, d_q, d_k, d_v, d_iq, d_ik, d_iw = (w_in[:, o[i]:o[i + 1]] for i in range(10))
    zeros = lambda n: jnp.zeros((D_MODEL, n), w_in.dtype)
    dq_cols = []
    for h in range(D_HEADS):
        g = h // (D_HEADS // D_KV_HEADS)
        wh = d_q[:, h * 64:(h + 1) * 64]
        dq_cols += [wh, zeros(64)] if g == 0 else [zeros(64), wh]
    w = jnp.concatenate([c_q, c_k, c_v] + dq_cols + [d_k, d_v, d_iq, d_ik, c_f, zeros(LANES - IDX_DIM - C_HEADS),
                                                     d_iw, zeros(LANES - IDX_HEADS)], axis=1).astype(BF16)
    fb = jnp.pad(f_bias.astype(F32), (IDX_DIM, LANES - IDX_DIM - C_HEADS)).reshape(1, LANES)
    return w, fb


def _pad_rows(z, s):
    return jnp.pad(z, ((0, 0), (0, s - z.shape[1]), (0, 0)))


def _pad_lanes(z, n=LANES):
    return jnp.pad(z, ((0, 0), (0, 0), (0, n - z.shape[2])))


def _trunk(x, s_a, s_b, cache, wts):
    bsz, t_len, _ = x.shape
    row = lambda z: z.reshape(1, -1).astype(F32)

    t_pad = -(-t_len // REC_CHUNK) * REC_CHUNK
    x1, sa_new, sb_new = _ab_layer(
        _pad_rows(x, t_pad), s_a, s_b.reshape(bsz, 2, 128, 128), row(wts['norm_mix'][0]), wts['ab_w_in'],
        wts['gla_w_gk'], row(wts['gla_b_gk']), row(wts['lb']), row(wts['hgrn_gnorm']), row(wts['gla_gnorm']),
        wts['ab_w_out'], t_len)
    n = bsz * t_len
    x1 = x1[:, :t_len].reshape(n, D_MODEL)
    x2 = _ffn(x1, row(wts['norm_ffn'][0]), wts['ffn_w_in'][0], wts['ffn_w_out'][0])

    tk = 512
    q_start = 0 if cache is None else cache[0].shape[1]
    s_valid = q_start + t_len
    tq_pad = -(-t_len // LANES) * LANES
    if cache is None and t_len % tk == 0:
        (ck, cv, dk, dv, misc, iw, iq, dq, q_aug, k_aug, v_t, k2, dv_t, ik2) = _cd_fused(
            x2.reshape(bsz, t_len, D_MODEL), row(wts['norm_mix'][1]), wts['cd_w_in'], wts['fox_f_bias'], tm=tk)
        logf = misc[:, :, IDX_DIM:IDX_DIM + C_HEADS]
    else:
        fq, ck, cv, dq, dk, dv, iq, misc, iw = _cd_proj(x2, row(wts['norm_mix'][1]), wts['cd_w_in'],
                                                        wts['fox_f_bias'])
        per_b = lambda z: z.reshape(bsz, t_len, z.shape[-1])
        fq, ck, cv, dq, dk, dv, iq, misc, iw = map(per_b, (fq, ck, cv, dq, dk, dv, iq, misc, iw))
        logf = misc[:, :, IDX_DIM:IDX_DIM + C_HEADS]
        if cache is None:
            k_all, v_all, lf_all, dk_all, dv_all, ik_all = ck, cv, _pad_lanes(logf), dk, dv, misc
        else:
            c_k, c_v, c_lf, c_dk, c_dv, c_ik = cache
            cat = lambda c, r: jnp.concatenate([c.reshape(bsz, q_start, -1).astype(F32), r], axis=1)
            k_all, v_all, dk_all, dv_all = cat(c_k, ck), cat(c_v, cv), cat(c_dk, dk), cat(c_dv, dv)
            lf_all = _pad_lanes(cat(c_lf, logf))
            ik_all = cat(_pad_lanes(c_ik), misc)
        s_pad = -(-s_valid // tk) * tk
        k_all, v_all, lf_all, dk_all, dv_all, ik_all = (_pad_rows(z, s_pad) for z in
                                                        (k_all, v_all, lf_all, dk_all, dv_all, ik_all))
        k_aug, v_t, cum = _fox_pack(k_all, v_all, lf_all, tm=tk)
        q_aug = _q_pack(_pad_rows(fq, tq_pad), _pad_rows(cum[:, q_start:q_start + t_len], tq_pad),
                        tm=512 if tq_pad % 512 == 0 else LANES)
        k2, dv_t, ik2 = _dsa_pack(dk_all, dv_all, ik_all, tm=tk)
    o_c = _fox_attn(q_aug, k_aug, v_t, tq=512 if tq_pad % 512 == 0 else LANES, tk=tk, q_start=q_start)[:, :t_len]
    o_d = _dsa_attn(_pad_rows(dq, tq_pad), _pad_rows(iq, tq_pad), _pad_rows(iw, tq_pad), ik2, k2, dv_t,
                    tq=256 if tq_pad % 256 == 0 else LANES, tk=tk, q_start=q_start, s_valid=s_valid)[:, :t_len]

    y = _ffn(x2, row(wts['norm_ffn'][1]), wts['ffn_w_in'][1], wts['ffn_w_out'][1],
             attn=(o_c.reshape(n, 512), o_d.reshape(n, 512), wts['cd_w_out']), g_final=row(wts['norm_final']))
    rows = (ck.reshape(1, bsz, t_len, C_HEADS, C_HD), cv.reshape(1, bsz, t_len, C_HEADS, C_HD),
            logf[None], dk.reshape(1, bsz, t_len, D_KV_HEADS, D_HD), dv.reshape(1, bsz, t_len, D_KV_HEADS, D_HD),
            misc[None, :, :, :IDX_DIM])
    return (y.reshape(bsz, t_len, D_MODEL), sa_new[None], sb_new.reshape(1, bsz, B_HEADS, B_DK, B_DV)) + rows


def kernel(x_prompt, x_sample, state_hgrn, state_gla, cache_fox_k, cache_fox_v, cache_fox_logf, cache_dsa_k, cache_dsa_v, cache_dsa_ik, norm_mix, norm_ffn, norm_final, ab_w_in, ab_w_out, hgrn_lb_logits, hgrn_gnorm, gla_w_gk, gla_b_gk, gla_gnorm, cd_w_in, cd_w_out, fox_f_bias, ffn_w_in, ffn_w_out):
    lbs = jnp.cumsum(jax.nn.softmax(hgrn_lb_logits.astype(F32), axis=0), axis=0)
    w_ab, w_gk = _prep_ab(ab_w_in[0], gla_w_gk[0])
    w_cd, fb = _prep_cd(cd_w_in[0], fox_f_bias[0])
    wts = dict(norm_mix=norm_mix, norm_ffn=norm_ffn, norm_final=norm_final, ab_w_in=w_ab,
               ab_w_out=ab_w_out[0].astype(BF16), lb=lbs[0], hgrn_gnorm=hgrn_gnorm[0], gla_w_gk=w_gk,
               gla_b_gk=gla_b_gk[0], gla_gnorm=gla_gnorm[0], cd_w_in=w_cd, cd_w_out=cd_w_out[0].astype(BF16),
               fox_f_bias=fb, ffn_w_in=ffn_w_in.astype(BF16), ffn_w_out=ffn_w_out.astype(BF16))
    bp = x_prompt.shape[0]
    p_out = _trunk(x_prompt, jnp.zeros((bp, A_HEADS, A_DK, A_DV), F32), jnp.zeros((bp, B_HEADS, B_DK, B_DV), F32),
                   None, wts)
    cache = (cache_fox_k[0], cache_fox_v[0], cache_fox_logf[0], cache_dsa_k[0], cache_dsa_v[0], cache_dsa_ik[0])
    s_out = _trunk(x_sample, state_hgrn[0], state_gla[0], cache, wts)
    return (p_out[0], s_out[0]) + tuple(p_out[1:]) + tuple(s_out[1:])
```

```python
import functools

import numpy as np
import jax
import jax.numpy as jnp
from jax import lax
from jax.experimental import pallas as pl
from jax.experimental.pallas import tpu as pltpu

F32 = jnp.float32
BF16 = jnp.bfloat16
I32 = jnp.int32

D_MODEL = 1024
CHUNK = 64
A_HEADS, A_DK, A_DV = 4, 128, 128
B_HEADS, B_DK, B_DV = 4, 64, 128
B_GATE_RANK = 16
B_GATE_NORM = 16.0
C_HEADS, C_HD = 8, 64
D_HEADS, D_KV_HEADS, D_HD = 8, 2, 64
IDX_HEADS, IDX_DIM = 4, 64
IDX_TOPK_MAX = 256
IDX_SCALE = (IDX_DIM ** -0.5) * (IDX_HEADS ** -0.5)
FFN_HIDDEN = ((8 * D_MODEL // 3 + 255) // 256) * 256

LANES = 128
RSUB = 16
REC_CHUNK = 128
VMEM_LIMIT = 56 * 1024 * 1024
NEG = -1e30
MASKED = -2e30
LOG2E = 1.4426950408889634
ONES_ROWS = 16
INT_MIN = -2 ** 31

AB_COLS = 4 * 512 + 256 + 256 + 512 + 512 + LANES
CD_COLS = 3 * 512 + 8 * LANES + 2 * LANES + 256 + LANES + LANES


def _dot(a, b):
    return jnp.dot(a, b, preferred_element_type=F32)


def _dot_nt(a, b):
    return lax.dot_general(a, b, (((1,), (1,)), ((), ())), preferred_element_type=F32)


def _rms(x, g, eps=1e-6):
    return x * lax.rsqrt(jnp.mean(x * x, axis=-1, keepdims=True) + eps) * g


def _silu(x):
    return x * jax.nn.sigmoid(x)


def _log_sigmoid(x):
    return jnp.minimum(x, 0.0) - jnp.log1p(jnp.exp(-jnp.abs(x)))


def _split3(x):
    hi = x.astype(BF16)
    r = x - hi.astype(F32)
    mid = r.astype(BF16)
    lo = (r - mid.astype(F32)).astype(BF16)
    return hi, mid, lo


def _tri_dot(tri, x):
    hi, mid, lo = _split3(x)
    return _dot(tri, hi) + _dot(tri, mid) + _dot(tri, lo)


def _iota(shape, dim):
    return lax.broadcasted_iota(I32, shape, dim)


def _const_spec(shape):
    zeros = (0,) * len(shape)
    return pl.BlockSpec(shape, lambda *_: zeros, pipeline_mode=pl.Buffered(1))


def _params(sem):
    return pltpu.CompilerParams(dimension_semantics=sem, vmem_limit_bytes=VMEM_LIMIT)


def _ab_kernel(x_ref, sa_ref, sb_ref, g_ref, win_ref, wgk_ref, bgk_ref, lb_ref, agn_ref, bgn_ref, wout_ref,
               xo_ref, sao_ref, sbo_ref, s_sc, h_sc, o_sc, *, t_valid, t_pad):
    C = REC_CHUNK
    tm = x_ref.shape[1]
    t = pl.program_id(1)

    @pl.when(t == 0)
    def _():
        s_sc[0:4] = sa_ref[0]
        s_sc[4:6] = sb_ref[0]

    h_sc[...] = _dot(_rms(x_ref[0], g_ref[...]).astype(BF16), win_ref[...])

    def chunk_body(c, carry):
        r0 = pl.multiple_of(c * C, C)
        _ab_chunk(r0, t * tm + r0, h_sc, o_sc, s_sc, wgk_ref, bgk_ref, lb_ref, agn_ref, bgn_ref,
                  t_valid=t_valid, t_pad=t_pad)
        return carry

    lax.fori_loop(0, tm // C, chunk_body, 0)
    xo_ref[0] = x_ref[0] + _dot(o_sc[...], wout_ref[...])

    @pl.when(t == pl.num_programs(1) - 1)
    def _():
        sao_ref[0] = s_sc[0:4]
        sbo_ref[0] = s_sc[4:6]


def _ab_chunk(r0, row0, h_sc, o_sc, s_sc, wgk_ref, bgk_ref, lb_ref, agn_ref, bgn_ref, *, t_valid, t_pad):
    C = REC_CHUNK
    cols = lambda a, b: h_sc[pl.ds(r0, C), a:b]
    a_q, a_f, a_i, a_g = cols(0, 512), cols(512, 1024), cols(1024, 1536), cols(1536, 2048)
    b_q, b_k, b_v, b_g = cols(2048, 2304), cols(2304, 2560), cols(2560, 3072), cols(3072, 3584)
    b_lr = cols(3584, 3712)

    lb = lb_ref[...]
    f = lb + (1.0 - lb) * jax.nn.sigmoid(a_f)
    gk = _dot(b_lr.astype(BF16), wgk_ref[...]) + bgk_ref[...]
    la = jnp.concatenate([jnp.log2(f), _log_sigmoid(gk) * (LOG2E / B_GATE_NORM)], axis=1)
    q = jnp.concatenate([_silu(a_q), b_q * (B_DK ** -0.5)], axis=1)
    k = jnp.concatenate([1.0 - f, b_k], axis=1)
    v_a, v_b = a_i, b_v
    if t_valid < t_pad:
        ok = (row0 + _iota((C, 1), 0)) < t_valid
        la = jnp.where(ok, la, 0.0)
        k = jnp.where(ok, k, 0.0)
        v_a = jnp.where(ok, v_a, 0.0)
        v_b = jnp.where(ok, v_b, 0.0)

    row = _iota((C, C), 0)
    col = _iota((C, C), 1)
    causal = col <= row
    tri = jnp.where(causal, 1.0, 0.0).astype(BF16)
    tri_in = jnp.where(causal & ((row >> 4) == (col >> 4)), 1.0, 0.0).astype(BF16)
    bc = _tri_dot(tri, la)
    b_in = _tri_dot(tri_in, la)
    lane = _iota((1, LANES), 1)
    lo_half = lane < 64
    srow = _iota((LANES, 1), 0) < 64

    o_heads = [None] * 8
    for u in range(6):
        sl = slice(u * LANES, (u + 1) * LANES)
        qu, ku, bcu = q[:, sl], k[:, sl], bc[:, sl]
        s_old = s_sc[u]
        bend = bcu[C - 1:C, :]
        qt = qu * jnp.exp2(b_in[:, sl])
        qdec = qu * jnp.exp2(bcu)
        if u < 4:
            heads = [(u, None, v_a[:, sl])]
        else:
            ha = 4 + 2 * (u - 4)
            heads = [(ha, lo_half, v_b[:, (ha - 4) * LANES:(ha - 3) * LANES]),
                     (ha + 1, jnp.logical_not(lo_half), v_b[:, (ha - 3) * LANES:(ha - 2) * LANES])]
        a_rows = [[] for _ in heads]
        for i in range(C // RSUB):
            n = RSUB * (i + 1)
            if i == 0:
                kt = ku[0:n] * jnp.exp2(-bcu[0:n])
            else:
                kt = ku[0:n] * jnp.exp2(bcu[RSUB * i - 1:RSUB * i, :] - bcu[0:n])
            if n < C:
                kt = jnp.concatenate([kt, jnp.zeros((C - n, LANES), F32)], axis=0)
            ktb = kt.astype(BF16)
            qi = qt[RSUB * i:RSUB * (i + 1)]
            lhs = jnp.concatenate([qi if msk is None else jnp.where(msk, qi, 0.0) for _, msk, _ in heads], axis=0)
            res = _dot_nt(lhs.astype(BF16), ktb)
            for hi_ in range(len(heads)):
                a_rows[hi_].append(res[RSUB * hi_:RSUB * (hi_ + 1)])
        qd = jnp.concatenate([qdec if msk is None else jnp.where(msk, qdec, 0.0) for _, msk, _ in heads], axis=0)
        o_inter = _dot(qd.astype(BF16), s_old.astype(BF16))
        for hi_, (hd, msk, vh) in enumerate(heads):
            att = jnp.where(causal, jnp.concatenate(a_rows[hi_], axis=0), 0.0)
            o_heads[hd] = o_inter[C * hi_:C * (hi_ + 1)] + _dot(att.astype(BF16), vh.astype(BF16))
        kht = (ku * jnp.exp2(bend - bcu)).T
        dcol = jnp.broadcast_to(jnp.exp2(bend), (LANES, LANES)).T
        if u < 4:
            upd = _dot(kht.astype(BF16), heads[0][2].astype(BF16))
        else:
            lhs = jnp.concatenate([jnp.where(srow, kht, 0.0), jnp.where(srow, 0.0, kht)], axis=1)
            rhs = jnp.concatenate([heads[0][2], heads[1][2]], axis=0)
            upd = _dot(lhs.astype(BF16), rhs.astype(BF16))
        s_sc[u] = dcol * s_old + upd

    outs = []
    for hd in range(8):
        if hd < 4:
            gn, gate = agn_ref[...], a_g[:, hd * LANES:(hd + 1) * LANES]
        else:
            gn, gate = bgn_ref[...], b_g[:, (hd - 4) * LANES:(hd - 3) * LANES]
        outs.append(_rms(o_heads[hd], gn) * _silu(gate))
    o_sc[pl.ds(r0, C), :] = jnp.concatenate(outs, axis=1).astype(BF16)


def _ab_layer(x, s_a, s_b, g, w_in, w_gk, b_gk, lb, a_gn, b_gn, w_out, t_valid):
    bsz, t_pad, _ = x.shape
    tm = 512 if t_pad % 512 == 0 else REC_CHUNK
    kern = functools.partial(_ab_kernel, t_valid=t_valid, t_pad=t_pad)
    return pl.pallas_call(
        kern,
        grid=(bsz, t_pad // tm),
        in_specs=[
            pl.BlockSpec((1, tm, D_MODEL), lambda b, t: (b, t, 0)),
            pl.BlockSpec((1, 4, 128, 128), lambda b, t: (b, 0, 0, 0)),
            pl.BlockSpec((1, 2, 128, 128), lambda b, t: (b, 0, 0, 0)),
            _const_spec((1, D_MODEL)),
            _const_spec((D_MODEL, AB_COLS)),
            _const_spec((LANES, 256)),
            _const_spec((1, 256)),
            _const_spec((1, 512)),
            _const_spec((1, 128)),
            _const_spec((1, 128)),
            _const_spec((D_MODEL, D_MODEL)),
        ],
        out_specs=[
            pl.BlockSpec((1, tm, D_MODEL), lambda b, t: (b, t, 0)),
            pl.BlockSpec((1, 4, 128, 128), lambda b, t: (b, 0, 0, 0)),
            pl.BlockSpec((1, 2, 128, 128), lambda b, t: (b, 0, 0, 0)),
        ],
        out_shape=[
            jax.ShapeDtypeStruct((bsz, t_pad, D_MODEL), F32),
            jax.ShapeDtypeStruct((bsz, 4, 128, 128), F32),
            jax.ShapeDtypeStruct((bsz, 2, 128, 128), F32),
        ],
        scratch_shapes=[pltpu.VMEM((6, 128, 128), F32), pltpu.VMEM((tm, AB_COLS), F32),
                        pltpu.VMEM((tm, D_MODEL), BF16)],
        compiler_params=_params(("parallel", "arbitrary")),
        name="ab_layer",
    )(x, s_a, s_b, g, w_in, w_gk, b_gk, lb, a_gn, b_gn, w_out)


FFN_TILE = 256


def _ffn_kernel(*refs, has_attn, has_final):
    refs = list(refs)
    x_ref = refs.pop(0)
    x = x_ref[...]
    if has_attn:
        oc_ref, od_ref, wo_ref = refs.pop(0), refs.pop(0), refs.pop(0)
        x = x + _dot(jnp.concatenate([oc_ref[...], od_ref[...]], axis=1), wo_ref[...])
    g_ref, win_ref, wout_ref = refs.pop(0), refs.pop(0), refs.pop(0)
    gf_ref = refs.pop(0) if has_final else None
    out_ref = refs.pop(0)
    xn = _rms(x, g_ref[...]).astype(BF16)
    acc = x
    for j in range(FFN_HIDDEN // FFN_TILE):
        gate = _dot(xn, win_ref[:, j * FFN_TILE:(j + 1) * FFN_TILE])
        up = _dot(xn, win_ref[:, FFN_HIDDEN + j * FFN_TILE:FFN_HIDDEN + (j + 1) * FFN_TILE])
        act = (_silu(gate) * up).astype(BF16)
        acc = acc + _dot(act, wout_ref[j * FFN_TILE:(j + 1) * FFN_TILE, :])
    if has_final:
        acc = _rms(acc, gf_ref[...])
    out_ref[...] = acc


def _ffn(x, g, w_in, w_out, attn=None, g_final=None, tm=1024):
    n = x.shape[0]
    tm = min(tm, n)
    row = lambda i: (i, 0)
    args = [x]
    specs = [pl.BlockSpec((tm, D_MODEL), row)]
    if attn is not None:
        oc, od, wo = attn
        args += [oc, od, wo]
        specs += [pl.BlockSpec((tm, 512), row), pl.BlockSpec((tm, 512), row), _const_spec((D_MODEL, D_MODEL))]
    args += [g, w_in, w_out]
    specs += [_const_spec((1, D_MODEL)), _const_spec((D_MODEL, 2 * FFN_HIDDEN)), _const_spec((FFN_HIDDEN, D_MODEL))]
    if g_final is not None:
        args.append(g_final)
        specs.append(_const_spec((1, D_MODEL)))
    kern = functools.partial(_ffn_kernel, has_attn=attn is not None, has_final=g_final is not None)
    return pl.pallas_call(
        kern,
        grid=(n // tm,),
        in_specs=specs,
        out_specs=pl.BlockSpec((tm, D_MODEL), row),
        out_shape=jax.ShapeDtypeStruct((n, D_MODEL), F32),
        compiler_params=_params(("parallel",)),
        name="ffn",
    )(*args)


def _cd_proj_kernel(x_ref, g_ref, w_ref, fb_ref, fq_ref, ck_ref, cv_ref, dq_ref, dk_ref, dv_ref, iq_ref,
                    misc_ref, iw_ref):
    xn = _rms(x_ref[...], g_ref[...]).astype(BF16)
    h = _dot(xn, w_ref[...])
    fq_ref[...] = (h[:, 0:512] * (C_HD ** -0.5 * LOG2E)).astype(BF16)
    ck_ref[...] = h[:, 512:1024]
    cv_ref[...] = h[:, 1024:1536]
    dq_ref[...] = (h[:, 1536:2560] * (D_HD ** -0.5 * LOG2E)).astype(BF16)
    dk_ref[...] = h[:, 2560:2688]
    dv_ref[...] = h[:, 2688:2816]
    iq_ref[...] = h[:, 2816:3072].astype(BF16)
    misc = h[:, 3072:3200]
    lane = _iota((1, LANES), 1)
    is_f = (lane >= IDX_DIM) & (lane < IDX_DIM + C_HEADS)
    misc_ref[...] = jnp.where(is_f, _log_sigmoid(misc + fb_ref[...]), misc)
    iw_ref[...] = h[:, 3200:3328]


def _cd_proj(x, g, w, fb, tm=512):
    n = x.shape[0]
    tm = min(tm, n)
    row = lambda i: (i, 0)
    widths = [(512, BF16), (512, F32), (512, F32), (1024, BF16), (128, F32), (128, F32), (256, BF16),
              (128, F32), (128, F32)]
    return pl.pallas_call(
        _cd_proj_kernel,
        grid=(n // tm,),
        in_specs=[pl.BlockSpec((tm, D_MODEL), row), _const_spec((1, D_MODEL)), _const_spec((D_MODEL, CD_COLS)),
                  _const_spec((1, LANES))],
        out_specs=[pl.BlockSpec((tm, w_), row) for w_, _ in widths],
        out_shape=[jax.ShapeDtypeStruct((n, w_), dt) for w_, dt in widths],
        compiler_params=_params(("parallel",)),
        name="cd_proj",
    )(x, g, w, fb)


def _aug_consts(is_query):
    p = np.zeros((3, LANES, 8 * LANES), np.float32)
    ones = np.zeros((1, 8 * LANES), np.float32)
    for h in range(8):
        off = h * LANES + (64 if h % 2 == 0 else 0)
        for c in range(3):
            if is_query:
                p[c, h, off + c] = 1.0
                ones[0, off + 3 + c] = 1.0
            else:
                p[c, h, off + 3 + c] = -1.0
                ones[0, off + c] = 1.0
    return jnp.asarray(p, BF16), jnp.asarray(ones, F32)


def _aug_lanes(cum, p_ref, ones_ref):
    hi, mid, lo = _split3(cum * LOG2E)
    return _dot(hi, p_ref[0]) + _dot(mid, p_ref[1]) + _dot(lo, p_ref[2]) + ones_ref[...]


def _running_sum(lf, carry):
    tm = lf.shape[0]

    @pl.when(pl.program_id(1) == 0)
    def _():
        carry[...] = jnp.zeros_like(carry)

    tri = jnp.where(_iota((tm, tm), 1) <= _iota((tm, tm), 0), 1.0, 0.0).astype(BF16)
    cum = _tri_dot(tri, lf) + carry[...]
    carry[...] = cum[tm - 1:tm, :]
    return cum


def _fox_rows(kk, vv, aug, ka_ref, vt_ref):
    lane = _iota((1, LANES), 1)
    for h in range(8):
        pr = slice((h // 2) * LANES, (h // 2 + 1) * LANES)
        own = (lane < 64) if h % 2 == 0 else (lane >= 64)
        ka_ref[0, h] = jnp.where(own, kk[:, pr], aug[:, h * LANES:(h + 1) * LANES]).astype(BF16)
    for pr in range(4):
        vt_ref[0, pr, 0] = vv[:, pr * LANES:(pr + 1) * LANES].T.astype(BF16)


def _fox_pack_kernel(k_ref, v_ref, lf_ref, p_ref, ones_ref, ka_ref, vt_ref, cum_ref, carry):
    cum = _running_sum(lf_ref[0], carry)
    cum_ref[0] = cum
    _fox_rows(k_ref[0], v_ref[0], _aug_lanes(cum, p_ref, ones_ref), ka_ref, vt_ref)


def _fox_pack(k, v, lf, tm):
    bsz, s, _ = k.shape
    p, ones = _aug_consts(False)
    return pl.pallas_call(
        _fox_pack_kernel,
        grid=(bsz, s // tm),
        in_specs=[pl.BlockSpec((1, tm, 512), lambda b, t: (b, t, 0)),
                  pl.BlockSpec((1, tm, 512), lambda b, t: (b, t, 0)),
                  pl.BlockSpec((1, tm, LANES), lambda b, t: (b, t, 0)),
                  _const_spec((3, LANES, 8 * LANES)), _const_spec((1, 8 * LANES))],
        out_specs=[pl.BlockSpec((1, 8, tm, LANES), lambda b, t: (b, 0, t, 0)),
                   pl.BlockSpec((1, 4, 1, LANES, tm), lambda b, t: (b, 0, t, 0, 0)),
                   pl.BlockSpec((1, tm, LANES), lambda b, t: (b, t, 0))],
        out_shape=[jax.ShapeDtypeStruct((bsz, 8, s, LANES), BF16),
                   jax.ShapeDtypeStruct((bsz, 4, s // tm, LANES, tm), BF16),
                   jax.ShapeDtypeStruct((bsz, s, LANES), F32)],
        scratch_shapes=[pltpu.VMEM((1, LANES), F32)],
        compiler_params=_params(("parallel", "arbitrary")),
        name="fox_pack",
    )(k, v, lf, p, ones)


def _q_rows(qq, aug, qa_ref):
    lane = _iota((1, LANES), 1)
    for h in range(8):
        pr = slice((h // 2) * LANES, (h // 2 + 1) * LANES)
        own = (lane < 64) if h % 2 == 0 else (lane >= 64)
        qa_ref[0, h] = jnp.where(own, qq[:, pr], aug[:, h * LANES:(h + 1) * LANES]).astype(BF16)


def _q_pack_kernel(q_ref, cum_ref, p_ref, ones_ref, qa_ref):
    _q_rows(q_ref[0].astype(F32), _aug_lanes(cum_ref[0], p_ref, ones_ref), qa_ref)


def _q_pack(q, cum_q, tm):
    bsz, tq, _ = q.shape
    p, ones = _aug_consts(True)
    return pl.pallas_call(
        _q_pack_kernel,
        grid=(bsz, tq // tm),
        in_specs=[pl.BlockSpec((1, tm, 512), lambda b, t: (b, t, 0)),
                  pl.BlockSpec((1, tm, LANES), lambda b, t: (b, t, 0)),
                  _const_spec((3, LANES, 8 * LANES)), _const_spec((1, 8 * LANES))],
        out_specs=pl.BlockSpec((1, 8, tm, LANES), lambda b, t: (b, 0, t, 0)),
        out_shape=jax.ShapeDtypeStruct((bsz, 8, tq, LANES), BF16),
        compiler_params=_params(("parallel", "parallel")),
        name="q_pack",
    )(q, cum_q, p, ones)


def _fox_attn_kernel(q_ref, k_ref, vt_ref, o_ref, s_sc, mt_sc, *, tq, tk, q_start):
    i = pl.program_id(2)
    q_lo = q_start + i * tq
    n_full = (q_lo + 1) // tk
    n_tiles = (q_lo + tq + tk - 1) // tk
    q_pos = q_lo + _iota((1, tq), 1)
    qs = (q_ref[0, 0], q_ref[0, 1])

    def logits(j, masked):
        off = pl.multiple_of(j * tk, tk)
        ss = [_dot_nt(k_ref[0, hh, pl.ds(off, tk), :], qs[hh]) for hh in range(2)]
        if masked:
            ok = (off + _iota((tk, 1), 0)) <= q_pos
            ss = [jnp.where(ok, s, MASKED) for s in ss]
        return ss, [jnp.max(s, axis=0, keepdims=True) for s in ss]

    def stage(ss, mts):
        for hh in range(2):
            s_sc[hh] = ss[hh]
            mt_sc[hh] = mts[hh]

    stage(*logits(0, True))

    def consume(j, carry):
        out = []
        for hh in range(2):
            m, acc = carry[hh]
            m_new = jnp.maximum(m, mt_sc[hh])
            p = jnp.exp2(s_sc[hh] - m_new).astype(BF16)
            vt = jnp.concatenate([vt_ref[0, 0, j, hh * 64:(hh + 1) * 64, :], ones], axis=0)
            acc = jnp.exp2(m - m_new) * acc + _dot(vt, p)
            out.append((m_new, acc))
        return tuple(out)

    def body(j, carry, masked):
        nxt = logits(j + 1, masked)
        out = consume(j, carry)
        stage(*nxt)
        return out

    ones = jnp.ones((ONES_ROWS, tk), BF16)
    init = (jnp.full((1, tq), NEG, F32), jnp.zeros((64 + ONES_ROWS, tq), F32))
    n_plain = jnp.maximum(n_full - 1, 0)
    carry = lax.fori_loop(0, n_plain, functools.partial(body, masked=False), (init, init))
    carry = lax.fori_loop(n_plain, n_tiles - 1, functools.partial(body, masked=True), carry)
    carry = consume(n_tiles - 1, carry)
    o_t = jnp.concatenate([acc[0:64] / acc[64:65] for _, acc in carry], axis=0)
    o_ref[0] = o_t.T.astype(BF16)


def _fox_attn(q_aug, k_aug, vt, tq, tk, q_start):
    bsz, _, t_q, _ = q_aug.shape
    s = k_aug.shape[2]
    kern = functools.partial(_fox_attn_kernel, tq=tq, tk=tk, q_start=q_start)
    return pl.pallas_call(
        kern,
        grid=(bsz, 4, t_q // tq),
        in_specs=[pl.BlockSpec((1, 2, tq, LANES), lambda b, p, i: (b, p, i, 0)),
                  pl.BlockSpec((1, 2, s, LANES), lambda b, p, i: (b, p, 0, 0)),
                  pl.BlockSpec((1, 1, s // tk, LANES, tk), lambda b, p, i: (b, p, 0, 0, 0))],
        out_specs=pl.BlockSpec((1, tq, LANES), lambda b, p, i: (b, i, p)),
        out_shape=jax.ShapeDtypeStruct((bsz, t_q, 512), BF16),
        scratch_shapes=[pltpu.VMEM((2, tk, tq), F32), pltpu.VMEM((2, 1, tq), F32)],
        compiler_params=_params(("parallel", "parallel", "arbitrary")),
        name="fox_attn",
    )(q_aug, k_aug, vt)


def _dsa_rows(dk, dv, ik, k2_ref, vt_ref, ik2_ref):
    lo = _iota((1, LANES), 1) < 64
    k2_ref[0] = dk.astype(BF16)
    vt_ref[0, 0] = dv.T.astype(BF16)
    ik = jnp.where(lo, ik, 0.0)
    ik2_ref[0, 0] = ik.astype(BF16)
    ik2_ref[0, 1] = pltpu.roll(ik, 64, 1).astype(BF16)


def _dsa_pack_kernel(dk_ref, dv_ref, ik_ref, k2_ref, vt_ref, ik2_ref):
    _dsa_rows(dk_ref[0], dv_ref[0], ik_ref[0], k2_ref, vt_ref, ik2_ref)


def _dsa_pack(dk, dv, ik, tm):
    bsz, s, _ = dk.shape
    tok = pl.BlockSpec((1, tm, LANES), lambda b, t: (b, t, 0))
    return pl.pallas_call(
        _dsa_pack_kernel,
        grid=(bsz, s // tm),
        in_specs=[tok, tok, tok],
        out_specs=[tok,
                   pl.BlockSpec((1, 1, LANES, tm), lambda b, t: (b, t, 0, 0)),
                   pl.BlockSpec((1, 2, tm, LANES), lambda b, t: (b, 0, t, 0))],
        out_shape=[jax.ShapeDtypeStruct((bsz, s, LANES), BF16),
                   jax.ShapeDtypeStruct((bsz, s // tm, LANES, tm), BF16),
                   jax.ShapeDtypeStruct((bsz, 2, s, LANES), BF16)],
        compiler_params=_params(("parallel", "parallel")),
        name="dsa_pack",
    )(dk, dv, ik)


def _cd_fused_kernel(x_ref, g_ref, w_ref, fb_ref, pk_ref, onesk_ref, pq_ref, onesq_ref,
                     ck_ref, cv_ref, dk_ref, dv_ref, misc_ref, iw_ref, iq_ref, dq_ref,
                     qa_ref, ka_ref, vt_ref, k2_ref, dvt_ref, ik2_ref, carry):
    xn = _rms(x_ref[0], g_ref[...]).astype(BF16)
    h = _dot(xn, w_ref[...])
    ck, cv = h[:, 512:1024], h[:, 1024:1536]
    dk, dv = h[:, 2560:2688], h[:, 2688:2816]
    misc = h[:, 3072:3200]
    lane = _iota((1, LANES), 1)
    is_f = (lane >= IDX_DIM) & (lane < IDX_DIM + C_HEADS)
    logf = _log_sigmoid(misc + fb_ref[...])
    misc = jnp.where(is_f, logf, misc)
    ck_ref[0], cv_ref[0], dk_ref[0], dv_ref[0], misc_ref[0] = ck, cv, dk, dv, misc
    iw_ref[0] = h[:, 3200:3328]
    iq_ref[0] = h[:, 2816:3072].astype(BF16)
    dq_ref[0] = (h[:, 1536:2560] * (D_HD ** -0.5 * LOG2E)).astype(BF16)
    cum = _running_sum(pltpu.roll(jnp.where(is_f, logf, 0.0), LANES - IDX_DIM, 1), carry)
    _fox_rows(ck, cv, _aug_lanes(cum, pk_ref, onesk_ref), ka_ref, vt_ref)
    _q_rows(h[:, 0:512] * (C_HD ** -0.5 * LOG2E), _aug_lanes(cum, pq_ref, onesq_ref), qa_ref)
    _dsa_rows(dk, dv, misc, k2_ref, dvt_ref, ik2_ref)


def _cd_fused(x, g, w, fb, tm):
    bsz, t_len, _ = x.shape
    pk, onesk = _aug_consts(False)
    pq, onesq = _aug_consts(True)
    tok = lambda w_: pl.BlockSpec((1, tm, w_), lambda b, t: (b, t, 0))
    hm = pl.BlockSpec((1, 8, tm, LANES), lambda b, t: (b, 0, t, 0))
    sds = jax.ShapeDtypeStruct
    n_t = t_len // tm
    return pl.pallas_call(
        _cd_fused_kernel,
        grid=(bsz, n_t),
        in_specs=[tok(D_MODEL), _const_spec((1, D_MODEL)), _const_spec((D_MODEL, CD_COLS)), _const_spec((1, LANES)),
                  _const_spec((3, LANES, 8 * LANES)), _const_spec((1, 8 * LANES)),
                  _const_spec((3, LANES, 8 * LANES)), _const_spec((1, 8 * LANES))],
        out_specs=[tok(512), tok(512), tok(LANES), tok(LANES), tok(LANES), tok(LANES), tok(256), tok(8 * LANES),
                   hm, hm, pl.BlockSpec((1, 4, 1, LANES, tm), lambda b, t: (b, 0, t, 0, 0)),
                   tok(LANES), pl.BlockSpec((1, 1, LANES, tm), lambda b, t: (b, t, 0, 0)),
                   pl.BlockSpec((1, 2, tm, LANES), lambda b, t: (b, 0, t, 0))],
        out_shape=[sds((bsz, t_len, 512), F32), sds((bsz, t_len, 512), F32), sds((bsz, t_len, LANES), F32),
                   sds((bsz, t_len, LANES), F32), sds((bsz, t_len, LANES), F32), sds((bsz, t_len, LANES), F32),
                   sds((bsz, t_len, 256), BF16), sds((bsz, t_len, 8 * LANES), BF16),
                   sds((bsz, 8, t_len, LANES), BF16), sds((bsz, 8, t_len, LANES), BF16),
                   sds((bsz, 4, n_t, LANES, tm), BF16), sds((bsz, t_len, LANES), BF16),
                   sds((bsz, n_t, LANES, tm), BF16), sds((bsz, 2, t_len, LANES), BF16)],
        scratch_shapes=[pltpu.VMEM((1, LANES), F32)],
        compiler_params=_params(("parallel", "arbitrary")),
        name="cd_fused",
    )(x, g, w, fb, pk, onesk, pq, onesq)


def _dsa_attn_kernel(dq_ref, iq_ref, iw_ref, ik2_ref, k2_ref, vt_ref, o_ref, keys_sc, hi_sc, lo_sc, eq_sc, m_sc, mt_sc, acc_sc,
                     s_sc,
                     *, tq, tk, q_start, s_valid, topk):
    i = pl.program_id(1)
    q_lo = q_start + i * tq
    adm_row = jnp.minimum(((q_lo + _iota((1, tq), 1)) // CHUNK + 1) * CHUNK, s_valid)
    adm_end = jnp.minimum(((q_lo + tq - 1) // CHUNK + 1) * CHUNK, s_valid)
    n_tiles = (adm_end + tk - 1) // tk
    iw_t = iw_ref[0].T
    iq = iq_ref[0]

    def score_body(j, c):
        off = pl.multiple_of(j * tk, tk)
        sc = jnp.zeros((tk, tq), F32)
        for hd in range(IDX_HEADS):
            qp = iq[:, (hd // 2) * LANES:(hd // 2 + 1) * LANES]
            r = jnp.maximum(_dot_nt(ik2_ref[0, hd % 2, pl.ds(off, tk), :], qp), 0.0)
            sc = sc + r * iw_t[hd:hd + 1, :]
        sc = sc * IDX_SCALE
        adm = (off + _iota((tk, tq), 0)) < adm_row
        bits = lax.bitcast_convert_type(sc, I32)
        bits = jnp.where(bits == INT_MIN, 0, bits)
        key = bits ^ ((bits >> 31) & 0x7FFFFFFF)
        keys_sc[j] = jnp.where(adm, key, INT_MIN)
        hi = lax.bitcast_convert_type(bits & -65536, F32)
        hi_sc[j] = jnp.where(adm, hi, -jnp.inf).astype(BF16)
        return c

    lax.fori_loop(0, n_tiles, score_body, 0)

    def pairs(one, init):
        part = lax.fori_loop(0, n_tiles // 2, lambda jj, c: one(2 * jj + 1, one(2 * jj, c)), init)
        return lax.fori_loop(2 * (n_tiles // 2), n_tiles, one, part)

    one16 = jnp.ones((tk, tq), BF16)
    zero16 = jnp.zeros((tk, tq), BF16)

    def hi_body(b, pre):
        cand = pre + lax.shift_left(jnp.int32(1), 15 - b)
        cbits = (cand ^ ((cand >> 15) & 0x7FFF)) & 0xFFFF
        subnormal = ((cbits & 0x7F80) == 0) & ((cbits & 0x007F) != 0)
        cbits = jnp.where(subnormal, jnp.where((cbits & 0x8000) != 0, 0x0000, 0x0080), cbits)
        cval = jnp.broadcast_to(lax.bitcast_convert_type(cbits << 16, F32).astype(BF16), (tk, tq))

        def one(j, c):
            hit = jnp.where(hi_sc[j] >= cval, one16, zero16).reshape(tk // 64, 4, 16, tq)
            for r in range(tk // 64):
                c = c + hit[r]
            return c
        part = pairs(one, jnp.zeros((4, 16, tq), BF16)).astype(F32)
        cnt = jnp.sum(jnp.sum(part, axis=0), axis=0, keepdims=True)
        return jnp.where(cnt >= topk, cand, pre)

    pre = lax.fori_loop(0, 16, hi_body, jnp.full((1, tq), -2 ** 15, I32))

    def lo_prep(j, c):
        kk = keys_sc[j]
        lo_sc[j] = jnp.where((kk >> 16) == pre, (kk & 0xFFFF) - 2 ** 15, -2 ** 15).astype(jnp.int16)
        hit = jnp.where((kk >> 16) > pre, 1.0, 0.0)
        return c + jnp.sum(hit.reshape(tk // 32, 4, 8, tq), axis=0)

    above = lax.fori_loop(0, n_tiles, lo_prep, jnp.zeros((4, 8, tq), F32))
    above = jnp.sum(jnp.sum(above, axis=0), axis=0, keepdims=True)
    one_i16 = jnp.ones((tk, tq), jnp.int16)
    zero_i16 = jnp.zeros((tk, tq), jnp.int16)

    def count_from(cand, strict):
        cval = jnp.broadcast_to(cand.astype(jnp.int16), (tk, tq))

        def one(j, c):
            lo = lo_sc[j]
            hit = jnp.where((lo > cval) if strict else (lo >= cval), one_i16, zero_i16).reshape(tk // 64, 4, 16, tq)
            for r in range(tk // 64):
                c = c + hit[r]
            return c
        part = pairs(one, jnp.zeros((4, 16, tq), jnp.int16)).astype(I32).astype(F32)
        return above + jnp.sum(jnp.sum(part, axis=0), axis=0, keepdims=True)

    def lo_body(b, low):
        cand = low + lax.shift_left(jnp.int32(1), 15 - b)
        return jnp.where(count_from(cand, False) >= topk, cand, low)

    low = lax.fori_loop(0, 16, lo_body, jnp.full((1, tq), -2 ** 15, I32))
    tau = (pre << 16) + (low + 2 ** 15)
    tau = jnp.where(adm_row < int(topk), INT_MIN, tau)
    need = jnp.where(tau == INT_MIN, -1.0, topk - count_from(low, True))
    eq_sc[...] = jnp.zeros(eq_sc.shape, F32)
    tri = jnp.where(_iota((tk, tk), 1) <= _iota((tk, tk), 0), 1.0, 0.0).astype(BF16)

    m_sc[...] = jnp.full(m_sc.shape, NEG, F32)
    acc_sc[...] = jnp.zeros(acc_sc.shape, F32)
    ones = jnp.ones((ONES_ROWS, tk), BF16)
    n_rep = D_HEADS // D_KV_HEADS
    q_stack = [jnp.concatenate([dq_ref[0, :, hd * LANES:(hd + 1) * LANES]
                                for hd in range(g * n_rep, (g + 1) * n_rep)], axis=0) for g in range(D_KV_HEADS)]

    def logits(j):
        kk = keys_sc[j]
        eq = kk == tau
        run = _dot(tri, jnp.where(eq, 1.0, 0.0).astype(BF16)) + eq_sc[...]
        eq_sc[...] = run[tk - 1:tk, :]
        sel = (kk > tau) | (eq & (run <= need))
        bias = jnp.where(sel, 0.0, MASKED)
        bias = jnp.concatenate([bias] * n_rep, axis=1)
        kt = k2_ref[0, pl.ds(pl.multiple_of(j * tk, tk), tk), :]
        ss = [_dot_nt(kt, q_stack[g]) + bias for g in range(D_KV_HEADS)]
        return ss, [jnp.max(s, axis=0, keepdims=True) for s in ss]

    def stage(slot, ss, mts):
        for g in range(D_KV_HEADS):
            s_sc[slot, g] = ss[g]
            mt_sc[slot, g] = mts[g]

    stage(0, *logits(0))

    def consume(j, cur):
        for g in range(D_KV_HEADS):
            m_old = m_sc[g]
            m_new = jnp.maximum(m_old, mt_sc[cur, g])
            p = jnp.exp2(s_sc[cur, g] - m_new).astype(BF16)
            vt = jnp.concatenate([vt_ref[0, j, g * 64:(g + 1) * 64, :], ones], axis=0)
            acc_sc[g] = jnp.exp2(m_old - m_new) * acc_sc[g] + _dot(vt, p)
            m_sc[g] = m_new

    def step(j, cur):
        stage(1 - cur, *logits(j + 1))
        consume(j, cur)

    def attn_body(jj, c):
        step(2 * jj, 0)
        step(2 * jj + 1, 1)
        return c

    lax.fori_loop(0, (n_tiles - 1) // 2, attn_body, 0)

    @pl.when(n_tiles % 2 == 0)
    def _():
        step(n_tiles - 2, 0)
        consume(n_tiles - 1, 1)

    @pl.when(n_tiles % 2 == 1)
    def _():
        consume(n_tiles - 1, 0)
    for g in range(D_KV_HEADS):
        o_g = acc_sc[g, 0:64] / acc_sc[g, 64:65]
        for pr in range(n_rep // 2):
            o_t = jnp.concatenate([o_g[:, (2 * pr) * tq:(2 * pr + 1) * tq],
                                   o_g[:, (2 * pr + 1) * tq:(2 * pr + 2) * tq]], axis=0)
            col = (g * n_rep // 2 + pr) * LANES
            o_ref[0, :, col:col + LANES] = o_t.T.astype(BF16)


def _dsa_attn(dq, iq, iw, ik2, k2, vt, tq, tk, q_start, s_valid):
    bsz, t_q, _ = dq.shape
    s = k2.shape[1]
    topk = min(IDX_TOPK_MAX, s_valid // 4)
    assert (s // tk) * (tk // 64) <= 256, "packed bf16 partial counts must stay exactly representable"
    kern = functools.partial(_dsa_attn_kernel, tq=tq, tk=tk, q_start=q_start, s_valid=s_valid, topk=float(topk))
    return pl.pallas_call(
        kern,
        grid=(bsz, t_q // tq),
        in_specs=[pl.BlockSpec((1, tq, 8 * LANES), lambda b, i: (b, i, 0)),
                  pl.BlockSpec((1, tq, 256), lambda b, i: (b, i, 0)),
                  pl.BlockSpec((1, tq, LANES), lambda b, i: (b, i, 0)),
                  pl.BlockSpec((1, 2, s, LANES), lambda b, i: (b, 0, 0, 0)),
                  pl.BlockSpec((1, s, LANES), lambda b, i: (b, 0, 0)),
                  pl.BlockSpec((1, s // tk, LANES, tk), lambda b, i: (b, 0, 0, 0))],
        out_specs=pl.BlockSpec((1, tq, 512), lambda b, i: (b, i, 0)),
        out_shape=jax.ShapeDtypeStruct((bsz, t_q, 512), BF16),
        scratch_shapes=[pltpu.VMEM((s // tk, tk, tq), I32),
                        pltpu.VMEM((s // tk, tk, tq), BF16),
                        pltpu.VMEM((s // tk, tk, tq), jnp.int16),
                        pltpu.VMEM((1, tq), F32),
                        pltpu.VMEM((D_KV_HEADS, 1, D_HEADS // D_KV_HEADS * tq), F32),
                        pltpu.VMEM((2, D_KV_HEADS, 1, D_HEADS // D_KV_HEADS * tq), F32),
                        pltpu.VMEM((D_KV_HEADS, 64 + ONES_ROWS, D_HEADS // D_KV_HEADS * tq), F32),
                        pltpu.VMEM((2, D_KV_HEADS, tk, D_HEADS // D_KV_HEADS * tq), F32)],
        compiler_params=_params(("parallel", "arbitrary")),
        name="dsa_attn",
    )(dq, iq, iw, ik2, k2, vt)


def _prep_ab(w_in, w_gk):
    w = jnp.pad(w_in, ((0, 0), (0, AB_COLS - w_in.shape[1]))).astype(BF16)
    wgk = jnp.pad(w_gk, ((0, LANES - B_GATE_RANK), (0, 0))).astype(BF16)
    return w, wgk


def _prep_cd(w_in, f_bias):
    o = np.cumsum([0, 512, 512, 512, C_HEADS, 512, 128, 128, 256, IDX_DIM, IDX_HEADS])
    c_q, c_k, c_v, c_f, d_q, d_k, d_v, d_iq, d_ik, d_iw = (w_in[:, o[i]:o[i + 1]] for i in range(10))
    zeros = lambda n: jnp.zeros((D_MODEL, n), w_in.dtype)
    dq_cols = []
    for h in range(D_HEADS):
        g = h // (D_HEADS // D_KV_HEADS)
        wh = d_q[:, h * 64:(h + 1) * 64]
        dq_cols += [wh, zeros(64)] if g == 0 else [zeros(64), wh]
    w = jnp.concatenate([c_q, c_k, c_v] + dq_cols + [d_k, d_v, d_iq, d_ik, c_f, zeros(LANES - IDX_DIM - C_HEADS),
                                                     d_iw, zeros(LANES - IDX_HEADS)], axis=1).astype(BF16)
    fb = jnp.pad(f_bias.astype(F32), (IDX_DIM, LANES - IDX_DIM - C_HEADS)).reshape(1, LANES)
    return w, fb


def _pad_rows(z, s):
    return jnp.pad(z, ((0, 0), (0, s - z.shape[1]), (0, 0)))


def _pad_lanes(z, n=LANES):
    return jnp.pad(z, ((0, 0), (0, 0), (0, n - z.shape[2])))


def _trunk(x, s_a, s_b, cache, wts):
    bsz, t_len, _ = x.shape
    row = lambda z: z.reshape(1, -1).astype(F32)

    t_pad = -(-t_len // REC_CHUNK) * REC_CHUNK
    x1, sa_new, sb_new = _ab_layer(
        _pad_rows(x, t_pad), s_a, s_b.reshape(bsz, 2, 128, 128), row(wts['norm_mix'][0]), wts['ab_w_in'],
        wts['gla_w_gk'], row(wts['gla_b_gk']), row(wts['lb']), row(wts['hgrn_gnorm']), row(wts['gla_gnorm']),
        wts['ab_w_out'], t_len)
    n = bsz * t_len
    x1 = x1[:, :t_len].reshape(n, D_MODEL)
    x2 = _ffn(x1, row(wts['norm_ffn'][0]), wts['ffn_w_in'][0], wts['ffn_w_out'][0])

    tk = 512
    q_start = 0 if cache is None else cache[0].shape[1]
    s_valid = q_start + t_len
    tq_pad = -(-t_len // LANES) * LANES
    if cache is None and t_len % tk == 0:
        (ck, cv, dk, dv, misc, iw, iq, dq, q_aug, k_aug, v_t, k2, dv_t, ik2) = _cd_fused(
            x2.reshape(bsz, t_len, D_MODEL), row(wts['norm_mix'][1]), wts['cd_w_in'], wts['fox_f_bias'], tm=tk)
        logf = misc[:, :, IDX_DIM:IDX_DIM + C_HEADS]
    else:
        fq, ck, cv, dq, dk, dv, iq, misc, iw = _cd_proj(x2, row(wts['norm_mix'][1]), wts['cd_w_in'],
                                                        wts['fox_f_bias'])
        per_b = lambda z: z.reshape(bsz, t_len, z.shape[-1])
        fq, ck, cv, dq, dk, dv, iq, misc, iw = map(per_b, (fq, ck, cv, dq, dk, dv, iq, misc, iw))
        logf = misc[:, :, IDX_DIM:IDX_DIM + C_HEADS]
        if cache is None:
            k_all, v_all, lf_all, dk_all, dv_all, ik_all = ck, cv, _pad_lanes(logf), dk, dv, misc
        else:
            c_k, c_v, c_lf, c_dk, c_dv, c_ik = cache
            cat = lambda c, r: jnp.concatenate([c.reshape(bsz, q_start, -1).astype(F32), r], axis=1)
            k_all, v_all, dk_all, dv_all = cat(c_k, ck), cat(c_v, cv), cat(c_dk, dk), cat(c_dv, dv)
            lf_all = _pad_lanes(cat(c_lf, logf))
            ik_all = cat(_pad_lanes(c_ik), misc)
        s_pad = -(-s_valid // tk) * tk
        k_all, v_all, lf_all, dk_all, dv_all, ik_all = (_pad_rows(z, s_pad) for z in
                                                        (k_all, v_all, lf_all, dk_all, dv_all, ik_all))
        k_aug, v_t, cum = _fox_pack(k_all, v_all, lf_all, tm=tk)
        q_aug = _q_pack(_pad_rows(fq, tq_pad), _pad_rows(cum[:, q_start:q_start + t_len], tq_pad),
                        tm=512 if tq_pad % 512 == 0 else LANES)
        k2, dv_t, ik2 = _dsa_pack(dk_all, dv_all, ik_all, tm=tk)
    o_c = _fox_attn(q_aug, k_aug, v_t, tq=512 if tq_pad % 512 == 0 else LANES, tk=tk, q_start=q_start)[:, :t_len]
    o_d = _dsa_attn(_pad_rows(dq, tq_pad), _pad_rows(iq, tq_pad), _pad_rows(iw, tq_pad), ik2, k2, dv_t,
                    tq=256 if tq_pad % 256 == 0 else LANES, tk=tk, q_start=q_start, s_valid=s_valid)[:, :t_len]

    y = _ffn(x2, row(wts['norm_ffn'][1]), wts['ffn_w_in'][1], wts['ffn_w_out'][1],
             attn=(o_c.reshape(n, 512), o_d.reshape(n, 512), wts['cd_w_out']), g_final=row(wts['norm_final']))
    rows = (ck.reshape(1, bsz, t_len, C_HEADS, C_HD), cv.reshape(1, bsz, t_len, C_HEADS, C_HD),
            logf[None], dk.reshape(1, bsz, t_len, D_KV_HEADS, D_HD), dv.reshape(1, bsz, t_len, D_KV_HEADS, D_HD),
            misc[None, :, :, :IDX_DIM])
    return (y.reshape(bsz, t_len, D_MODEL), sa_new[None], sb_new.reshape(1, bsz, B_HEADS, B_DK, B_DV)) + rows


def kernel(x_prompt, x_sample, state_hgrn, state_gla, cache_fox_k, cache_fox_v, cache_fox_logf, cache_dsa_k, cache_dsa_v, cache_dsa_ik, norm_mix, norm_ffn, norm_final, ab_w_in, ab_w_out, hgrn_lb_logits, hgrn_gnorm, gla_w_gk, gla_b_gk, gla_gnorm, cd_w_in, cd_w_out, fox_f_bias, ffn_w_in, ffn_w_out):
    lbs = jnp.cumsum(jax.nn.softmax(hgrn_lb_logits.astype(F32), axis=0), axis=0)
    w_ab, w_gk = _prep_ab(ab_w_in[0], gla_w_gk[0])
    w_cd, fb = _prep_cd(cd_w_in[0], fox_f_bias[0])
    wts = dict(norm_mix=norm_mix, norm_ffn=norm_ffn, norm_final=norm_final, ab_w_in=w_ab,
               ab_w_out=ab_w_out[0].astype(BF16), lb=lbs[0], hgrn_gnorm=hgrn_gnorm[0], gla_w_gk=w_gk,
               gla_b_gk=gla_b_gk[0], gla_gnorm=gla_gnorm[0], cd_w_in=w_cd, cd_w_out=cd_w_out[0].astype(BF16),
               fox_f_bias=fb, ffn_w_in=ffn_w_in.astype(BF16), ffn_w_out=ffn_w_out.astype(BF16))
    bp = x_prompt.shape[0]
    p_out = _trunk(x_prompt, jnp.zeros((bp, A_HEADS, A_DK, A_DV), F32), jnp.zeros((bp, B_HEADS, B_DK, B_DV), F32),
                   None, wts)
    cache = (cache_fox_k[0], cache_fox_v[0], cache_fox_logf[0], cache_dsa_k[0], cache_dsa_v[0], cache_dsa_ik[0])
    s_out = _trunk(x_sample, state_hgrn[0], state_gla[0], cache, wts)
    return (p_out[0], s_out[0]) + tuple(p_out[1:]) + tuple(s_out[1:])
```
